```python
import jax, jax.numpy as jnp
from jax import lax
import numpy as np

D_MODEL = 1024
BATCH = 2
SEQ = 8192
DEPTH = 2

MIX_WIDTH = D_MODEL
ATTN_WIDTH = MIX_WIDTH // 2
HGRN_WIDTH = MIX_WIDTH - ATTN_WIDTH
HEAD_DIM = 64
N_ATTN_HEADS = ATTN_WIDTH // HEAD_DIM
HGRN_EXPAND = 64
N_HGRN_HEADS = HGRN_WIDTH // HGRN_EXPAND
HGRN_VDIM = HGRN_WIDTH // N_HGRN_HEADS
DILATED_PATTERNS = ((128, 1), (512, 4), (2048, 16))
ATTN_BLOCK = 128
HGRN_CHUNK = 64
ROPE_THETA = 10000.0
N_EXPERTS = 16
N_GROUPS = 4
EXPERTS_PER_GROUP = N_EXPERTS // N_GROUPS
TOP_K = 2
D_FF_EXPERT = 512
RMS_EPS = 1e-6
IN_COLS = 3 * ATTN_WIDTH + 4 * HGRN_WIDTH

kernel_name = 'hybrid_dilated_attn_hgrn2_grouped_moe_adaln'


def rms_norm(x, gain=None):
    xf = x.astype(jnp.float32)
    y = xf * lax.rsqrt(jnp.mean(xf * xf, axis=-1, keepdims=True) + RMS_EPS)
    if gain is not None:
        y = y * gain.astype(jnp.float32)
    return y


def rotary(t, positions):
    half = HEAD_DIM // 2
    inv_freq = ROPE_THETA ** (-jnp.arange(half, dtype=jnp.float32) / half)
    ang = positions.astype(jnp.float32)[:, None, :, None] * inv_freq
    cos, sin = jnp.cos(ang), jnp.sin(ang)
    t1, t2 = t[..., :half], t[..., half:]
    return jnp.concatenate([t1 * cos - t2 * sin, t1 * sin + t2 * cos], axis=-1)


def dilated_window_pattern(q, k, v, window, dilation):
    B, H, S, Dh = q.shape
    steps = window // dilation
    span = dilation * ATTN_BLOCK
    s_pad = -(-S // span) * span
    m = s_pad // dilation
    nb = m // ATTN_BLOCK

    def to_streams(t):
        t = jnp.pad(t, ((0, 0), (0, 0), (0, s_pad - S), (0, 0)))
        t = t.reshape(B, H, m, dilation, Dh).transpose(0, 1, 3, 2, 4)
        return t.reshape(B, H, dilation, nb, ATTN_BLOCK, Dh)

    qs, ks, vs = to_streams(q), to_streams(k), to_streams(v)

    def with_prev(t):
        prev = jnp.pad(t, ((0, 0), (0, 0), (0, 0), (1, 0), (0, 0), (0, 0)))[:, :, :, :-1]
        return jnp.concatenate([prev, t], axis=-2)

    kk, vv = with_prev(ks), with_prev(vs)
    scores = jnp.einsum('bhrnqd,bhrnkd->bhrnqk', qs, kk) * (HEAD_DIM ** -0.5)
    qi = jnp.arange(ATTN_BLOCK)[:, None]
    kj = jnp.arange(2 * ATTN_BLOCK)[None, :]
    dist = qi + ATTN_BLOCK - kj
    in_band = (dist >= 0) & (dist <= steps)
    first_block = (jnp.arange(nb) == 0)[:, None, None]
    valid = in_band[None] & ~(first_block & (kj < ATTN_BLOCK)[None])
    scores = jnp.where(valid, scores, -jnp.inf)
    mx = jnp.max(scores, axis=-1, keepdims=True)
    p = jnp.exp(scores - mx)
    den = jnp.sum(p, axis=-1, keepdims=True)
    out = jnp.einsum('bhrnqk,bhrnkd->bhrnqd', p, vv) / den

    def from_streams(t):
        X = t.shape[-1]
        t = t.reshape(B, H, dilation, m, X).transpose(0, 1, 3, 2, 4).reshape(B, H, s_pad, X)
        return t[:, :, :S]

    return from_streams(out), from_streams(mx), from_streams(den)


def dilated_mixture_attention(q, k, v):
    results = [dilated_window_pattern(q, k, v, w, d) for (w, d) in DILATED_PATTERNS]
    out = jnp.stack([r[0] for r in results])
    mx = jnp.stack([r[1] for r in results])
    den = jnp.stack([r[2] for r in results])
    weight = den * jnp.exp(mx - jnp.max(mx, axis=0, keepdims=True))
    return jnp.sum(weight * out, axis=0) / jnp.sum(weight, axis=0)


def hgrn2_mixer(q, f_raw, i, lower_bound):
    B, S, _ = q.shape
    C = HGRN_CHUNK
    nc = S // C
    q = jax.nn.silu(q)
    f = lower_bound + (1.0 - lower_bound) * jax.nn.sigmoid(f_raw)
    log_f = jnp.log(f)
    key = 1.0 - f

    def to_chunks(t, dim):
        return t.reshape(B, nc, C, N_HGRN_HEADS, dim).transpose(1, 0, 3, 2, 4)

    qc = to_chunks(q, HGRN_EXPAND)
    kc = to_chunks(key, HGRN_EXPAND)
    gc = to_chunks(log_f, HGRN_EXPAND)
    vc = to_chunks(i, HGRN_VDIM)
    causal = jnp.tril(jnp.ones((C, C), dtype=bool))[:, :, None]

    def chunk_step(state, inp):
        q_, k_, g_, v_ = inp
        b = jnp.cumsum(g_, axis=-2)
        o_inter = jnp.einsum('bhtk,bhkv->bhtv', q_ * jnp.exp(b), state)
        diff = b[:, :, :, None, :] - b[:, :, None, :, :]
        decay = jnp.where(causal, jnp.exp(jnp.where(causal, diff, 0.0)), 0.0)
        scores = jnp.einsum('bhtk,bhsk,bhtsk->bhts', q_, k_, decay)
        o_intra = jnp.einsum('bhts,bhsv->bhtv', scores, v_)
        b_last = b[:, :, -1:, :]
        new_state = jnp.exp(b_last)[:, :, 0, :, None] * state + jnp.einsum(
            'bhsk,bhsv->bhkv', k_ * jnp.exp(b_last - b), v_)
        return new_state, o_inter + o_intra

    state0 = jnp.zeros((B, N_HGRN_HEADS, HGRN_EXPAND, HGRN_VDIM), jnp.float32)
    _, o = lax.scan(chunk_step, state0, (qc, kc, gc, vc))
    return o.transpose(1, 0, 3, 2, 4).reshape(B, S, HGRN_WIDTH)


def mixing_sublayer(h, positions, w_in, w_out, attn_norm, hgrn_norm, lower_bound):
    B, S, _ = h.shape
    proj = jnp.einsum('bsd,de->bse', h, w_in).astype(jnp.float32)
    cuts = np.cumsum([ATTN_WIDTH] * 3 + [HGRN_WIDTH] * 3).tolist()
    q_a, k_a, v_a, q_h, f_h, i_h, g_h = jnp.split(proj, cuts, axis=-1)

    def heads(t):
        return t.reshape(B, S, N_ATTN_HEADS, HEAD_DIM).transpose(0, 2, 1, 3)

    q_a = rotary(heads(q_a), positions)
    k_a = rotary(heads(k_a), positions)
    attn = dilated_mixture_attention(q_a, k_a, heads(v_a))
    attn = attn.transpose(0, 2, 1, 3).reshape(B, S, ATTN_WIDTH)
    rec = hgrn2_mixer(q_h, f_h, i_h, lower_bound)
    rec = rms_norm(rec, hgrn_norm) * jax.nn.sigmoid(g_h)
    merged = jnp.concatenate([rms_norm(attn, attn_norm), rec], axis=-1)
    return jnp.einsum('bse,ed->bsd', merged.astype(w_out.dtype), w_out)


def grouped_moe(h, w_router, b_router, w_gate, w_up, w_down):
    B, S, D = h.shape
    T = B * S
    hf = h.reshape(T, D)
    logits = (hf @ w_router).astype(jnp.float32) + b_router.astype(jnp.float32)
    probs = jax.nn.softmax(logits, axis=-1)
    pg = probs.reshape(T, N_GROUPS, EXPERTS_PER_GROUP)
    top_in_group, _ = lax.top_k(pg, TOP_K)
    g_sel = jnp.argmax(jnp.sum(top_in_group, axis=-1), axis=-1)
    in_group = jnp.einsum('tg,tge->te', jax.nn.one_hot(g_sel, N_GROUPS, dtype=jnp.float32), pg)
    vals, idx = lax.top_k(in_group, TOP_K)
    wts = vals / jnp.sum(vals, axis=-1, keepdims=True)
    expert_idx = g_sel[:, None] * EXPERTS_PER_GROUP + idx
    combine = jnp.einsum('tke,tk->te', jax.nn.one_hot(expert_idx, N_EXPERTS, dtype=jnp.float32), wts)
    y = jnp.zeros((T, D), jnp.float32)
    for e in range(N_EXPERTS):
        a = jax.nn.silu(hf @ w_gate[e]) * (hf @ w_up[e])
        y = y + combine[:, e:e + 1] * (a @ w_down[e]).astype(jnp.float32)
    return y.reshape(B, S, D)


def setup_inputs(seed: int = 0) -> dict:
    key = jax.random.key(seed)
    ks = jax.random.split(key, 16)
    D, E, F = D_MODEL, N_EXPERTS, D_FF_EXPERT
    offsets = jax.random.randint(ks[2], (BATCH, 1), 0, 1024, dtype=jnp.int32)
    positions = (offsets + jnp.arange(SEQ, dtype=jnp.int32)[None, :]).astype(jnp.int32)
    return {
        'x': jax.random.normal(ks[0], (BATCH, SEQ, D), jnp.float32),
        'c': jax.random.normal(ks[1], (BATCH, D), jnp.float32),
        'positions': positions,
        'w_in': jax.random.normal(ks[3], (DEPTH, D, IN_COLS), jnp.float32) * D ** -0.5,
        'w_out': jax.random.normal(ks[4], (DEPTH, MIX_WIDTH, D), jnp.float32) * MIX_WIDTH ** -0.5,
        'attn_norm': 1.0 + 0.05 * jax.random.normal(ks[5], (DEPTH, ATTN_WIDTH), jnp.float32),
        'hgrn_norm': 1.0 + 0.05 * jax.random.normal(ks[6], (DEPTH, HGRN_WIDTH), jnp.float32),
        'lb_params': 0.5 * jax.random.normal(ks[7], (DEPTH, HGRN_WIDTH), jnp.float32),
        'ada_w': jax.random.normal(ks[8], (DEPTH, D, 6 * D), jnp.float32) * (0.5 * D ** -0.5),
        'ada_b': 0.02 * jax.random.normal(ks[9], (DEPTH, 6 * D), jnp.float32),
        'w_router': jax.random.normal(ks[10], (D, E), jnp.float32) * D ** -0.5,
        'b_router': 0.01 * jax.random.normal(ks[11], (E,), jnp.float32),
        'w_gate': jax.random.normal(ks[12], (DEPTH, E, D, F), jnp.float32) * D ** -0.5,
        'w_up': jax.random.normal(ks[13], (DEPTH, E, D, F), jnp.float32) * D ** -0.5,
        'w_down': jax.random.normal(ks[14], (DEPTH, E, F, D), jnp.float32) * F ** -0.5,
        'final_norm': 1.0 + 0.05 * jax.random.normal(ks[15], (D,), jnp.float32),
    }


def reference(x, c, positions, w_in, w_out, attn_norm, hgrn_norm, lb_params, ada_w, ada_b,
              w_router, b_router, w_gate, w_up, w_down, final_norm):
    p = jax.nn.softmax(lb_params.astype(jnp.float32), axis=0)
    lower_bounds = jnp.cumsum(p, axis=0) - p[0:1]
    c_act = jax.nn.silu(c)
    for l in range(DEPTH):
        mod = (jnp.einsum('bd,de->be', c_act, ada_w[l]) + ada_b[l]).astype(jnp.float32)
        sh1, sc1, g1, sh2, sc2, g2 = jnp.split(mod[:, None, :], 6, axis=-1)
        h = (rms_norm(x) * (1.0 + sc1) + sh1).astype(x.dtype)
        mix = mixing_sublayer(h, positions, w_in[l], w_out[l], attn_norm[l], hgrn_norm[l],
                              lower_bounds[l])
        x = x + (g1 * mix.astype(jnp.float32)).astype(x.dtype)
        h = (rms_norm(x) * (1.0 + sc2) + sh2).astype(x.dtype)
        ffn = grouped_moe(h, w_router, b_router, w_gate[l], w_up[l], w_down[l])
        x = x + (g2 * ffn).astype(x.dtype)
    return rms_norm(x, final_norm).astype(x.dtype)
```

```python
import functools

import numpy as np
import jax
import jax.numpy as jnp
from jax import lax
from jax.experimental import pallas as pl
from jax.experimental.pallas import tpu as pltpu

D_MODEL = 1024
DEPTH = 2
ATTN_WIDTH = 512
HGRN_WIDTH = 512
HEAD_DIM = 64
N_HEADS = 8
DILATED_PATTERNS = ((128, 1), (512, 4), (2048, 16))
ATTN_BLOCK = 128
ROPE_THETA = 10000.0
N_EXPERTS = 16
N_GROUPS = 4
EXPERTS_PER_GROUP = 4
D_FF_EXPERT = 512
RMS_EPS = 1e-6
IN_COLS = 3 * ATTN_WIDTH + 4 * HGRN_WIDTH

LANES = 128
VMEM_LIMIT_BYTES = 56 * 1024 * 1024

HGRN_CHUNK = 64
HGRN_SUB = 16
NEG_BIG = -1e30

F32 = jnp.float32
BF16 = jnp.bfloat16

_NT = (((1,), (1,)), ((), ()))
_TN = (((0,), (0,)), ((), ()))


def _params(*sem):
    return pltpu.CompilerParams(dimension_semantics=sem, vmem_limit_bytes=VMEM_LIMIT_BYTES)


def _sigmoid(x):
    return 1.0 / (1.0 + jnp.exp(-x))


def _rms(x):
    return x * lax.rsqrt(jnp.mean(x * x, axis=-1, keepdims=True) + RMS_EPS)


def _split3(a):
    a1 = a.astype(BF16)
    r1 = a - a1.astype(F32)
    a2 = r1.astype(BF16)
    a3 = (r1 - a2.astype(F32)).astype(BF16)
    return a1, a2, a3


def _dot_hi(a, b, dims):
    a1, a2, _ = _split3(a)
    b1, b2, _ = _split3(b)
    d = lambda p, q: lax.dot_general(p, q, dims, preferred_element_type=F32)
    return d(a1, b1) + (d(a2, b1) + d(a1, b2))


def _ada_kernel(c_ref, w_ref, b_ref, o_ref):
    c = c_ref[...]
    ca = c * _sigmoid(c)
    o_ref[...] = _dot_hi(ca, w_ref[...], (((1,), (0,)), ((), ()))) + b_ref[...]


def _ada_mod(c_pad, ada_w, ada_b):
    depth, d, n = ada_w.shape
    tn = 1536
    rows = c_pad.shape[0]
    return pl.pallas_call(
        _ada_kernel,
        grid=(depth, n // tn),
        in_specs=[
            pl.BlockSpec((rows, d), lambda l, j: (0, 0)),
            pl.BlockSpec((None, d, tn), lambda l, j: (l, 0, j)),
            pl.BlockSpec((None, 1, tn), lambda l, j: (l, 0, j)),
        ],
        out_specs=pl.BlockSpec((None, rows, tn), lambda l, j: (l, 0, j)),
        out_shape=jax.ShapeDtypeStruct((depth, rows, n), F32),
        compiler_params=_params("parallel", "parallel"),
        name="ada_mod",
    )(c_pad, ada_w, ada_b.reshape(depth, 1, n))


def _lb_kernel(p_ref, o_ref):
    p = p_ref[...]
    e = jnp.exp(p - jnp.max(p, axis=0, keepdims=True))
    sm = e / jnp.sum(e, axis=0, keepdims=True)
    run = jnp.zeros_like(sm[0:1])
    for l in range(p.shape[0]):
        run = run + sm[l:l + 1]
        o_ref[l:l + 1, :] = run - sm[0:1]


def _lower_bounds(lb_params):
    return pl.pallas_call(
        _lb_kernel,
        out_shape=jax.ShapeDtypeStruct(lb_params.shape, F32),
        name="lower_bounds",
    )(lb_params)


def _rope_kernel(pos_ref, inv_ref, cos_ref, sin_ref):
    ang = pos_ref[...].astype(F32) * inv_ref[...]
    lane = lax.broadcasted_iota(jnp.int32, ang.shape, 1)
    first = (lane % HEAD_DIM) < (HEAD_DIM // 2)
    s = jnp.sin(ang)
    cos_ref[...] = jnp.cos(ang)
    sin_ref[...] = jnp.where(first, -s, s)


def _rope_tables(positions):
    b, s = positions.shape
    ts = 1024
    half = HEAD_DIM // 2
    inv = ROPE_THETA ** (-jnp.arange(half, dtype=F32) / half)
    inv = jnp.tile(inv, LANES // half).reshape(1, LANES)
    out = jax.ShapeDtypeStruct((b, s, LANES), F32)
    return pl.pallas_call(
        _rope_kernel,
        grid=(b, s // ts),
        in_specs=[
            pl.BlockSpec((None, ts, 1), lambda i, j: (i, j, 0)),
            pl.BlockSpec((1, LANES), lambda i, j: (0, 0)),
        ],
        out_specs=[pl.BlockSpec((None, ts, LANES), lambda i, j: (i, j, 0))] * 2,
        out_shape=[out, out],
        compiler_params=_params("parallel", "parallel"),
        name="rope_tables",
    )(positions.reshape(b, s, 1), inv)


def _inproj_kernel(x_ref, mod_ref, w_ref, cos_ref, sin_ref, lb_ref,
                   qa_ref, ka_ref, va_ref, qh_ref, kh_ref, gh_ref, ih_ref, og_ref):
    x = x_ref[...]
    mod = mod_ref[...]
    h = _rms(x) * (1.0 + mod[1:2]) + mod[0:1]
    hb = h.astype(BF16)
    reps = ATTN_WIDTH // LANES
    cos = jnp.concatenate([cos_ref[...]] * reps, axis=1)
    sin = jnp.concatenate([sin_ref[...]] * reps, axis=1)
    lane = lax.broadcasted_iota(jnp.int32, cos.shape, 1)
    first = (lane % HEAD_DIM) < (HEAD_DIM // 2)
    half = HEAD_DIM // 2

    def proj(j):
        return jnp.dot(hb, w_ref[:, j * 512:(j + 1) * 512], preferred_element_type=F32)

    def rot(t):
        swapped = jnp.where(first, pltpu.roll(t, ATTN_WIDTH - half, 1), pltpu.roll(t, half, 1))
        return t * cos + swapped * sin

    qa_ref[...] = (rot(proj(0)) * (HEAD_DIM ** -0.5)).astype(BF16)
    ka_ref[...] = rot(proj(1)).astype(BF16)
    va_ref[...] = proj(2).astype(BF16)
    qh = proj(3)
    qh_ref[...] = (qh * _sigmoid(qh)).astype(BF16)
    lb = lb_ref[...]
    f = lb + (1.0 - lb) * _sigmoid(proj(4))
    kh_ref[...] = (1.0 - f).astype(BF16)
    gh_ref[...] = jnp.log(f)
    ih_ref[...] = proj(5).astype(BF16)
    og_ref[...] = _sigmoid(proj(6)).astype(BF16)


def _inproj(x2d, mod_l, w_in_l, cos, sin, lb_l, seq):
    t, d = x2d.shape
    tm = 512
    per_batch = seq // tm
    row = lambda i: (i, 0)
    half_spec = pl.BlockSpec((tm, 512), row)
    bf = jax.ShapeDtypeStruct((t, 512), BF16)
    return pl.pallas_call(
        _inproj_kernel,
        grid=(t // tm,),
        in_specs=[
            pl.BlockSpec((tm, d), row),
            pl.BlockSpec((None, 6, d), lambda i: (i // per_batch, 0, 0)),
            pl.BlockSpec((d, IN_COLS), lambda i: (0, 0)),
            pl.BlockSpec((tm, LANES), row),
            pl.BlockSpec((tm, LANES), row),
            pl.BlockSpec((1, 512), lambda i: (0, 0)),
        ],
        out_specs=[half_spec] * 8,
        out_shape=[bf, bf, bf, bf, bf, jax.ShapeDtypeStruct((t, 512), F32), bf, bf],
        compiler_params=_params("parallel"),
        name="inproj",
    )(x2d, mod_l, w_in_l, cos, sin, lb_l)


def _attn_kernel(steps, q_ref, kc_ref, kp_ref, vc_ref, vp_ref, u_ref, st_ref):
    n = pl.program_id(2)
    blk = ATTN_BLOCK
    q = q_ref[...]
    kk = jnp.concatenate([kp_ref[...], kc_ref[...]], axis=0)
    vv = jnp.concatenate([vp_ref[...], vc_ref[...]], axis=0)
    qi = lax.broadcasted_iota(jnp.int32, (blk, 2 * blk), 0)
    kj = lax.broadcasted_iota(jnp.int32, (blk, 2 * blk), 1)
    dist = qi + blk - kj
    valid = (dist >= 0) & (dist <= steps) & ((kj >= blk) | (n > 0))
    lane = lax.broadcasted_iota(jnp.int32, (blk, LANES), 1)
    low = lane < HEAD_DIM
    st = jnp.zeros((blk, LANES), F32)
    zero = jnp.zeros((), BF16)
    for pair in range(N_HEADS // 2):
        sl = slice(pair * LANES, (pair + 1) * LANES)
        qp, kp2, vp2 = q[:, sl], kk[:, sl], vv[:, sl]
        outs = []
        for hh in range(2):
            head = 2 * pair + hh
            qm = jnp.where(low if hh == 0 else jnp.logical_not(low), qp, zero)
            s = lax.dot_general(qm, kp2, _NT, preferred_element_type=F32)
            s = jnp.where(valid, s, NEG_BIG)
            m = jnp.max(s, axis=1, keepdims=True).astype(BF16).astype(F32)
            p = jnp.exp(s - m)
            l = jnp.sum(p, axis=1, keepdims=True)
            outs.append(jnp.dot(p.astype(BF16), vp2, preferred_element_type=F32))
            st = jnp.where(lane == head, m, st)
            st = jnp.where(lane == N_HEADS + head, l, st)
        u_ref[:, sl] = jnp.where(low, outs[0], outs[1]).astype(BF16)
    st_ref[...] = st


def _attn_pattern(qa, ka, va, batch, seq, window, dilation):
    steps = window // dilation
    m = seq // dilation
    nb = m // ATTN_BLOCK
    view = lambda a: a.reshape(batch, m, dilation * ATTN_WIDTH)
    cur = pl.BlockSpec((None, ATTN_BLOCK, ATTN_WIDTH), lambda b, r, n: (b, n, r))
    prev = pl.BlockSpec((None, ATTN_BLOCK, ATTN_WIDTH), lambda b, r, n: (b, jnp.maximum(n - 1, 0), r))
    u, st = pl.pallas_call(
        functools.partial(_attn_kernel, steps),
        grid=(batch, dilation, nb),
        in_specs=[cur, cur, prev, cur, prev],
        out_specs=[cur, pl.BlockSpec((None, ATTN_BLOCK, LANES), lambda b, r, n: (b, n, r))],
        out_shape=[jax.ShapeDtypeStruct((batch, m, dilation * ATTN_WIDTH), BF16),
                   jax.ShapeDtypeStruct((batch, m, dilation * LANES), F32)],
        compiler_params=_params("parallel", "parallel", "arbitrary"),
        name=f"attn_d{dilation}",
    )(view(qa), view(ka), view(ka), view(va), view(va))
    return u.reshape(batch * seq, ATTN_WIDTH), st.reshape(batch * seq, LANES)


def _hgrn_kernel(q_ref, k_ref, g_ref, v_ref, ones_ref, o_ref, st_ref):
    c, sub, w = HGRN_CHUNK, HGRN_SUB, HGRN_WIDTH

    @pl.when(pl.program_id(1) == 0)
    def _():
        st_ref[...] = jnp.zeros_like(st_ref)

    q = q_ref[...].astype(F32)
    k = k_ref[...].astype(F32)
    vb = v_ref[...]
    v = vb.astype(F32)
    g = g_ref[...]

    ri = lax.broadcasted_iota(jnp.int32, (c, c), 0)
    ci = lax.broadcasted_iota(jnp.int32, (c, c), 1)
    tri = (ci <= ri).astype(BF16)
    g1, g2, g3 = _split3(g)
    cs = lambda t: jnp.dot(tri, t, preferred_element_type=F32)
    b = cs(g1) + (cs(g2) + cs(g3))
    b_last = b[c - 1:c, :]

    state = st_ref[...]
    o_inter = lax.dot_general((q * jnp.exp(b)).astype(BF16), state.astype(BF16), _NT,
                              preferred_element_type=F32)
    kt = (k * jnp.exp(b_last - b)).astype(BF16)
    upd = lax.dot_general(vb, kt, _TN, preferred_element_type=F32)
    r0i = lax.broadcasted_iota(jnp.int32, (w, w), 0) // HEAD_DIM
    c0i = lax.broadcasted_iota(jnp.int32, (w, w), 1) // HEAD_DIM
    st_ref[...] = state * jnp.exp(b_last) + jnp.where(r0i == c0i, upd, 0.0)

    hrow = lax.broadcasted_iota(jnp.int32, (N_HEADS * sub, w), 0) // sub
    hlane = lax.broadcasted_iota(jnp.int32, (N_HEADS * sub, w), 1) // HEAD_DIM
    hmask = hrow == hlane
    trow = lax.broadcasted_iota(jnp.int32, (sub, w), 0)
    ones = ones_ref[...]

    for blk in range(c // sub):
        r0 = blk * sub
        rows = slice(r0, r0 + sub)
        bi, qi, ki, vi = b[rows], q[rows], k[rows], v[rows]
        acc = o_inter[rows]
        if blk > 0:
            bref = b[r0:r0 + 1]
            qs = qi * jnp.exp(bi - bref)
            kp = (k[0:r0] * jnp.exp(bref - b[0:r0])).astype(BF16)
            qexp = jnp.where(hmask, jnp.concatenate([qs] * N_HEADS, axis=0), 0.0).astype(BF16)
            a = lax.dot_general(qexp, kp, _NT, preferred_element_type=F32)
            oexp = jnp.dot(a.astype(BF16), vb[0:r0], preferred_element_type=F32)
            oexp = jnp.where(hmask, oexp, 0.0)
            for hd in range(N_HEADS):
                acc = acc + oexp[hd * sub:(hd + 1) * sub]
        ws = []
        for s in range(sub):
            e = jnp.exp(bi - bi[s:s + 1])
            ws.append(jnp.where(trow >= s, qi * (ki[s:s + 1] * e), 0.0))
        wcat = jnp.concatenate(ws, axis=0).astype(BF16)
        sc = jnp.dot(wcat, ones, preferred_element_type=F32)
        for s in range(sub):
            acc = acc + sc[s * sub:(s + 1) * sub] * vi[s:s + 1]
        o_ref[rows, :] = acc


def _hgrn(qh, kh, gh, ih, ones_bd, batch, seq):
    c, w = HGRN_CHUNK, HGRN_WIDTH
    view = lambda a: a.reshape(batch, seq, w)
    blk = pl.BlockSpec((None, c, w), lambda b, n: (b, n, 0))
    o = pl.pallas_call(
        _hgrn_kernel,
        grid=(batch, seq // c),
        in_specs=[blk, blk, blk, blk, pl.BlockSpec((w, w), lambda b, n: (0, 0))],
        out_specs=blk,
        out_shape=jax.ShapeDtypeStruct((batch, seq, w), F32),
        scratch_shapes=[pltpu.VMEM((w, w), F32)],
        compiler_params=_params("parallel", "arbitrary"),
        name="hgrn2",
    )(view(qh), view(kh), view(gh), view(ih), ones_bd)
    return o.reshape(batch * seq, w)


def _route(probs):
    rows = [probs[i:i + 1, :] for i in range(N_EXPERTS)]
    gsum = []
    for gidx in range(N_GROUPS):
        a, b_, c_, d_ = rows[4 * gidx:4 * gidx + 4]
        hi1, lo1 = jnp.maximum(a, b_), jnp.minimum(a, b_)
        hi2, lo2 = jnp.maximum(c_, d_), jnp.minimum(c_, d_)
        top1 = jnp.maximum(hi1, hi2)
        second = jnp.maximum(jnp.minimum(hi1, hi2), jnp.maximum(lo1, lo2))
        gsum.append(top1 + second)
    best, gi = gsum[0], jnp.zeros(gsum[0].shape, jnp.int32)
    for gidx in range(1, N_GROUPS):
        upd = gsum[gidx] > best
        best = jnp.where(upd, gsum[gidx], best)
        gi = jnp.where(upd, gidx, gi)
    vals = []
    for j in range(EXPERTS_PER_GROUP):
        vj = rows[j]
        for gidx in range(1, N_GROUPS):
            vj = jnp.where(gi == gidx, rows[4 * gidx + j], vj)
        vals.append(vj)
    v1, i1 = vals[0], jnp.zeros(gi.shape, jnp.int32)
    for j in range(1, EXPERTS_PER_GROUP):
        upd = vals[j] > v1
        v1 = jnp.where(upd, vals[j], v1)
        i1 = jnp.where(upd, j, i1)
    v2, i2 = jnp.full(v1.shape, -1.0, F32), jnp.zeros(gi.shape, jnp.int32)
    for j in range(EXPERTS_PER_GROUP):
        upd = (i1 != j) & (vals[j] > v2)
        v2 = jnp.where(upd, vals[j], v2)
        i2 = jnp.where(upd, j, i2)
    tot = v1 + v2
    base = gi * EXPERTS_PER_GROUP
    return base + i1, base + i2, v1 / tot, v2 / tot


def _outproj_kernel(u1_ref, u2_ref, u3_ref, s1_ref, s2_ref, s3_ref, o_ref, og_ref, x_ref, mod_ref,
                    an_ref, hn_ref, w_ref, em_ref, el_ref, wr_ref, br_ref,
                    x1_ref, h2_ref, comb_ref):
    us = (u1_ref, u2_ref, u3_ref)
    ss = (s1_ref, s2_ref, s3_ref)
    em, el = em_ref[...], el_ref[...]
    ms, ls = [], []
    for s_ref in ss:
        m1, m2, _ = _split3(s_ref[...])
        ms.append(jnp.dot(m1, em, preferred_element_type=F32))
        ls.append(jnp.dot(m1, el, preferred_element_type=F32) + jnp.dot(m2, el, preferred_element_type=F32))
    mmax = jnp.maximum(jnp.maximum(ms[0], ms[1]), ms[2])
    num = jnp.zeros_like(mmax)
    den = jnp.zeros_like(mmax)
    for p in range(3):
        wgt = jnp.exp(ms[p] - mmax)
        num = num + wgt * us[p][...].astype(F32)
        den = den + wgt * ls[p]
    attn = num / den
    mod = mod_ref[...]
    a_n = _rms(attn) * an_ref[...]
    rec = _rms(o_ref[...]) * hn_ref[...] * og_ref[...].astype(F32)
    merged = jnp.concatenate([a_n, rec], axis=1).astype(BF16)
    mix = jnp.dot(merged, w_ref[...], preferred_element_type=F32)
    x1 = x_ref[...] + mod[2:3] * mix
    x1_ref[...] = x1
    h2 = _rms(x1) * (1.0 + mod[4:5]) + mod[3:4]
    h2_ref[...] = h2.astype(BF16)

    logits = _dot_hi(wr_ref[...], h2, _NT) + br_ref[...]
    e = jnp.exp(logits - jnp.max(logits, axis=0, keepdims=True))
    probs = e / jnp.sum(e, axis=0, keepdims=True)
    e1, e2, w1, w2 = _route(probs)
    tm = probs.shape[1]
    erow = lax.broadcasted_iota(jnp.int32, (LANES, tm), 0)
    comb_t = jnp.where(erow == e1, w1, 0.0) + jnp.where(erow == e2, w2, 0.0)
    comb_ref[...] = comb_t.T


def _outproj(us, sts, o_h, og, x2d, mod_l, an_l, hn_l, w_out_l, em, el, wr_t, br, seq):
    t, d = x2d.shape
    tm = 512
    per_batch = seq // tm
    row = lambda i: (i, 0)
    const = lambda i: (0, 0)
    half = pl.BlockSpec((tm, 512), row)
    stat = pl.BlockSpec((tm, LANES), row)
    full = pl.BlockSpec((tm, d), row)
    return pl.pallas_call(
        _outproj_kernel,
        grid=(t // tm,),
        in_specs=[half, half, half, stat, stat, stat, half, half, full,
                  pl.BlockSpec((None, 6, d), lambda i: (i // per_batch, 0, 0)),
                  pl.BlockSpec((1, 512), const), pl.BlockSpec((1, 512), const),
                  pl.BlockSpec((d, d), const),
                  pl.BlockSpec((LANES, 512), const), pl.BlockSpec((LANES, 512), const),
                  pl.BlockSpec((N_EXPERTS, d), const), pl.BlockSpec((N_EXPERTS, 1), const)],
        out_specs=[full, full, stat],
        out_shape=[jax.ShapeDtypeStruct((t, d), F32), jax.ShapeDtypeStruct((t, d), BF16),
                   jax.ShapeDtypeStruct((t, LANES), F32)],
        compiler_params=_params("parallel"),
        name="outproj_route",
    )(*us, *sts, o_h, og, x2d, mod_l, an_l, hn_l, w_out_l, em, el, wr_t, br)


def _moe_kernel(final, h_ref, comb_ref, wg_ref, wu_ref, wd_ref, x_ref, mod_ref, fn_ref, o_ref, acc_ref):
    e = pl.program_id(1)

    @pl.when(e == 0)
    def _():
        acc_ref[...] = jnp.zeros_like(acc_ref)

    h = h_ref[...]
    gt = jnp.dot(h, wg_ref[...], preferred_element_type=F32)
    up = jnp.dot(h, wu_ref[...], preferred_element_type=F32)
    comb = comb_ref[...]
    lane = lax.broadcasted_iota(jnp.int32, comb.shape, 1)
    col = jnp.sum(jnp.where(lane == e, comb, 0.0), axis=1, keepdims=True)
    a = (gt * _sigmoid(gt) * up * col).astype(BF16)
    acc_ref[...] += jnp.dot(a, wd_ref[...], preferred_element_type=F32)

    @pl.when(e == N_EXPERTS - 1)
    def _():
        y = x_ref[...] + mod_ref[...][5:6] * acc_ref[...]
        if final:
            y = _rms(y) * fn_ref[...]
        o_ref[...] = y


def _moe(h2, comb, wg_l, wu_l, wd_l, x1, mod_l, fnorm, seq, final):
    t, d = x1.shape
    tm = 512
    per_batch = seq // tm
    row = lambda i, e: (i, 0)
    return pl.pallas_call(
        functools.partial(_moe_kernel, final),
        grid=(t // tm, N_EXPERTS),
        in_specs=[pl.BlockSpec((tm, d), row), pl.BlockSpec((tm, LANES), row),
                  pl.BlockSpec((None, d, D_FF_EXPERT), lambda i, e: (e, 0, 0)),
                  pl.BlockSpec((None, d, D_FF_EXPERT), lambda i, e: (e, 0, 0)),
                  pl.BlockSpec((None, D_FF_EXPERT, d), lambda i, e: (e, 0, 0)),
                  pl.BlockSpec((tm, d), row),
                  pl.BlockSpec((None, 6, d), lambda i, e: (i // per_batch, 0, 0)),
                  pl.BlockSpec((1, d), lambda i, e: (0, 0))],
        out_specs=pl.BlockSpec((tm, d), row),
        out_shape=jax.ShapeDtypeStruct((t, d), F32),
        scratch_shapes=[pltpu.VMEM((tm, d), F32)],
        compiler_params=_params("parallel", "arbitrary"),
        name="moe_dense",
    )(h2, comb, wg_l, wu_l, wd_l, x1, mod_l, fnorm)


def _head_expand(offset):
    m = np.zeros((LANES, ATTN_WIDTH), np.float32)
    for h in range(N_HEADS):
        m[offset + h, h * HEAD_DIM:(h + 1) * HEAD_DIM] = 1.0
    return jnp.asarray(m, BF16)


def kernel(x, c, positions, w_in, w_out, attn_norm, hgrn_norm, lb_params, ada_w, ada_b,
           w_router, b_router, w_gate, w_up, w_down, final_norm):
    batch, seq, d = x.shape
    t = batch * seq
    c_pad = jnp.pad(c, ((0, 8 - batch), (0, 0)))
    mod = _ada_mod(c_pad, ada_w, ada_b)[:, :batch].reshape(DEPTH, batch, 6, d)
    lbs = _lower_bounds(lb_params)
    cos, sin = _rope_tables(positions)
    cos, sin = cos.reshape(t, LANES), sin.reshape(t, LANES)
    head_id = np.arange(HGRN_WIDTH) // HEAD_DIM
    ones_bd = jnp.asarray(head_id[:, None] == head_id[None, :], BF16)
    em, el = _head_expand(0), _head_expand(N_HEADS)
    wr_t = w_router.T
    br = b_router.reshape(N_EXPERTS, 1)
    fnorm = final_norm.reshape(1, d)

    x2d = x.reshape(t, d)
    for l in range(DEPTH):
        qa, ka, va, qh, kh, gh, ih, og = _inproj(
            x2d, mod[l], w_in[l].astype(BF16), cos, sin, lbs[l:l + 1], seq)
        us, sts = [], []
        for window, dilation in DILATED_PATTERNS:
            u, st = _attn_pattern(qa, ka, va, batch, seq, window, dilation)
            us.append(u)
            sts.append(st)
        o_h = _hgrn(qh, kh, gh, ih, ones_bd, batch, seq)
        x1, h2, comb = _outproj(us, sts, o_h, og, x2d, mod[l], attn_norm[l:l + 1], hgrn_norm[l:l + 1],
                                w_out[l].astype(BF16), em, el, wr_t, br, seq)
        x2d = _moe(h2, comb, w_gate[l].astype(BF16), w_up[l].astype(BF16), w_down[l].astype(BF16),
                   x1, mod[l], fnorm, seq, final=(l == DEPTH - 1))
    return x2d.reshape(batch, seq, d)
```

```python
import functools

import numpy as np
import jax
import jax.numpy as jnp
from jax import lax
from jax.experimental import pallas as pl
from jax.experimental.pallas import tpu as pltpu

D_MODEL = 1024
DEPTH = 2
ATTN_WIDTH = 512
HGRN_WIDTH = 512
HEAD_DIM = 64
N_HEADS = 8
DILATED_PATTERNS = ((128, 1), (512, 4), (2048, 16))
ATTN_BLOCK = 128
ROPE_THETA = 10000.0
N_EXPERTS = 16
N_GROUPS = 4
EXPERTS_PER_GROUP = 4
D_FF_EXPERT = 512
RMS_EPS = 1e-6
IN_COLS = 3 * ATTN_WIDTH + 4 * HGRN_WIDTH

LANES = 128
VMEM_LIMIT_BYTES = 56 * 1024 * 1024

TOKEN_TILE = 512
HGRN_CHUNK = 64
HGRN_SUB = 16
NEG_BIG = -1e30

F32 = jnp.float32
BF16 = jnp.bfloat16

_NT = (((1,), (1,)), ((), ()))
_TN = (((0,), (0,)), ((), ()))


def _params(*sem):
    return pltpu.CompilerParams(dimension_semantics=sem, vmem_limit_bytes=VMEM_LIMIT_BYTES)


def _sigmoid(x):
    return 1.0 / (1.0 + jnp.exp(-x))


def _rms(x):
    return x * lax.rsqrt(jnp.mean(x * x, axis=-1, keepdims=True) + RMS_EPS)


def _split3(a):
    a1 = a.astype(BF16)
    r1 = a - a1.astype(F32)
    a2 = r1.astype(BF16)
    a3 = (r1 - a2.astype(F32)).astype(BF16)
    return a1, a2, a3


def _dot_hi(a, b, dims):
    a1, a2, _ = _split3(a)
    b1, b2, _ = _split3(b)
    d = lambda p, q: lax.dot_general(p, q, dims, preferred_element_type=F32)
    return d(a1, b1) + (d(a2, b1) + d(a1, b2))


def _ada_kernel(c_ref, w_ref, b_ref, o_ref):
    c = c_ref[...]
    ca = c * _sigmoid(c)
    o_ref[...] = _dot_hi(ca, w_ref[...], (((1,), (0,)), ((), ()))) + b_ref[...]


def _ada_mod(c_pad, ada_w, ada_b):
    depth, d, n = ada_w.shape
    tn = 1536
    rows = c_pad.shape[0]
    return pl.pallas_call(
        _ada_kernel,
        grid=(depth, n // tn),
        in_specs=[
            pl.BlockSpec((rows, d), lambda l, j: (0, 0)),
            pl.BlockSpec((None, d, tn), lambda l, j: (l, 0, j)),
            pl.BlockSpec((None, 1, tn), lambda l, j: (l, 0, j)),
        ],
        out_specs=pl.BlockSpec((None, rows, tn), lambda l, j: (l, 0, j)),
        out_shape=jax.ShapeDtypeStruct((depth, rows, n), F32),
        compiler_params=_params("parallel", "parallel"),
        name="ada_mod",
    )(c_pad, ada_w, ada_b.reshape(depth, 1, n))


def _lb_kernel(p_ref, o_ref):
    p = p_ref[...]
    e = jnp.exp(p - jnp.max(p, axis=0, keepdims=True))
    sm = e / jnp.sum(e, axis=0, keepdims=True)
    run = jnp.zeros_like(sm[0:1])
    for l in range(p.shape[0]):
        run = run + sm[l:l + 1]
        o_ref[l:l + 1, :] = run - sm[0:1]


def _lower_bounds(lb_params):
    return pl.pallas_call(
        _lb_kernel,
        out_shape=jax.ShapeDtypeStruct(lb_params.shape, F32),
        name="lower_bounds",
    )(lb_params)


def _rope_kernel(pos_ref, inv_ref, cos_ref, sin_ref):
    ang = pos_ref[...].astype(F32) * inv_ref[...]
    lane = lax.broadcasted_iota(jnp.int32, ang.shape, 1)
    first = (lane % HEAD_DIM) < (HEAD_DIM // 2)
    s = jnp.sin(ang)
    cos_ref[...] = jnp.cos(ang)
    sin_ref[...] = jnp.where(first, -s, s)


def _rope_tables(positions):
    b, s = positions.shape
    ts = 1024
    half = HEAD_DIM // 2
    inv = ROPE_THETA ** (-jnp.arange(half, dtype=F32) / half)
    inv = jnp.tile(inv, LANES // half).reshape(1, LANES)
    out = jax.ShapeDtypeStruct((b, s, LANES), F32)
    return pl.pallas_call(
        _rope_kernel,
        grid=(b, s // ts),
        in_specs=[
            pl.BlockSpec((None, ts, 1), lambda i, j: (i, j, 0)),
            pl.BlockSpec((1, LANES), lambda i, j: (0, 0)),
        ],
        out_specs=[pl.BlockSpec((None, ts, LANES), lambda i, j: (i, j, 0))] * 2,
        out_shape=[out, out],
        compiler_params=_params("parallel", "parallel"),
        name="rope_tables",
    )(positions.reshape(b, s, 1), inv)


def _store_streams(scr_ref, val, refs):
    tm = val.shape[0]
    slabs = scr_ref.shape[0]
    for c in range(slabs):
        scr_ref[c] = val[:, c * LANES:(c + 1) * LANES]
    for (_, d), ref in zip(DILATED_PATTERNS, refs):
        if d == 1:
            ref[0] = val.astype(BF16)
        else:
            for r in range(d):
                rows = [scr_ref[c, pl.ds(r, tm // d, stride=d), :] for c in range(slabs)]
                ref[r] = jnp.concatenate(rows, axis=1).astype(BF16)


def _inproj_kernel(x_ref, mod_ref, w_ref, cos_ref, sin_ref, lb_ref,
                   q1_ref, q4_ref, q16_ref, k1_ref, k4_ref, k16_ref, v1_ref, v4_ref, v16_ref,
                   qh_ref, kh_ref, gh_ref, ih_ref, og_ref, scr_ref):
    x = x_ref[...]
    mod = mod_ref[...]
    h = _rms(x) * (1.0 + mod[1:2]) + mod[0:1]
    hb = h.astype(BF16)
    reps = ATTN_WIDTH // LANES
    cos = jnp.concatenate([cos_ref[...]] * reps, axis=1)
    sin = jnp.concatenate([sin_ref[...]] * reps, axis=1)
    lane = lax.broadcasted_iota(jnp.int32, cos.shape, 1)
    first = (lane % HEAD_DIM) < (HEAD_DIM // 2)
    half = HEAD_DIM // 2

    def proj(j):
        return jnp.dot(hb, w_ref[:, j * 512:(j + 1) * 512], preferred_element_type=F32)

    def rot(t):
        swapped = jnp.where(first, pltpu.roll(t, ATTN_WIDTH - half, 1), pltpu.roll(t, half, 1))
        return t * cos + swapped * sin

    _store_streams(scr_ref, rot(proj(0)) * (HEAD_DIM ** -0.5), (q1_ref, q4_ref, q16_ref))
    _store_streams(scr_ref, rot(proj(1)), (k1_ref, k4_ref, k16_ref))
    _store_streams(scr_ref, proj(2), (v1_ref, v4_ref, v16_ref))
    qh = proj(3)
    qh_ref[...] = (qh * _sigmoid(qh)).astype(BF16)
    lb = lb_ref[...]
    f = lb + (1.0 - lb) * _sigmoid(proj(4))
    kh_ref[...] = (1.0 - f).astype(BF16)
    gh_ref[...] = jnp.log(f)
    ih_ref[...] = proj(5).astype(BF16)
    og_ref[...] = _sigmoid(proj(6)).astype(BF16)


def _stream_spec(tm, dil, per_batch, width):
    return pl.BlockSpec((None, dil, tm // dil, width), lambda i: (i // per_batch, 0, i % per_batch, 0))


def _inproj(x2d, mod_l, w_in_l, cos, sin, lb_l, seq):
    t, d = x2d.shape
    tm = TOKEN_TILE
    per_batch = seq // tm
    batch = t // seq
    row = lambda i: (i, 0)
    half_spec = pl.BlockSpec((tm, 512), row)
    bf = jax.ShapeDtypeStruct((t, 512), BF16)
    stream_specs = [_stream_spec(tm, dil, per_batch, ATTN_WIDTH) for _, dil in DILATED_PATTERNS]
    stream_shapes = [jax.ShapeDtypeStruct((batch, dil, seq // dil, ATTN_WIDTH), BF16)
                     for _, dil in DILATED_PATTERNS]
    return pl.pallas_call(
        _inproj_kernel,
        grid=(t // tm,),
        in_specs=[
            pl.BlockSpec((tm, d), row),
            pl.BlockSpec((None, 6, d), lambda i: (i // per_batch, 0, 0)),
            pl.BlockSpec((d, IN_COLS), lambda i: (0, 0)),
            pl.BlockSpec((tm, LANES), row),
            pl.BlockSpec((tm, LANES), row),
            pl.BlockSpec((1, 512), lambda i: (0, 0)),
        ],
        out_specs=stream_specs * 3 + [half_spec] * 5,
        out_shape=stream_shapes * 3 + [bf, bf, jax.ShapeDtypeStruct((t, 512), F32), bf, bf],
        scratch_shapes=[pltpu.VMEM((ATTN_WIDTH // LANES, tm, LANES), F32)],
        compiler_params=_params("parallel"),
        name="inproj",
    )(x2d, mod_l, w_in_l, cos, sin, lb_l)


def _attn_kernel(steps, q_ref, kc_ref, kp_ref, vc_ref, vp_ref, u_ref, st_ref):
    n = pl.program_id(2)
    blk = ATTN_BLOCK
    q = q_ref[...]
    kk = jnp.concatenate([kp_ref[...], kc_ref[...]], axis=0)
    vv = jnp.concatenate([vp_ref[...], vc_ref[...]], axis=0)
    qi = lax.broadcasted_iota(jnp.int32, (blk, 2 * blk), 0)
    kj = lax.broadcasted_iota(jnp.int32, (blk, 2 * blk), 1)
    dist = qi + blk - kj
    valid = (dist >= 0) & (dist <= steps) & ((kj >= blk) | (n > 0))
    lane = lax.broadcasted_iota(jnp.int32, (blk, LANES), 1)
    low = lane < HEAD_DIM
    st = jnp.zeros((blk, LANES), F32)
    zero = jnp.zeros((), BF16)
    for pair in range(N_HEADS // 2):
        sl = slice(pair * LANES, (pair + 1) * LANES)
        qp, kp2, vp2 = q[:, sl], kk[:, sl], vv[:, sl]
        outs = []
        for hh in range(2):
            head = 2 * pair + hh
            qm = jnp.where(low if hh == 0 else jnp.logical_not(low), qp, zero)
            s = lax.dot_general(qm, kp2, _NT, preferred_element_type=F32)
            s = jnp.where(valid, s, NEG_BIG)
            m = jnp.max(s, axis=1, keepdims=True).astype(BF16).astype(F32)
            p = jnp.exp(s - m)
            l = jnp.sum(p, axis=1, keepdims=True)
            outs.append(jnp.dot(p.astype(BF16), vp2, preferred_element_type=F32))
            st = jnp.where(lane == head, m, st)
            st = jnp.where(lane == N_HEADS + head, l, st)
        u_ref[:, sl] = jnp.where(low, outs[0], outs[1]).astype(BF16)
    st_ref[...] = st


def _attn_pattern(qs, ks, vs, window, dilation):
    batch, _, m, _ = qs.shape
    steps = window // dilation
    nb = m // ATTN_BLOCK
    cur = pl.BlockSpec((None, None, ATTN_BLOCK, ATTN_WIDTH), lambda b, r, n: (b, r, n, 0))
    prev = pl.BlockSpec((None, None, ATTN_BLOCK, ATTN_WIDTH),
                        lambda b, r, n: (b, r, jnp.maximum(n - 1, 0), 0))
    return pl.pallas_call(
        functools.partial(_attn_kernel, steps),
        grid=(batch, dilation, nb),
        in_specs=[cur, cur, prev, cur, prev],
        out_specs=[cur, pl.BlockSpec((None, None, ATTN_BLOCK, LANES), lambda b, r, n: (b, r, n, 0))],
        out_shape=[jax.ShapeDtypeStruct((batch, dilation, m, ATTN_WIDTH), BF16),
                   jax.ShapeDtypeStruct((batch, dilation, m, LANES), F32)],
        compiler_params=_params("parallel", "parallel", "arbitrary"),
        name=f"attn_d{dilation}",
    )(qs, ks, ks, vs, vs)


def _hgrn_kernel(q_ref, k_ref, g_ref, v_ref, ones_ref, o_ref, st_ref):
    c, sub, w = HGRN_CHUNK, HGRN_SUB, HGRN_WIDTH

    @pl.when(pl.program_id(1) == 0)
    def _():
        st_ref[...] = jnp.zeros_like(st_ref)

    q = q_ref[...].astype(F32)
    k = k_ref[...].astype(F32)
    vb = v_ref[...]
    v = vb.astype(F32)
    g = g_ref[...]

    ri = lax.broadcasted_iota(jnp.int32, (c, c), 0)
    ci = lax.broadcasted_iota(jnp.int32, (c, c), 1)
    tri = (ci <= ri).astype(BF16)
    g1, g2, g3 = _split3(g)
    cs = lambda t: jnp.dot(tri, t, preferred_element_type=F32)
    b = cs(g1) + (cs(g2) + cs(g3))
    b_last = b[c - 1:c, :]

    state = st_ref[...]
    o_inter = lax.dot_general((q * jnp.exp(b)).astype(BF16), state.astype(BF16), _NT,
                              preferred_element_type=F32)
    kt = (k * jnp.exp(b_last - b)).astype(BF16)
    upd = lax.dot_general(vb, kt, _TN, preferred_element_type=F32)
    r0i = lax.broadcasted_iota(jnp.int32, (w, w), 0) // HEAD_DIM
    c0i = lax.broadcasted_iota(jnp.int32, (w, w), 1) // HEAD_DIM
    st_ref[...] = state * jnp.exp(b_last) + jnp.where(r0i == c0i, upd, 0.0)

    hrow = lax.broadcasted_iota(jnp.int32, (N_HEADS * sub, w), 0) // sub
    hlane = lax.broadcasted_iota(jnp.int32, (N_HEADS * sub, w), 1) // HEAD_DIM
    hmask = hrow == hlane
    trow = lax.broadcasted_iota(jnp.int32, (sub, w), 0)
    ones = ones_ref[...]

    for blk in range(c // sub):
        r0 = blk * sub
        rows = slice(r0, r0 + sub)
        bi, qi, ki, vi = b[rows], q[rows], k[rows], v[rows]
        acc = o_inter[rows]
        if blk > 0:
            bref = b[r0:r0 + 1]
            qs = qi * jnp.exp(bi - bref)
            kp = (k[0:r0] * jnp.exp(bref - b[0:r0])).astype(BF16)
            qexp = jnp.where(hmask, jnp.concatenate([qs] * N_HEADS, axis=0), 0.0).astype(BF16)
            a = lax.dot_general(qexp, kp, _NT, preferred_element_type=F32)
            oexp = jnp.dot(a.astype(BF16), vb[0:r0], preferred_element_type=F32)
            oexp = jnp.where(hmask, oexp, 0.0)
            for hd in range(N_HEADS):
                acc = acc + oexp[hd * sub:(hd + 1) * sub]
        ws = []
        for s in range(sub):
            e = jnp.exp(bi - bi[s:s + 1])
            ws.append(jnp.where(trow >= s, qi * (ki[s:s + 1] * e), 0.0))
        wcat = jnp.concatenate(ws, axis=0).astype(BF16)
        sc = jnp.dot(wcat, ones, preferred_element_type=F32)
        for s in range(sub):
            acc = acc + sc[s * sub:(s + 1) * sub] * vi[s:s + 1]
        o_ref[rows, :] = acc


def _hgrn(qh, kh, gh, ih, ones_bd, batch, seq):
    c, w = HGRN_CHUNK, HGRN_WIDTH
    view = lambda a: a.reshape(batch, seq, w)
    blk = pl.BlockSpec((None, c, w), lambda b, n: (b, n, 0))
    o = pl.pallas_call(
        _hgrn_kernel,
        grid=(batch, seq // c),
        in_specs=[blk, blk, blk, blk, pl.BlockSpec((w, w), lambda b, n: (0, 0))],
        out_specs=blk,
        out_shape=jax.ShapeDtypeStruct((batch, seq, w), F32),
        scratch_shapes=[pltpu.VMEM((w, w), F32)],
        compiler_params=_params("parallel", "arbitrary"),
        name="hgrn2",
    )(view(qh), view(kh), view(gh), view(ih), ones_bd)
    return o.reshape(batch * seq, w)


def _route(probs):
    rows = [probs[i:i + 1, :] for i in range(N_EXPERTS)]
    gsum = []
    for gidx in range(N_GROUPS):
        a, b_, c_, d_ = rows[4 * gidx:4 * gidx + 4]
        hi1, lo1 = jnp.maximum(a, b_), jnp.minimum(a, b_)
        hi2, lo2 = jnp.maximum(c_, d_), jnp.minimum(c_, d_)
        top1 = jnp.maximum(hi1, hi2)
        second = jnp.maximum(jnp.minimum(hi1, hi2), jnp.maximum(lo1, lo2))
        gsum.append(top1 + second)
    best, gi = gsum[0], jnp.zeros(gsum[0].shape, jnp.int32)
    for gidx in range(1, N_GROUPS):
        upd = gsum[gidx] > best
        best = jnp.where(upd, gsum[gidx], best)
        gi = jnp.where(upd, gidx, gi)
    vals = []
    for j in range(EXPERTS_PER_GROUP):
        vj = rows[j]
        for gidx in range(1, N_GROUPS):
            vj = jnp.where(gi == gidx, rows[4 * gidx + j], vj)
        vals.append(vj)
    v1, i1 = vals[0], jnp.zeros(gi.shape, jnp.int32)
    for j in range(1, EXPERTS_PER_GROUP):
        upd = vals[j] > v1
        v1 = jnp.where(upd, vals[j], v1)
        i1 = jnp.where(upd, j, i1)
    v2, i2 = jnp.full(v1.shape, -1.0, F32), jnp.zeros(gi.shape, jnp.int32)
    for j in range(EXPERTS_PER_GROUP):
        upd = (i1 != j) & (vals[j] > v2)
        v2 = jnp.where(upd, vals[j], v2)
        i2 = jnp.where(upd, j, i2)
    tot = v1 + v2
    base = gi * EXPERTS_PER_GROUP
    return base + i1, base + i2, v1 / tot, v2 / tot


def _outproj_kernel(u1_ref, u2_ref, u3_ref, s1_ref, s2_ref, s3_ref, o_ref, og_ref, x_ref, mod_ref,
                    an_ref, hn_ref, w_ref, em_ref, el_ref, wr_ref, br_ref,
                    x1_ref, h2_ref, comb_ref, uscr_ref, sscr_ref):
    us = (u1_ref, u2_ref, u3_ref)
    ss = (s1_ref, s2_ref, s3_ref)
    em, el = em_ref[...], el_ref[...]

    def natural(ref, scr, dil):
        if dil == 1:
            return ref[0].astype(F32)
        slabs = scr.shape[0]
        for r in range(dil):
            val = ref[r].astype(F32)
            for c in range(slabs):
                scr[c, pl.ds(r, ref.shape[1], stride=dil), :] = val[:, c * LANES:(c + 1) * LANES]
        return jnp.concatenate([scr[c] for c in range(slabs)], axis=1)

    ms, ls = [], []
    for s_ref, (_, dil) in zip(ss, DILATED_PATTERNS):
        m1, m2, _ = _split3(natural(s_ref, sscr_ref, dil))
        ms.append(jnp.dot(m1, em, preferred_element_type=F32))
        ls.append(jnp.dot(m1, el, preferred_element_type=F32) + jnp.dot(m2, el, preferred_element_type=F32))
    mmax = jnp.maximum(jnp.maximum(ms[0], ms[1]), ms[2])
    num = jnp.zeros_like(mmax)
    den = jnp.zeros_like(mmax)
    for p in range(3):
        wgt = jnp.exp(ms[p] - mmax)
        num = num + wgt * natural(us[p], uscr_ref, DILATED_PATTERNS[p][1])
        den = den + wgt * ls[p]
    attn = num / den
    mod = mod_ref[...]
    a_n = _rms(attn) * an_ref[...]
    rec = _rms(o_ref[...]) * hn_ref[...] * og_ref[...].astype(F32)
    merged = jnp.concatenate([a_n, rec], axis=1).astype(BF16)
    mix = jnp.dot(merged, w_ref[...], preferred_element_type=F32)
    x1 = x_ref[...] + mod[2:3] * mix
    x1_ref[...] = x1
    h2 = _rms(x1) * (1.0 + mod[4:5]) + mod[3:4]
    h2_ref[...] = h2.astype(BF16)

    logits = _dot_hi(wr_ref[...], h2, _NT) + br_ref[...]
    e = jnp.exp(logits - jnp.max(logits, axis=0, keepdims=True))
    probs = e / jnp.sum(e, axis=0, keepdims=True)
    e1, e2, w1, w2 = _route(probs)
    tm = probs.shape[1]
    erow = lax.broadcasted_iota(jnp.int32, (LANES, tm), 0)
    comb_t = jnp.where(erow == e1, w1, 0.0) + jnp.where(erow == e2, w2, 0.0)
    comb_ref[...] = comb_t.T


def _outproj(us, sts, o_h, og, x2d, mod_l, an_l, hn_l, w_out_l, em, el, wr_t, br, seq):
    t, d = x2d.shape
    tm = TOKEN_TILE
    per_batch = seq // tm
    row = lambda i: (i, 0)
    const = lambda i: (0, 0)
    half = pl.BlockSpec((tm, 512), row)
    stat = pl.BlockSpec((tm, LANES), row)
    full = pl.BlockSpec((tm, d), row)
    u_specs = [_stream_spec(tm, dil, per_batch, ATTN_WIDTH) for _, dil in DILATED_PATTERNS]
    s_specs = [_stream_spec(tm, dil, per_batch, LANES) for _, dil in DILATED_PATTERNS]
    return pl.pallas_call(
        _outproj_kernel,
        grid=(t // tm,),
        in_specs=u_specs + s_specs + [half, half, full,
                  pl.BlockSpec((None, 6, d), lambda i: (i // per_batch, 0, 0)),
                  pl.BlockSpec((1, 512), const), pl.BlockSpec((1, 512), const),
                  pl.BlockSpec((d, d), const),
                  pl.BlockSpec((LANES, 512), const), pl.BlockSpec((LANES, 512), const),
                  pl.BlockSpec((N_EXPERTS, d), const), pl.BlockSpec((N_EXPERTS, 1), const)],
        out_specs=[full, full, stat],
        out_shape=[jax.ShapeDtypeStruct((t, d), F32), jax.ShapeDtypeStruct((t, d), BF16),
                   jax.ShapeDtypeStruct((t, LANES), F32)],
        scratch_shapes=[pltpu.VMEM((ATTN_WIDTH // LANES, tm, LANES), F32), pltpu.VMEM((1, tm, LANES), F32)],
        compiler_params=_params("parallel"),
        name="outproj_route",
    )(*us, *sts, o_h, og, x2d, mod_l, an_l, hn_l, w_out_l, em, el, wr_t, br)


def _moe_kernel(final, h_ref, comb_ref, wg_ref, wu_ref, wd_ref, x_ref, mod_ref, fn_ref, o_ref, acc_ref):
    e = pl.program_id(1)

    @pl.when(e == 0)
    def _():
        acc_ref[...] = jnp.zeros_like(acc_ref)

    h = h_ref[...]
    gt = jnp.dot(h, wg_ref[...], preferred_element_type=F32)
    up = jnp.dot(h, wu_ref[...], preferred_element_type=F32)
    comb = comb_ref[...]
    lane = lax.broadcasted_iota(jnp.int32, comb.shape, 1)
    col = jnp.sum(jnp.where(lane == e, comb, 0.0), axis=1, keepdims=True)
    a = (gt * _sigmoid(gt) * up * col).astype(BF16)
    acc_ref[...] += jnp.dot(a, wd_ref[...], preferred_element_type=F32)

    @pl.when(e == N_EXPERTS - 1)
    def _():
        y = x_ref[...] + mod_ref[...][5:6] * acc_ref[...]
        if final:
            y = _rms(y) * fn_ref[...]
        o_ref[...] = y


def _moe(h2, comb, wg_l, wu_l, wd_l, x1, mod_l, fnorm, seq, final):
    t, d = x1.shape
    tm = 512
    per_batch = seq // tm
    row = lambda i, e: (i, 0)
    return pl.pallas_call(
        functools.partial(_moe_kernel, final),
        grid=(t // tm, N_EXPERTS),
        in_specs=[pl.BlockSpec((tm, d), row), pl.BlockSpec((tm, LANES), row),
                  pl.BlockSpec((None, d, D_FF_EXPERT), lambda i, e: (e, 0, 0)),
                  pl.BlockSpec((None, d, D_FF_EXPERT), lambda i, e: (e, 0, 0)),
                  pl.BlockSpec((None, D_FF_EXPERT, d), lambda i, e: (e, 0, 0)),
                  pl.BlockSpec((tm, d), row),
                  pl.BlockSpec((None, 6, d), lambda i, e: (i // per_batch, 0, 0)),
                  pl.BlockSpec((1, d), lambda i, e: (0, 0))],
        out_specs=pl.BlockSpec((tm, d), row),
        out_shape=jax.ShapeDtypeStruct((t, d), F32),
        scratch_shapes=[pltpu.VMEM((tm, d), F32)],
        compiler_params=_params("parallel", "arbitrary"),
        name="moe_dense",
    )(h2, comb, wg_l, wu_l, wd_l, x1, mod_l, fnorm)


def _head_expand(offset):
    m = np.zeros((LANES, ATTN_WIDTH), np.float32)
    for h in range(N_HEADS):
        m[offset + h, h * HEAD_DIM:(h + 1) * HEAD_DIM] = 1.0
    return jnp.asarray(m, BF16)


def kernel(x, c, positions, w_in, w_out, attn_norm, hgrn_norm, lb_params, ada_w, ada_b,
           w_router, b_router, w_gate, w_up, w_down, final_norm):
    batch, seq, d = x.shape
    t = batch * seq
    c_pad = jnp.pad(c, ((0, 8 - batch), (0, 0)))
    mod = _ada_mod(c_pad, ada_w, ada_b)[:, :batch].reshape(DEPTH, batch, 6, d)
    lbs = _lower_bounds(lb_params)
    cos, sin = _rope_tables(positions)
    cos, sin = cos.reshape(t, LANES), sin.reshape(t, LANES)
    head_id = np.arange(HGRN_WIDTH) // HEAD_DIM
    ones_bd = jnp.asarray(head_id[:, None] == head_id[None, :], BF16)
    em, el = _head_expand(0), _head_expand(N_HEADS)
    wr_t = w_router.T
    br = b_router.reshape(N_EXPERTS, 1)
    fnorm = final_norm.reshape(1, d)

    x2d = x.reshape(t, d)
    for l in range(DEPTH):
        outs = _inproj(x2d, mod[l], w_in[l].astype(BF16), cos, sin, lbs[l:l + 1], seq)
        qs, ks, vs = outs[0:3], outs[3:6], outs[6:9]
        qh, kh, gh, ih, og = outs[9:]
        us, sts = [], []
        for p, (window, dilation) in enumerate(DILATED_PATTERNS):
            u, st = _attn_pattern(qs[p], ks[p], vs[p], window, dilation)
            us.append(u)
            sts.append(st)
        o_h = _hgrn(qh, kh, gh, ih, ones_bd, batch, seq)
        x1, h2, comb = _outproj(us, sts, o_h, og, x2d, mod[l], attn_norm[l:l + 1], hgrn_norm[l:l + 1],
                                w_out[l].astype(BF16), em, el, wr_t, br, seq)
        x2d = _moe(h2, comb, w_gate[l].astype(BF16), w_up[l].astype(BF16), w_down[l].astype(BF16),
                   x1, mod[l], fnorm, seq, final=(l == DEPTH - 1))
    return x2d.reshape(batch, seq, d)
```

```python
import functools

import numpy as np
import jax
import jax.numpy as jnp
from jax import lax
from jax.experimental import pallas as pl
from jax.experimental.pallas import tpu as pltpu

D_MODEL = 1024
DEPTH = 2
ATTN_WIDTH = 512
HGRN_WIDTH = 512
HEAD_DIM = 64
N_HEADS = 8
DILATED_PATTERNS = ((128, 1), (512, 4), (2048, 16))
ATTN_BLOCK = 128
ROPE_THETA = 10000.0
N_EXPERTS = 16
N_GROUPS = 4
EXPERTS_PER_GROUP = 4
D_FF_EXPERT = 512
RMS_EPS = 1e-6
IN_COLS = 3 * ATTN_WIDTH + 4 * HGRN_WIDTH

LANES = 128
VMEM_LIMIT_BYTES = 56 * 1024 * 1024

TOKEN_TILE = 512
N_ASSIGN = 2
MOE_TILE = 512
POS_TILE = 1024
DMA_TILE = 512
HGRN_CHUNK = 64
HGRN_SUB = 16
NEG_BIG = -1e30

F32 = jnp.float32
BF16 = jnp.bfloat16

_NT = (((1,), (1,)), ((), ()))
_TN = (((0,), (0,)), ((), ()))


def _params(*sem):
    return pltpu.CompilerParams(dimension_semantics=sem, vmem_limit_bytes=VMEM_LIMIT_BYTES)


def _sigmoid(x):
    return 1.0 / (1.0 + jnp.exp(-x))


def _rms(x):
    return x * lax.rsqrt(jnp.mean(x * x, axis=-1, keepdims=True) + RMS_EPS)


def _split3(a):
    a1 = a.astype(BF16)
    r1 = a - a1.astype(F32)
    a2 = r1.astype(BF16)
    a3 = (r1 - a2.astype(F32)).astype(BF16)
    return a1, a2, a3


def _dot_hi(a, b, dims):
    a1, a2, _ = _split3(a)
    b1, b2, _ = _split3(b)
    d = lambda p, q: lax.dot_general(p, q, dims, preferred_element_type=F32)
    return d(a1, b1) + (d(a2, b1) + d(a1, b2))


def _ada_kernel(c_ref, w_ref, b_ref, o_ref):
    c = c_ref[...]
    ca = c * _sigmoid(c)
    o_ref[...] = _dot_hi(ca, w_ref[...], (((1,), (0,)), ((), ()))) + b_ref[...]


def _ada_mod(c_pad, ada_w, ada_b):
    depth, d, n = ada_w.shape
    tn = 1536
    rows = c_pad.shape[0]
    return pl.pallas_call(
        _ada_kernel,
        grid=(depth, n // tn),
        in_specs=[
            pl.BlockSpec((rows, d), lambda l, j: (0, 0)),
            pl.BlockSpec((None, d, tn), lambda l, j: (l, 0, j)),
            pl.BlockSpec((None, 1, tn), lambda l, j: (l, 0, j)),
        ],
        out_specs=pl.BlockSpec((None, rows, tn), lambda l, j: (l, 0, j)),
        out_shape=jax.ShapeDtypeStruct((depth, rows, n), F32),
        compiler_params=_params("parallel", "parallel"),
        name="ada_mod",
    )(c_pad, ada_w, ada_b.reshape(depth, 1, n))


def _lb_kernel(p_ref, o_ref):
    p = p_ref[...]
    e = jnp.exp(p - jnp.max(p, axis=0, keepdims=True))
    sm = e / jnp.sum(e, axis=0, keepdims=True)
    run = jnp.zeros_like(sm[0:1])
    for l in range(p.shape[0]):
        run = run + sm[l:l + 1]
        o_ref[l:l + 1, :] = run - sm[0:1]


def _lower_bounds(lb_params):
    return pl.pallas_call(
        _lb_kernel,
        out_shape=jax.ShapeDtypeStruct(lb_params.shape, F32),
        name="lower_bounds",
    )(lb_params)


def _rope_kernel(pos_ref, inv_ref, cos_ref, sin_ref):
    ang = pos_ref[...].astype(F32) * inv_ref[...]
    lane = lax.broadcasted_iota(jnp.int32, ang.shape, 1)
    first = (lane % HEAD_DIM) < (HEAD_DIM // 2)
    s = jnp.sin(ang)
    cos_ref[...] = jnp.cos(ang)
    sin_ref[...] = jnp.where(first, -s, s)


def _rope_tables(positions):
    b, s = positions.shape
    ts = 1024
    half = HEAD_DIM // 2
    inv = ROPE_THETA ** (-jnp.arange(half, dtype=F32) / half)
    inv = jnp.tile(inv, LANES // half).reshape(1, LANES)
    out = jax.ShapeDtypeStruct((b, s, LANES), F32)
    return pl.pallas_call(
        _rope_kernel,
        grid=(b, s // ts),
        in_specs=[
            pl.BlockSpec((None, ts, 1), lambda i, j: (i, j, 0)),
            pl.BlockSpec((1, LANES), lambda i, j: (0, 0)),
        ],
        out_specs=[pl.BlockSpec((None, ts, LANES), lambda i, j: (i, j, 0))] * 2,
        out_shape=[out, out],
        compiler_params=_params("parallel", "parallel"),
        name="rope_tables",
    )(positions.reshape(b, s, 1), inv)


def _store_streams(scr_ref, val, refs):
    tm = val.shape[0]
    slabs = scr_ref.shape[0]
    for c in range(slabs):
        scr_ref[c] = val[:, c * LANES:(c + 1) * LANES]
    for (_, d), ref in zip(DILATED_PATTERNS, refs):
        if d == 1:
            ref[0] = val.astype(BF16)
        else:
            for r in range(d):
                rows = [scr_ref[c, pl.ds(r, tm // d, stride=d), :] for c in range(slabs)]
                ref[r] = jnp.concatenate(rows, axis=1).astype(BF16)


def _inproj_kernel(x_ref, mod_ref, w_ref, cos_ref, sin_ref, lb_ref,
                   q1_ref, q4_ref, q16_ref, k1_ref, k4_ref, k16_ref, v1_ref, v4_ref, v16_ref,
                   qh_ref, kh_ref, gh_ref, ih_ref, og_ref, scr_ref):
    x = x_ref[...]
    mod = mod_ref[...]
    h = _rms(x) * (1.0 + mod[1:2]) + mod[0:1]
    hb = h.astype(BF16)
    reps = ATTN_WIDTH // LANES
    cos = jnp.concatenate([cos_ref[...]] * reps, axis=1)
    sin = jnp.concatenate([sin_ref[...]] * reps, axis=1)
    lane = lax.broadcasted_iota(jnp.int32, cos.shape, 1)
    first = (lane % HEAD_DIM) < (HEAD_DIM // 2)
    half = HEAD_DIM // 2

    def proj(j):
        return jnp.dot(hb, w_ref[:, j * 512:(j + 1) * 512], preferred_element_type=F32)

    def rot(t):
        swapped = jnp.where(first, pltpu.roll(t, ATTN_WIDTH - half, 1), pltpu.roll(t, half, 1))
        return t * cos + swapped * sin

    _store_streams(scr_ref, rot(proj(0)) * (HEAD_DIM ** -0.5), (q1_ref, q4_ref, q16_ref))
    _store_streams(scr_ref, rot(proj(1)), (k1_ref, k4_ref, k16_ref))
    _store_streams(scr_ref, proj(2), (v1_ref, v4_ref, v16_ref))
    qh = proj(3)
    qh_ref[...] = (qh * _sigmoid(qh)).astype(BF16)
    lb = lb_ref[...]
    f = lb + (1.0 - lb) * _sigmoid(proj(4))
    kh_ref[...] = (1.0 - f).astype(BF16)
    gh_ref[...] = jnp.log(f)
    ih_ref[...] = proj(5).astype(BF16)
    og_ref[...] = _sigmoid(proj(6)).astype(BF16)


def _stream_spec(tm, dil, per_batch, width):
    return pl.BlockSpec((None, dil, tm // dil, width), lambda i: (i // per_batch, 0, i % per_batch, 0))


def _inproj(x2d, mod_l, w_in_l, cos, sin, lb_l, seq):
    t, d = x2d.shape
    tm = TOKEN_TILE
    per_batch = seq // tm
    batch = t // seq
    row = lambda i: (i, 0)
    half_spec = pl.BlockSpec((tm, 512), row)
    bf = jax.ShapeDtypeStruct((t, 512), BF16)
    stream_specs = [_stream_spec(tm, dil, per_batch, ATTN_WIDTH) for _, dil in DILATED_PATTERNS]
    stream_shapes = [jax.ShapeDtypeStruct((batch, dil, seq // dil, ATTN_WIDTH), BF16)
                     for _, dil in DILATED_PATTERNS]
    return pl.pallas_call(
        _inproj_kernel,
        grid=(t // tm,),
        in_specs=[
            pl.BlockSpec((tm, d), row),
            pl.BlockSpec((None, 6, d), lambda i: (i // per_batch, 0, 0)),
            pl.BlockSpec((d, IN_COLS), lambda i: (0, 0)),
            pl.BlockSpec((tm, LANES), row),
            pl.BlockSpec((tm, LANES), row),
            pl.BlockSpec((1, 512), lambda i: (0, 0)),
        ],
        out_specs=stream_specs * 3 + [half_spec] * 5,
        out_shape=stream_shapes * 3 + [bf, bf, jax.ShapeDtypeStruct((t, 512), F32), bf, bf],
        scratch_shapes=[pltpu.VMEM((ATTN_WIDTH // LANES, tm, LANES), F32)],
        compiler_params=_params("parallel"),
        name="inproj",
    )(x2d, mod_l, w_in_l, cos, sin, lb_l)


def _attn_kernel(steps, q_ref, kc_ref, kp_ref, vc_ref, vp_ref, u_ref, st_ref):
    n = pl.program_id(2)
    blk = ATTN_BLOCK
    q = q_ref[...]
    kk = jnp.concatenate([kp_ref[...], kc_ref[...]], axis=0)
    vv = jnp.concatenate([vp_ref[...], vc_ref[...]], axis=0)
    qi = lax.broadcasted_iota(jnp.int32, (blk, 2 * blk), 0)
    kj = lax.broadcasted_iota(jnp.int32, (blk, 2 * blk), 1)
    dist = qi + blk - kj
    valid = (dist >= 0) & (dist <= steps) & ((kj >= blk) | (n > 0))
    lane = lax.broadcasted_iota(jnp.int32, (blk, LANES), 1)
    low = lane < HEAD_DIM
    st = jnp.zeros((blk, LANES), F32)
    zero = jnp.zeros((), BF16)
    for pair in range(N_HEADS // 2):
        sl = slice(pair * LANES, (pair + 1) * LANES)
        qp, kp2, vp2 = q[:, sl], kk[:, sl], vv[:, sl]
        outs = []
        for hh in range(2):
            head = 2 * pair + hh
            qm = jnp.where(low if hh == 0 else jnp.logical_not(low), qp, zero)
            s = lax.dot_general(qm, kp2, _NT, preferred_element_type=F32)
            s = jnp.where(valid, s, NEG_BIG)
            m = jnp.max(s, axis=1, keepdims=True).astype(BF16).astype(F32)
            p = jnp.exp(s - m)
            l = jnp.sum(p, axis=1, keepdims=True)
            outs.append(jnp.dot(p.astype(BF16), vp2, preferred_element_type=F32))
            st = jnp.where(lane == head, m, st)
            st = jnp.where(lane == N_HEADS + head, l, st)
        u_ref[:, sl] = jnp.where(low, outs[0], outs[1]).astype(BF16)
    st_ref[...] = st


def _attn_pattern(qs, ks, vs, window, dilation):
    batch, _, m, _ = qs.shape
    steps = window // dilation
    nb = m // ATTN_BLOCK
    cur = pl.BlockSpec((None, None, ATTN_BLOCK, ATTN_WIDTH), lambda b, r, n: (b, r, n, 0))
    prev = pl.BlockSpec((None, None, ATTN_BLOCK, ATTN_WIDTH),
                        lambda b, r, n: (b, r, jnp.maximum(n - 1, 0), 0))
    return pl.pallas_call(
        functools.partial(_attn_kernel, steps),
        grid=(batch, dilation, nb),
        in_specs=[cur, cur, prev, cur, prev],
        out_specs=[cur, pl.BlockSpec((None, None, ATTN_BLOCK, LANES), lambda b, r, n: (b, r, n, 0))],
        out_shape=[jax.ShapeDtypeStruct((batch, dilation, m, ATTN_WIDTH), BF16),
                   jax.ShapeDtypeStruct((batch, dilation, m, LANES), F32)],
        compiler_params=_params("parallel", "parallel", "arbitrary"),
        name=f"attn_d{dilation}",
    )(qs, ks, ks, vs, vs)


def _hgrn_kernel(q_ref, k_ref, g_ref, v_ref, ones_ref, o_ref, st_ref):
    c, sub, w = HGRN_CHUNK, HGRN_SUB, HGRN_WIDTH

    @pl.when(pl.program_id(1) == 0)
    def _():
        st_ref[...] = jnp.zeros_like(st_ref)

    q = q_ref[...].astype(F32)
    k = k_ref[...].astype(F32)
    vb = v_ref[...]
    v = vb.astype(F32)
    g = g_ref[...]

    ri = lax.broadcasted_iota(jnp.int32, (c, c), 0)
    ci = lax.broadcasted_iota(jnp.int32, (c, c), 1)
    tri = (ci <= ri).astype(BF16)
    g1, g2, g3 = _split3(g)
    cs = lambda t: jnp.dot(tri, t, preferred_element_type=F32)
    b = cs(g1) + (cs(g2) + cs(g3))
    b_last = b[c - 1:c, :]

    state = st_ref[...]
    o_inter = lax.dot_general((q * jnp.exp(b)).astype(BF16), state.astype(BF16), _NT,
                              preferred_element_type=F32)
    kt = (k * jnp.exp(b_last - b)).astype(BF16)
    upd = lax.dot_general(vb, kt, _TN, preferred_element_type=F32)
    r0i = lax.broadcasted_iota(jnp.int32, (w, w), 0) // HEAD_DIM
    c0i = lax.broadcasted_iota(jnp.int32, (w, w), 1) // HEAD_DIM
    st_ref[...] = state * jnp.exp(b_last) + jnp.where(r0i == c0i, upd, 0.0)

    hrow = lax.broadcasted_iota(jnp.int32, (N_HEADS * sub, w), 0) // sub
    hlane = lax.broadcasted_iota(jnp.int32, (N_HEADS * sub, w), 1) // HEAD_DIM
    hmask = hrow == hlane
    trow = lax.broadcasted_iota(jnp.int32, (sub, w), 0)
    ones = ones_ref[...]

    for blk in range(c // sub):
        r0 = blk * sub
        rows = slice(r0, r0 + sub)
        bi, qi, ki, vi = b[rows], q[rows], k[rows], v[rows]
        acc = o_inter[rows]
        if blk > 0:
            bref = b[r0:r0 + 1]
            qs = qi * jnp.exp(bi - bref)
            kp = (k[0:r0] * jnp.exp(bref - b[0:r0])).astype(BF16)
            qexp = jnp.where(hmask, jnp.concatenate([qs] * N_HEADS, axis=0), 0.0).astype(BF16)
            a = lax.dot_general(qexp, kp, _NT, preferred_element_type=F32)
            oexp = jnp.dot(a.astype(BF16), vb[0:r0], preferred_element_type=F32)
            oexp = jnp.where(hmask, oexp, 0.0)
            for hd in range(N_HEADS):
                acc = acc + oexp[hd * sub:(hd + 1) * sub]
        ws = []
        for s in range(sub):
            e = jnp.exp(bi - bi[s:s + 1])
            ws.append(jnp.where(trow >= s, qi * (ki[s:s + 1] * e), 0.0))
        wcat = jnp.concatenate(ws, axis=0).astype(BF16)
        sc = jnp.dot(wcat, ones, preferred_element_type=F32)
        for s in range(sub):
            acc = acc + sc[s * sub:(s + 1) * sub] * vi[s:s + 1]
        o_ref[rows, :] = acc


def _hgrn(qh, kh, gh, ih, ones_bd, batch, seq):
    c, w = HGRN_CHUNK, HGRN_WIDTH
    view = lambda a: a.reshape(batch, seq, w)
    blk = pl.BlockSpec((None, c, w), lambda b, n: (b, n, 0))
    o = pl.pallas_call(
        _hgrn_kernel,
        grid=(batch, seq // c),
        in_specs=[blk, blk, blk, blk, pl.BlockSpec((w, w), lambda b, n: (0, 0))],
        out_specs=blk,
        out_shape=jax.ShapeDtypeStruct((batch, seq, w), F32),
        scratch_shapes=[pltpu.VMEM((w, w), F32)],
        compiler_params=_params("parallel", "arbitrary"),
        name="hgrn2",
    )(view(qh), view(kh), view(gh), view(ih), ones_bd)
    return o.reshape(batch * seq, w)


def _route(probs):
    rows = [probs[i:i + 1, :] for i in range(N_EXPERTS)]
    gsum = []
    for gidx in range(N_GROUPS):
        a, b_, c_, d_ = rows[4 * gidx:4 * gidx + 4]
        hi1, lo1 = jnp.maximum(a, b_), jnp.minimum(a, b_)
        hi2, lo2 = jnp.maximum(c_, d_), jnp.minimum(c_, d_)
        top1 = jnp.maximum(hi1, hi2)
        second = jnp.maximum(jnp.minimum(hi1, hi2), jnp.maximum(lo1, lo2))
        gsum.append(top1 + second)
    best, gi = gsum[0], jnp.zeros(gsum[0].shape, jnp.int32)
    for gidx in range(1, N_GROUPS):
        upd = gsum[gidx] > best
        best = jnp.where(upd, gsum[gidx], best)
        gi = jnp.where(upd, gidx, gi)
    vals = []
    for j in range(EXPERTS_PER_GROUP):
        vj = rows[j]
        for gidx in range(1, N_GROUPS):
            vj = jnp.where(gi == gidx, rows[4 * gidx + j], vj)
        vals.append(vj)
    v1, i1 = vals[0], jnp.zeros(gi.shape, jnp.int32)
    for j in range(1, EXPERTS_PER_GROUP):
        upd = vals[j] > v1
        v1 = jnp.where(upd, vals[j], v1)
        i1 = jnp.where(upd, j, i1)
    v2, i2 = jnp.full(v1.shape, -1.0, F32), jnp.zeros(gi.shape, jnp.int32)
    for j in range(EXPERTS_PER_GROUP):
        upd = (i1 != j) & (vals[j] > v2)
        v2 = jnp.where(upd, vals[j], v2)
        i2 = jnp.where(upd, j, i2)
    tot = v1 + v2
    base = gi * EXPERTS_PER_GROUP
    return base + i1, base + i2, v1 / tot, v2 / tot


def _outproj_kernel(u1_ref, u2_ref, u3_ref, s1_ref, s2_ref, s3_ref, o_ref, og_ref, x_ref, mod_ref,
                    an_ref, hn_ref, w_ref, em_ref, el_ref, wr_ref, br_ref,
                    x1_ref, h2_ref, eidx_ref, wcol_ref, uscr_ref, sscr_ref):
    us = (u1_ref, u2_ref, u3_ref)
    ss = (s1_ref, s2_ref, s3_ref)
    em, el = em_ref[...], el_ref[...]

    def natural(ref, scr, dil):
        if dil == 1:
            return ref[0].astype(F32)
        slabs = scr.shape[0]
        for r in range(dil):
            val = ref[r].astype(F32)
            for c in range(slabs):
                scr[c, pl.ds(r, ref.shape[1], stride=dil), :] = val[:, c * LANES:(c + 1) * LANES]
        return jnp.concatenate([scr[c] for c in range(slabs)], axis=1)

    ms, ls = [], []
    for s_ref, (_, dil) in zip(ss, DILATED_PATTERNS):
        m1, m2, _ = _split3(natural(s_ref, sscr_ref, dil))
        ms.append(jnp.dot(m1, em, preferred_element_type=F32))
        ls.append(jnp.dot(m1, el, preferred_element_type=F32) + jnp.dot(m2, el, preferred_element_type=F32))
    mmax = jnp.maximum(jnp.maximum(ms[0], ms[1]), ms[2])
    num = jnp.zeros_like(mmax)
    den = jnp.zeros_like(mmax)
    for p in range(3):
        wgt = jnp.exp(ms[p] - mmax)
        num = num + wgt * natural(us[p], uscr_ref, DILATED_PATTERNS[p][1])
        den = den + wgt * ls[p]
    attn = num / den
    mod = mod_ref[...]
    a_n = _rms(attn) * an_ref[...]
    rec = _rms(o_ref[...]) * hn_ref[...] * og_ref[...].astype(F32)
    merged = jnp.concatenate([a_n, rec], axis=1).astype(BF16)
    mix = jnp.dot(merged, w_ref[...], preferred_element_type=F32)
    x1 = x_ref[...] + mod[2:3] * mix
    x1_ref[...] = x1
    h2 = _rms(x1) * (1.0 + mod[4:5]) + mod[3:4]
    h2_ref[...] = h2

    logits = _dot_hi(wr_ref[...], h2, _NT) + br_ref[...]
    e = jnp.exp(logits - jnp.max(logits, axis=0, keepdims=True))
    probs = e / jnp.sum(e, axis=0, keepdims=True)
    e1, e2, w1, w2 = _route(probs)
    eidx_ref[...] = jnp.concatenate([e1, e2], axis=0)
    tm = probs.shape[1]
    srow = lax.broadcasted_iota(jnp.int32, (LANES, tm), 0)
    w_t = jnp.where(srow == 0, w1, jnp.where(srow == 1, w2, 0.0))
    wcol_ref[...] = w_t.T


def _outproj(us, sts, o_h, og, x2d, mod_l, an_l, hn_l, w_out_l, em, el, wr_t, br, seq):
    t, d = x2d.shape
    tm = TOKEN_TILE
    per_batch = seq // tm
    row = lambda i: (i, 0)
    const = lambda i: (0, 0)
    half = pl.BlockSpec((tm, 512), row)
    stat = pl.BlockSpec((tm, LANES), row)
    full = pl.BlockSpec((tm, d), row)
    u_specs = [_stream_spec(tm, dil, per_batch, ATTN_WIDTH) for _, dil in DILATED_PATTERNS]
    s_specs = [_stream_spec(tm, dil, per_batch, LANES) for _, dil in DILATED_PATTERNS]
    return pl.pallas_call(
        _outproj_kernel,
        grid=(t // tm,),
        in_specs=u_specs + s_specs + [half, half, full,
                  pl.BlockSpec((None, 6, d), lambda i: (i // per_batch, 0, 0)),
                  pl.BlockSpec((1, 512), const), pl.BlockSpec((1, 512), const),
                  pl.BlockSpec((d, d), const),
                  pl.BlockSpec((LANES, 512), const), pl.BlockSpec((LANES, 512), const),
                  pl.BlockSpec((N_EXPERTS, d), const), pl.BlockSpec((N_EXPERTS, 1), const)],
        out_specs=[full, full, pl.BlockSpec((N_ASSIGN, tm), lambda i: (0, i)), stat],
        out_shape=[jax.ShapeDtypeStruct((t, d), F32), jax.ShapeDtypeStruct((t, d), F32),
                   jax.ShapeDtypeStruct((N_ASSIGN, t), jnp.int32), jax.ShapeDtypeStruct((t, LANES), F32)],
        scratch_shapes=[pltpu.VMEM((ATTN_WIDTH // LANES, tm, LANES), F32), pltpu.VMEM((1, tm, LANES), F32)],
        compiler_params=_params("parallel"),
        name="outproj_route",
    )(*us, *sts, o_h, og, x2d, mod_l, an_l, hn_l, w_out_l, em, el, wr_t, br)


def _one_hots(e_ref):
    e = e_ref[...]
    erow = lax.broadcasted_iota(jnp.int32, (N_EXPERTS, e.shape[1]), 0)
    return (erow == e[0:1]).astype(F32), (erow == e[1:2]).astype(F32)


def _spread(col):
    return jnp.broadcast_to(col, (N_EXPERTS, LANES))


def _count_kernel(e_ref, cnt_ref):
    @pl.when(pl.program_id(0) == 0)
    def _():
        cnt_ref[...] = jnp.zeros_like(cnt_ref)

    oh0, oh1 = _one_hots(e_ref)
    cnt_ref[...] += _spread(jnp.sum(oh0 + oh1, axis=1, keepdims=True))


def _positions_kernel(e_ref, cnt_ref, pos_ref, meta_ref, carry_ref, offs_ref):
    i = pl.program_id(0)
    tp = e_ref.shape[1]
    oh0, oh1 = _one_hots(e_ref)

    @pl.when(i == 0)
    def _():
        cnt = cnt_ref[...]
        padded = jnp.floor((cnt + (MOE_TILE - 1)) * (1.0 / MOE_TILE)) * MOE_TILE
        run = jnp.zeros((1, LANES), F32)
        starts = []
        for ex in range(N_EXPERTS):
            starts.append(run)
            run = run + padded[ex:ex + 1]
        offs = jnp.concatenate(starts, axis=0)
        offs_ref[...] = offs
        carry_ref[...] = jnp.zeros_like(carry_ref)
        ends = offs + padded
        lane = lax.broadcasted_iota(jnp.int32, (N_EXPERTS, LANES), 1)
        srow = lax.broadcasted_iota(jnp.int32, (N_EXPERTS, LANES), 0)
        tile_start = (lane * MOE_TILE).astype(F32)
        tile_expert = jnp.sum((ends <= tile_start).astype(F32), axis=0, keepdims=True)
        tile_expert = jnp.minimum(tile_expert, N_EXPERTS - 1.0)
        on_diag = srow == lane
        ends_lane = jnp.sum(jnp.where(on_diag, ends, 0.0), axis=0, keepdims=True)
        pad_lane = jnp.sum(jnp.where(on_diag, padded, 0.0), axis=0, keepdims=True)
        meta = jnp.concatenate([tile_expert, run * (1.0 / MOE_TILE), ends_lane, pad_lane,
                                jnp.zeros((4, LANES), F32)], axis=0)
        meta_ref[...] = meta.astype(jnp.int32)

    r = lax.broadcasted_iota(jnp.int32, (tp, tp), 0)
    c = lax.broadcasted_iota(jnp.int32, (tp, tp), 1)
    upper = (r <= c).astype(BF16)
    oh = jnp.concatenate([oh0, oh1], axis=0).astype(BF16)
    pre = jnp.dot(oh, upper, preferred_element_type=F32)
    pre0, pre1 = pre[:N_EXPERTS], pre[N_EXPERTS:]
    tot0, tot1 = pre0[:, tp - 1:tp], pre1[:, tp - 1:tp]
    base = offs_ref[...][:, 0:1] + carry_ref[...][:, 0:1]
    p0 = jnp.sum(oh0 * (pre0 - 1.0 + base), axis=0, keepdims=True)
    p1 = jnp.sum(oh1 * (pre1 - 1.0 + (base + tot0)), axis=0, keepdims=True)
    pos_ref[...] = jnp.concatenate([p0, p1], axis=0).astype(jnp.int32)
    carry_ref[...] += _spread(tot0 + tot1)


def _positions(eidx):
    t = eidx.shape[1]
    tp = POS_TILE
    blk = pl.BlockSpec((N_ASSIGN, tp), lambda i: (0, i))
    whole = pl.BlockSpec((N_EXPERTS, LANES), lambda i: (0, 0))
    stat = pltpu.VMEM((N_EXPERTS, LANES), F32)
    cnt = pl.pallas_call(
        _count_kernel,
        grid=(t // tp,),
        in_specs=[blk],
        out_specs=whole,
        out_shape=jax.ShapeDtypeStruct((N_EXPERTS, LANES), F32),
        compiler_params=_params("arbitrary"),
        name="moe_count",
    )(eidx)
    return pl.pallas_call(
        _positions_kernel,
        grid=(t // tp,),
        in_specs=[blk, whole],
        out_specs=[blk, pl.BlockSpec((8, LANES), lambda i: (0, 0))],
        out_shape=[jax.ShapeDtypeStruct((N_ASSIGN, t), jnp.int32), jax.ShapeDtypeStruct((8, LANES), jnp.int32)],
        scratch_shapes=[stat, stat],
        compiler_params=_params("arbitrary"),
        name="moe_positions",
    )(eidx, cnt)


def _row(ref, r):
    return ref.at[pl.ds(r, 1), :]


def _dispatch_kernel(ends_ref, pad_ref, used_ref, pos_ref, h_hbm, xs_hbm, zero_ref, sem):
    i = pl.program_id(0)
    tp = pos_ref.shape[1]

    def clear_tile(start):
        cp = pltpu.make_async_copy(zero_ref, xs_hbm.at[pl.ds(pl.multiple_of(start, MOE_TILE), MOE_TILE), :], sem)
        cp.start()
        cp.wait()

    @pl.when(i == 0)
    def _():
        zero_ref[...] = jnp.zeros_like(zero_ref)
        for ex in range(N_EXPERTS):
            @pl.when(pad_ref[ex] > 0)
            def _():
                clear_tile(ends_ref[ex] - MOE_TILE)

        def clear_tail(tile, carry):
            clear_tile(tile * MOE_TILE)
            return carry

        lax.fori_loop(used_ref[0], xs_hbm.shape[0] // MOE_TILE, clear_tail, 0)

    base = i * tp

    def issue(j, carry):
        for k in range(N_ASSIGN):
            pltpu.make_async_copy(_row(h_hbm, base + j), _row(xs_hbm, pos_ref[k, j]), sem).start()
        return carry

    def drain(j, carry):
        for k in range(N_ASSIGN):
            pltpu.make_async_copy(_row(h_hbm, 0), _row(xs_hbm, 0), sem).wait()
        return carry

    lax.fori_loop(0, tp, issue, 0, unroll=8)
    lax.fori_loop(0, tp, drain, 0, unroll=8)


def _dispatch(ends, padded, n_used, pos, h2):
    t, d = h2.shape
    tp = DMA_TILE
    rows = N_ASSIGN * t + N_EXPERTS * MOE_TILE
    return pl.pallas_call(
        _dispatch_kernel,
        grid_spec=pltpu.PrefetchScalarGridSpec(
            num_scalar_prefetch=3,
            grid=(t // tp,),
            in_specs=[pl.BlockSpec((N_ASSIGN, tp), lambda i, *_: (0, i), memory_space=pltpu.SMEM),
                      pl.BlockSpec(memory_space=pl.ANY)],
            out_specs=pl.BlockSpec(memory_space=pl.ANY),
            scratch_shapes=[pltpu.VMEM((MOE_TILE, d), F32), pltpu.SemaphoreType.DMA(())],
        ),
        out_shape=jax.ShapeDtypeStruct((rows, d), F32),
        compiler_params=_params("arbitrary"),
        name="moe_dispatch",
    )(ends, padded, n_used, pos, h2)


def _experts_kernel(te_ref, nv_ref, xs_ref, wg_ref, wu_ref, wd_ref, ys_ref, wgb_ref, wub_ref, wdb_ref):
    j = pl.program_id(0)
    used = nv_ref[0]
    jc = jnp.minimum(j, used - 1)
    new_expert = jnp.logical_or(j == 0, te_ref[jc] != te_ref[jnp.maximum(jc - 1, 0)])

    @pl.when(jnp.logical_and(j < used, new_expert))
    def _():
        wgb_ref[...] = wg_ref[...].astype(BF16)
        wub_ref[...] = wu_ref[...].astype(BF16)
        wdb_ref[...] = wd_ref[...].astype(BF16)

    @pl.when(j < used)
    def _():
        xb = xs_ref[...].astype(BF16)
        gt = jnp.dot(xb, wgb_ref[...], preferred_element_type=F32)
        up = jnp.dot(xb, wub_ref[...], preferred_element_type=F32)
        a = (gt * _sigmoid(gt) * up).astype(BF16)
        ys_ref[...] = jnp.dot(a, wdb_ref[...], preferred_element_type=F32)

    @pl.when(j >= used)
    def _():
        ys_ref[...] = jnp.zeros_like(ys_ref)


def _experts(tile_expert, n_used, xs, wg_l, wu_l, wd_l):
    rows, d = xs.shape
    f = D_FF_EXPERT
    tile = lambda j, te, nv: (jnp.minimum(j, nv[0] - 1), 0)
    wsel = lambda j, te, nv: (te[jnp.minimum(j, nv[0] - 1)], 0, 0)
    return pl.pallas_call(
        _experts_kernel,
        grid_spec=pltpu.PrefetchScalarGridSpec(
            num_scalar_prefetch=2,
            grid=(rows // MOE_TILE,),
            in_specs=[pl.BlockSpec((MOE_TILE, d), tile),
                      pl.BlockSpec((None, d, f), wsel), pl.BlockSpec((None, d, f), wsel),
                      pl.BlockSpec((None, f, d), wsel)],
            out_specs=pl.BlockSpec((MOE_TILE, d), lambda j, te, nv: (j, 0)),
            scratch_shapes=[pltpu.VMEM((d, f), BF16), pltpu.VMEM((d, f), BF16), pltpu.VMEM((f, d), BF16)],
        ),
        out_shape=jax.ShapeDtypeStruct((rows, d), F32),
        compiler_params=_params("arbitrary"),
        name="moe_experts",
    )(tile_expert, n_used, xs, wg_l, wu_l, wd_l)


def _combine_kernel(final, pos_ref, ys_hbm, wcol_ref, x_ref, mod_ref, fn_ref, o_ref, buf_ref, sem):
    tc = pos_ref.shape[1]

    def issue(j, carry):
        for k in range(N_ASSIGN):
            pltpu.make_async_copy(_row(ys_hbm, pos_ref[k, j]), buf_ref.at[k, pl.ds(j, 1), :], sem).start()
        return carry

    def drain(j, carry):
        for k in range(N_ASSIGN):
            pltpu.make_async_copy(_row(ys_hbm, 0), buf_ref.at[k, pl.ds(0, 1), :], sem).wait()
        return carry

    lax.fori_loop(0, tc, issue, 0, unroll=8)
    lax.fori_loop(0, tc, drain, 0, unroll=8)
    w = wcol_ref[...]
    ffn = w[:, 0:1] * buf_ref[0] + w[:, 1:2] * buf_ref[1]
    y = x_ref[...] + mod_ref[...][5:6] * ffn
    if final:
        y = _rms(y) * fn_ref[...]
    o_ref[...] = y


def _combine(pos, ys, wcol, x1, mod_l, fnorm, seq, final):
    t, d = x1.shape
    tc = DMA_TILE
    per_batch = seq // tc
    row = lambda i: (i, 0)
    return pl.pallas_call(
        functools.partial(_combine_kernel, final),
        grid=(t // tc,),
        in_specs=[pl.BlockSpec((N_ASSIGN, tc), lambda i: (0, i), memory_space=pltpu.SMEM),
                  pl.BlockSpec(memory_space=pl.ANY),
                  pl.BlockSpec((tc, LANES), row), pl.BlockSpec((tc, d), row),
                  pl.BlockSpec((None, 6, d), lambda i: (i // per_batch, 0, 0)),
                  pl.BlockSpec((1, d), lambda i: (0, 0))],
        out_specs=pl.BlockSpec((tc, d), row),
        out_shape=jax.ShapeDtypeStruct((t, d), F32),
        scratch_shapes=[pltpu.VMEM((N_ASSIGN, tc, d), F32), pltpu.SemaphoreType.DMA(())],
        compiler_params=_params("arbitrary"),
        name="moe_combine",
    )(pos, ys, wcol, x1, mod_l, fnorm)


def _moe(h2, eidx, wcol, wg_l, wu_l, wd_l, x1, mod_l, fnorm, seq, final):
    pos, meta = _positions(eidx)
    xs = _dispatch(meta[2], meta[3], meta[1], pos, h2)
    ys = _experts(meta[0], meta[1], xs, wg_l, wu_l, wd_l)
    return _combine(pos, ys, wcol, x1, mod_l, fnorm, seq, final)


def _head_expand(offset):
    m = np.zeros((LANES, ATTN_WIDTH), np.float32)
    for h in range(N_HEADS):
        m[offset + h, h * HEAD_DIM:(h + 1) * HEAD_DIM] = 1.0
    return jnp.asarray(m, BF16)


def kernel(x, c, positions, w_in, w_out, attn_norm, hgrn_norm, lb_params, ada_w, ada_b,
           w_router, b_router, w_gate, w_up, w_down, final_norm):
    batch, seq, d = x.shape
    t = batch * seq
    c_pad = jnp.pad(c, ((0, 8 - batch), (0, 0)))
    mod = _ada_mod(c_pad, ada_w, ada_b)[:, :batch].reshape(DEPTH, batch, 6, d)
    lbs = _lower_bounds(lb_params)
    cos, sin = _rope_tables(positions)
    cos, sin = cos.reshape(t, LANES), sin.reshape(t, LANES)
    head_id = np.arange(HGRN_WIDTH) // HEAD_DIM
    ones_bd = jnp.asarray(head_id[:, None] == head_id[None, :], BF16)
    em, el = _head_expand(0), _head_expand(N_HEADS)
    wr_t = w_router.T
    br = b_router.reshape(N_EXPERTS, 1)
    fnorm = final_norm.reshape(1, d)

    x2d = x.reshape(t, d)
    for l in range(DEPTH):
        outs = _inproj(x2d, mod[l], w_in[l].astype(BF16), cos, sin, lbs[l:l + 1], seq)
        qs, ks, vs = outs[0:3], outs[3:6], outs[6:9]
        qh, kh, gh, ih, og = outs[9:]
        us, sts = [], []
        for p, (window, dilation) in enumerate(DILATED_PATTERNS):
            u, st = _attn_pattern(qs[p], ks[p], vs[p], window, dilation)
            us.append(u)
            sts.append(st)
        o_h = _hgrn(qh, kh, gh, ih, ones_bd, batch, seq)
        x1, h2, eidx, wcol = _outproj(us, sts, o_h, og, x2d, mod[l], attn_norm[l:l + 1], hgrn_norm[l:l + 1],
                                      w_out[l].astype(BF16), em, el, wr_t, br, seq)
        x2d = _moe(h2, eidx, wcol, w_gate[l], w_up[l], w_down[l], x1, mod[l], fnorm, seq,
                   final=(l == DEPTH - 1))
    return x2d.reshape(batch, seq, d)
```

```python
import functools

import numpy as np
import jax
import jax.numpy as jnp
from jax import lax
from jax.experimental import pallas as pl
from jax.experimental.pallas import tpu as pltpu

D_MODEL = 1024
DEPTH = 2
ATTN_WIDTH = 512
HGRN_WIDTH = 512
HEAD_DIM = 64
N_HEADS = 8
DILATED_PATTERNS = ((128, 1), (512, 4), (2048, 16))
ATTN_BLOCK = 128
ROPE_THETA = 10000.0
N_EXPERTS = 16
N_GROUPS = 4
EXPERTS_PER_GROUP = 4
D_FF_EXPERT = 512
RMS_EPS = 1e-6
IN_COLS = 3 * ATTN_WIDTH + 4 * HGRN_WIDTH

LANES = 128
MXU_WIDTH = 256
VMEM_LIMIT_BYTES = 56 * 1024 * 1024

TOKEN_TILE = 512
N_ASSIGN = 2
MOE_TILE = 512
POS_TILE = 1024
DMA_TILE = 512
HGRN_CHUNK = 64
HGRN_SUB = 8
NEG_BIG = -1e30

F32 = jnp.float32
BF16 = jnp.bfloat16

_NT = (((1,), (1,)), ((), ()))
_TN = (((0,), (0,)), ((), ()))


def _params(*sem):
    return pltpu.CompilerParams(dimension_semantics=sem, vmem_limit_bytes=VMEM_LIMIT_BYTES)


def _sigmoid(x):
    return 1.0 / (1.0 + jnp.exp(-x))


def _rms(x):
    return x * lax.rsqrt(jnp.mean(x * x, axis=-1, keepdims=True) + RMS_EPS)


def _split3(a):
    a1 = a.astype(BF16)
    r1 = a - a1.astype(F32)
    a2 = r1.astype(BF16)
    a3 = (r1 - a2.astype(F32)).astype(BF16)
    return a1, a2, a3


def _dot_hi(a, b, dims):
    a1, a2, _ = _split3(a)
    b1, b2, _ = _split3(b)
    d = lambda p, q: lax.dot_general(p, q, dims, preferred_element_type=F32)
    return d(a1, b1) + (d(a2, b1) + d(a1, b2))


def _ada_kernel(c_ref, w_ref, b_ref, o_ref):
    c = c_ref[...]
    ca = c * _sigmoid(c)
    o_ref[...] = _dot_hi(ca, w_ref[...], (((1,), (0,)), ((), ()))) + b_ref[...]


def _ada_mod(c_pad, ada_w, ada_b):
    depth, d, n = ada_w.shape
    tn = 1536
    rows = c_pad.shape[0]
    return pl.pallas_call(
        _ada_kernel,
        grid=(depth, n // tn),
        in_specs=[
            pl.BlockSpec((rows, d), lambda l, j: (0, 0)),
            pl.BlockSpec((None, d, tn), lambda l, j: (l, 0, j)),
            pl.BlockSpec((None, 1, tn), lambda l, j: (l, 0, j)),
        ],
        out_specs=pl.BlockSpec((None, rows, tn), lambda l, j: (l, 0, j)),
        out_shape=jax.ShapeDtypeStruct((depth, rows, n), F32),
        compiler_params=_params("parallel", "parallel"),
        name="ada_mod",
    )(c_pad, ada_w, ada_b.reshape(depth, 1, n))


def _lb_kernel(p_ref, o_ref):
    p = p_ref[...]
    e = jnp.exp(p - jnp.max(p, axis=0, keepdims=True))
    sm = e / jnp.sum(e, axis=0, keepdims=True)
    run = jnp.zeros_like(sm[0:1])
    for l in range(p.shape[0]):
        run = run + sm[l:l + 1]
        o_ref[l:l + 1, :] = run - sm[0:1]


def _lower_bounds(lb_params):
    return pl.pallas_call(
        _lb_kernel,
        out_shape=jax.ShapeDtypeStruct(lb_params.shape, F32),
        name="lower_bounds",
    )(lb_params)


def _rope_kernel(pos_ref, inv_ref, cos_ref, sin_ref):
    ang = pos_ref[...].astype(F32) * inv_ref[...]
    lane = lax.broadcasted_iota(jnp.int32, ang.shape, 1)
    first = (lane % HEAD_DIM) < (HEAD_DIM // 2)
    s = jnp.sin(ang)
    cos_ref[...] = jnp.cos(ang)
    sin_ref[...] = jnp.where(first, -s, s)


def _rope_tables(positions):
    b, s = positions.shape
    ts = 1024
    half = HEAD_DIM // 2
    inv = ROPE_THETA ** (-jnp.arange(half, dtype=F32) / half)
    inv = jnp.tile(inv, LANES // half).reshape(1, LANES)
    out = jax.ShapeDtypeStruct((b, s, LANES), F32)
    return pl.pallas_call(
        _rope_kernel,
        grid=(b, s // ts),
        in_specs=[
            pl.BlockSpec((None, ts, 1), lambda i, j: (i, j, 0)),
            pl.BlockSpec((1, LANES), lambda i, j: (0, 0)),
        ],
        out_specs=[pl.BlockSpec((None, ts, LANES), lambda i, j: (i, j, 0))] * 2,
        out_shape=[out, out],
        compiler_params=_params("parallel", "parallel"),
        name="rope_tables",
    )(positions.reshape(b, s, 1), inv)


def _store_streams(scr_ref, val, refs):
    tm = val.shape[0]
    slabs = scr_ref.shape[0]
    for c in range(slabs):
        scr_ref[c] = val[:, c * LANES:(c + 1) * LANES]
    for (_, d), ref in zip(DILATED_PATTERNS, refs):
        if d == 1:
            ref[0] = val.astype(BF16)
        else:
            for r in range(d):
                rows = [scr_ref[c, pl.ds(r, tm // d, stride=d), :] for c in range(slabs)]
                ref[r] = jnp.concatenate(rows, axis=1).astype(BF16)


def _inproj_kernel(x_ref, mod_ref, w_ref, cos_ref, sin_ref, lb_ref,
                   q1_ref, q4_ref, q16_ref, k1_ref, k4_ref, k16_ref, v1_ref, v4_ref, v16_ref,
                   qh_ref, kh_ref, gh_ref, ih_ref, og_ref, scr_ref):
    x = x_ref[...]
    mod = mod_ref[...]
    h = _rms(x) * (1.0 + mod[1:2]) + mod[0:1]
    hb = h.astype(BF16)
    reps = ATTN_WIDTH // LANES
    cos = jnp.concatenate([cos_ref[...]] * reps, axis=1)
    sin = jnp.concatenate([sin_ref[...]] * reps, axis=1)
    lane = lax.broadcasted_iota(jnp.int32, cos.shape, 1)
    first = (lane % HEAD_DIM) < (HEAD_DIM // 2)
    half = HEAD_DIM // 2

    def proj(j):
        return jnp.dot(hb, w_ref[:, j * 512:(j + 1) * 512], preferred_element_type=F32)

    def rot(t):
        swapped = jnp.where(first, pltpu.roll(t, ATTN_WIDTH - half, 1), pltpu.roll(t, half, 1))
        return t * cos + swapped * sin

    _store_streams(scr_ref, rot(proj(0)) * (HEAD_DIM ** -0.5), (q1_ref, q4_ref, q16_ref))
    _store_streams(scr_ref, rot(proj(1)), (k1_ref, k4_ref, k16_ref))
    _store_streams(scr_ref, proj(2), (v1_ref, v4_ref, v16_ref))
    qh = proj(3)
    qh_ref[...] = (qh * _sigmoid(qh)).astype(BF16)
    lb = lb_ref[...]
    f = lb + (1.0 - lb) * _sigmoid(proj(4))
    kh_ref[...] = (1.0 - f).astype(BF16)
    gh_ref[...] = jnp.log(f)
    ih_ref[...] = proj(5).astype(BF16)
    og_ref[...] = _sigmoid(proj(6)).astype(BF16)


def _stream_spec(tm, dil, per_batch, width):
    return pl.BlockSpec((None, dil, tm // dil, width), lambda i: (i // per_batch, 0, i % per_batch, 0))


def _inproj(x2d, mod_l, w_in_l, cos, sin, lb_l, seq):
    t, d = x2d.shape
    tm = TOKEN_TILE
    per_batch = seq // tm
    batch = t // seq
    row = lambda i: (i, 0)
    half_spec = pl.BlockSpec((tm, 512), row)
    bf = jax.ShapeDtypeStruct((t, 512), BF16)
    stream_specs = [_stream_spec(tm, dil, per_batch, ATTN_WIDTH) for _, dil in DILATED_PATTERNS]
    stream_shapes = [jax.ShapeDtypeStruct((batch, dil, seq // dil, ATTN_WIDTH), BF16)
                     for _, dil in DILATED_PATTERNS]
    return pl.pallas_call(
        _inproj_kernel,
        grid=(t // tm,),
        in_specs=[
            pl.BlockSpec((tm, d), row),
            pl.BlockSpec((None, 6, d), lambda i: (i // per_batch, 0, 0)),
            pl.BlockSpec((d, IN_COLS), lambda i: (0, 0)),
            pl.BlockSpec((tm, LANES), row),
            pl.BlockSpec((tm, LANES), row),
            pl.BlockSpec((1, 512), lambda i: (0, 0)),
        ],
        out_specs=stream_specs * 3 + [half_spec] * 5,
        out_shape=stream_shapes * 3 + [bf, bf, jax.ShapeDtypeStruct((t, 512), F32), bf, bf],
        scratch_shapes=[pltpu.VMEM((ATTN_WIDTH // LANES, tm, LANES), F32)],
        compiler_params=_params("parallel"),
        name="inproj",
    )(x2d, mod_l, w_in_l, cos, sin, lb_l)


def _attn_kernel(q_ref, kc_ref, kp_ref, vc_ref, vp_ref, bias_ref, u_ref, st_ref):
    blk = ATTN_BLOCK
    q = q_ref[...]
    kk = jnp.concatenate([kp_ref[...], kc_ref[...]], axis=0)
    vv = jnp.concatenate([vp_ref[...], vc_ref[...]], axis=0)
    bias = bias_ref[...]
    lane = lax.broadcasted_iota(jnp.int32, (blk, LANES), 1)
    low = lane < HEAD_DIM
    st = jnp.zeros((blk, LANES), F32)
    zero = jnp.zeros((), BF16)
    for pair in range(N_HEADS // 2):
        sl = slice(pair * LANES, (pair + 1) * LANES)
        qp, kp2, vp2 = q[:, sl], kk[:, sl], vv[:, sl]
        qst = jnp.concatenate([jnp.where(low, qp, zero), jnp.where(low, zero, qp)], axis=0)
        s = lax.dot_general(qst, kp2, _NT, preferred_element_type=F32)
        s = s + bias
        m = jnp.max(s, axis=1, keepdims=True).astype(BF16).astype(F32)
        p = jnp.exp(s - m)
        l = jnp.sum(p, axis=1, keepdims=True)
        u = jnp.dot(p.astype(BF16), vp2, preferred_element_type=F32)
        for hh in range(2):
            head = 2 * pair + hh
            st = jnp.where(lane == head, m[hh * blk:(hh + 1) * blk], st)
            st = jnp.where(lane == N_HEADS + head, l[hh * blk:(hh + 1) * blk], st)
        u_ref[:, sl] = jnp.where(low, u[:blk], u[blk:]).astype(BF16)
    st_ref[...] = st


def _band_bias(steps):
    blk = ATTN_BLOCK
    qi = np.arange(2 * blk)[:, None] % blk
    kj = np.arange(2 * blk)[None, :]
    dist = qi + blk - kj
    band = (dist >= 0) & (dist <= steps)
    first = band & (kj >= blk)
    return jnp.asarray(np.where(np.stack([first, band]), 0.0, NEG_BIG), F32)


def _attn_pattern(qs, ks, vs, window, dilation):
    batch, _, m, _ = qs.shape
    nb = m // ATTN_BLOCK
    bias = _band_bias(window // dilation)
    cur = pl.BlockSpec((None, None, ATTN_BLOCK, ATTN_WIDTH), lambda b, r, n: (b, r, n, 0))
    prev = pl.BlockSpec((None, None, ATTN_BLOCK, ATTN_WIDTH),
                        lambda b, r, n: (b, r, jnp.maximum(n - 1, 0), 0))
    return pl.pallas_call(
        _attn_kernel,
        grid=(batch, dilation, nb),
        in_specs=[cur, cur, prev, cur, prev,
                  pl.BlockSpec((None,) + bias.shape[1:], lambda b, r, n: (jnp.minimum(n, 1), 0, 0))],
        out_specs=[cur, pl.BlockSpec((None, None, ATTN_BLOCK, LANES), lambda b, r, n: (b, r, n, 0))],
        out_shape=[jax.ShapeDtypeStruct((batch, dilation, m, ATTN_WIDTH), BF16),
                   jax.ShapeDtypeStruct((batch, dilation, m, LANES), F32)],
        compiler_params=_params("parallel", "parallel", "arbitrary"),
        name=f"attn_d{dilation}",
    )(qs, ks, ks, vs, vs, bias)


def _hgrn_kernel(q_ref, k_ref, g_ref, v_ref, ones_ref, o_ref, st_ref):
    c, sub, w = HGRN_CHUNK, HGRN_SUB, HGRN_WIDTH

    @pl.when(pl.program_id(1) == 0)
    def _():
        st_ref[...] = jnp.zeros_like(st_ref)

    q = q_ref[...].astype(F32)
    k = k_ref[...].astype(F32)
    vb = v_ref[...]
    v = vb.astype(F32)
    g = g_ref[...]

    ri = lax.broadcasted_iota(jnp.int32, (c, c), 0)
    ci = lax.broadcasted_iota(jnp.int32, (c, c), 1)
    tri = (ci <= ri).astype(BF16)
    g1, g2, g3 = _split3(g)
    cs = lambda t: jnp.dot(tri, t, preferred_element_type=F32)
    b = cs(g1) + (cs(g2) + cs(g3))
    b_last = b[c - 1:c, :]

    state = st_ref[...]
    o_inter = lax.dot_general((q * jnp.exp(b)).astype(BF16), state.astype(BF16), _NT,
                              preferred_element_type=F32)
    kt = (k * jnp.exp(b_last - b)).astype(BF16)
    upd = lax.dot_general(vb, kt, _TN, preferred_element_type=F32)
    r0i = lax.broadcasted_iota(jnp.int32, (w, w), 0) // HEAD_DIM
    c0i = lax.broadcasted_iota(jnp.int32, (w, w), 1) // HEAD_DIM
    st_ref[...] = state * jnp.exp(b_last) + jnp.where(r0i == c0i, upd, 0.0)

    hrow = lax.broadcasted_iota(jnp.int32, (N_HEADS * sub, w), 0) // sub
    hlane = lax.broadcasted_iota(jnp.int32, (N_HEADS * sub, w), 1) // HEAD_DIM
    hmask = hrow == hlane
    trow = lax.broadcasted_iota(jnp.int32, (sub, w), 0)
    ones = ones_ref[...]
    ow = ones.shape[0]

    for blk in range(c // sub):
        r0 = blk * sub
        rows = slice(r0, r0 + sub)
        bi, qi, ki, vi = b[rows], q[rows], k[rows], v[rows]
        acc = o_inter[rows]
        if blk > 0:
            bref = b[r0:r0 + 1]
            qs = qi * jnp.exp(bi - bref)
            kp = (k[0:r0] * jnp.exp(bref - b[0:r0])).astype(BF16)
            qexp = jnp.where(hmask, jnp.concatenate([qs] * N_HEADS, axis=0), 0.0).astype(BF16)
            a = lax.dot_general(qexp, kp, _NT, preferred_element_type=F32)
            oexp = jnp.dot(a.astype(BF16), vb[0:r0], preferred_element_type=F32)
            oexp = jnp.where(hmask, oexp, 0.0)
            for hd in range(N_HEADS):
                acc = acc + oexp[hd * sub:(hd + 1) * sub]
        ws = []
        for s in range(sub):
            e = jnp.exp(bi - bi[s:s + 1])
            ws.append(jnp.where(trow >= s, qi * (ki[s:s + 1] * e), 0.0))
        wcat = jnp.concatenate(ws, axis=0).astype(BF16)
        sc = jnp.concatenate(
            [jnp.dot(wcat[:, j * ow:(j + 1) * ow], ones, preferred_element_type=F32) for j in range(w // ow)],
            axis=1)
        for s in range(sub):
            acc = acc + sc[s * sub:(s + 1) * sub] * vi[s:s + 1]
        o_ref[rows, :] = acc


def _hgrn(qh, kh, gh, ih, ones_bd, batch, seq):
    c, w = HGRN_CHUNK, HGRN_WIDTH
    view = lambda a: a.reshape(batch, seq, w)
    blk = pl.BlockSpec((None, c, w), lambda b, n: (b, n, 0))
    o = pl.pallas_call(
        _hgrn_kernel,
        grid=(batch, seq // c),
        in_specs=[blk, blk, blk, blk, pl.BlockSpec(ones_bd.shape, lambda b, n: (0, 0))],
        out_specs=blk,
        out_shape=jax.ShapeDtypeStruct((batch, seq, w), F32),
        scratch_shapes=[pltpu.VMEM((w, w), F32)],
        compiler_params=_params("parallel", "arbitrary"),
        name="hgrn2",
    )(view(qh), view(kh), view(gh), view(ih), ones_bd)
    return o.reshape(batch * seq, w)


def _route(probs):
    rows = [probs[i:i + 1, :] for i in range(N_EXPERTS)]
    gsum = []
    for gidx in range(N_GROUPS):
        a, b_, c_, d_ = rows[4 * gidx:4 * gidx + 4]
        hi1, lo1 = jnp.maximum(a, b_), jnp.minimum(a, b_)
        hi2, lo2 = jnp.maximum(c_, d_), jnp.minimum(c_, d_)
        top1 = jnp.maximum(hi1, hi2)
        second = jnp.maximum(jnp.minimum(hi1, hi2), jnp.maximum(lo1, lo2))
        gsum.append(top1 + second)
    best, gi = gsum[0], jnp.zeros(gsum[0].shape, jnp.int32)
    for gidx in range(1, N_GROUPS):
        upd = gsum[gidx] > best
        best = jnp.where(upd, gsum[gidx], best)
        gi = jnp.where(upd, gidx, gi)
    vals = []
    for j in range(EXPERTS_PER_GROUP):
        vj = rows[j]
        for gidx in range(1, N_GROUPS):
            vj = jnp.where(gi == gidx, rows[4 * gidx + j], vj)
        vals.append(vj)
    v1, i1 = vals[0], jnp.zeros(gi.shape, jnp.int32)
    for j in range(1, EXPERTS_PER_GROUP):
        upd = vals[j] > v1
        v1 = jnp.where(upd, vals[j], v1)
        i1 = jnp.where(upd, j, i1)
    v2, i2 = jnp.full(v1.shape, -1.0, F32), jnp.zeros(gi.shape, jnp.int32)
    for j in range(EXPERTS_PER_GROUP):
        upd = (i1 != j) & (vals[j] > v2)
        v2 = jnp.where(upd, vals[j], v2)
        i2 = jnp.where(upd, j, i2)
    tot = v1 + v2
    base = gi * EXPERTS_PER_GROUP
    return base + i1, base + i2, v1 / tot, v2 / tot


def _outproj_kernel(u1_ref, u2_ref, u3_ref, s1_ref, s2_ref, s3_ref, o_ref, og_ref, x_ref, mod_ref,
                    an_ref, hn_ref, w_ref, em_ref, el_ref, wr_ref, br_ref,
                    x1_ref, h2_ref, eidx_ref, wcol_ref, uscr_ref, sscr_ref):
    us = (u1_ref, u2_ref, u3_ref)
    ss = (s1_ref, s2_ref, s3_ref)
    em, el = em_ref[...], el_ref[...]

    def natural(ref, scr, dil):
        if dil == 1:
            return ref[0].astype(F32)
        slabs = scr.shape[0]
        for r in range(dil):
            val = ref[r].astype(F32)
            for c in range(slabs):
                scr[c, pl.ds(r, ref.shape[1], stride=dil), :] = val[:, c * LANES:(c + 1) * LANES]
        return jnp.concatenate([scr[c] for c in range(slabs)], axis=1)

    ms, ls = [], []
    for s_ref, (_, dil) in zip(ss, DILATED_PATTERNS):
        m1, m2, _ = _split3(natural(s_ref, sscr_ref, dil))
        ms.append(jnp.dot(m1, em, preferred_element_type=F32))
        ls.append(jnp.dot(m1, el, preferred_element_type=F32) + jnp.dot(m2, el, preferred_element_type=F32))
    mmax = jnp.maximum(jnp.maximum(ms[0], ms[1]), ms[2])
    num = jnp.zeros_like(mmax)
    den = jnp.zeros_like(mmax)
    for p in range(3):
        wgt = jnp.exp(ms[p] - mmax)
        num = num + wgt * natural(us[p], uscr_ref, DILATED_PATTERNS[p][1])
        den = den + wgt * ls[p]
    attn = num / den
    mod = mod_ref[...]
    a_n = _rms(attn) * an_ref[...]
    rec = _rms(o_ref[...]) * hn_ref[...] * og_ref[...].astype(F32)
    merged = jnp.concatenate([a_n, rec], axis=1).astype(BF16)
    mix = jnp.dot(merged, w_ref[...], preferred_element_type=F32)
    x1 = x_ref[...] + mod[2:3] * mix
    x1_ref[...] = x1
    h2 = _rms(x1) * (1.0 + mod[4:5]) + mod[3:4]
    h2_ref[...] = h2

    logits = _dot_hi(wr_ref[...], h2, _NT) + br_ref[...]
    e = jnp.exp(logits - jnp.max(logits, axis=0, keepdims=True))
    probs = e / jnp.sum(e, axis=0, keepdims=True)
    e1, e2, w1, w2 = _route(probs)
    eidx_ref[...] = jnp.concatenate([e1, e2], axis=0)
    tm = probs.shape[1]
    srow = lax.broadcasted_iota(jnp.int32, (LANES, tm), 0)
    w_t = jnp.where(srow == 0, w1, jnp.where(srow == 1, w2, 0.0))
    wcol_ref[...] = w_t.T


def _outproj(us, sts, o_h, og, x2d, mod_l, an_l, hn_l, w_out_l, em, el, wr_t, br, seq):
    t, d = x2d.shape
    tm = TOKEN_TILE
    per_batch = seq // tm
    row = lambda i: (i, 0)
    const = lambda i: (0, 0)
    half = pl.BlockSpec((tm, 512), row)
    stat = pl.BlockSpec((tm, LANES), row)
    full = pl.BlockSpec((tm, d), row)
    u_specs = [_stream_spec(tm, dil, per_batch, ATTN_WIDTH) for _, dil in DILATED_PATTERNS]
    s_specs = [_stream_spec(tm, dil, per_batch, LANES) for _, dil in DILATED_PATTERNS]
    return pl.pallas_call(
        _outproj_kernel,
        grid=(t // tm,),
        in_specs=u_specs + s_specs + [half, half, full,
                  pl.BlockSpec((None, 6, d), lambda i: (i // per_batch, 0, 0)),
                  pl.BlockSpec((1, 512), const), pl.BlockSpec((1, 512), const),
                  pl.BlockSpec((d, d), const),
                  pl.BlockSpec((LANES, 512), const), pl.BlockSpec((LANES, 512), const),
                  pl.BlockSpec((N_EXPERTS, d), const), pl.BlockSpec((N_EXPERTS, 1), const)],
        out_specs=[full, full, pl.BlockSpec((N_ASSIGN, tm), lambda i: (0, i)), stat],
        out_shape=[jax.ShapeDtypeStruct((t, d), F32), jax.ShapeDtypeStruct((t, d), F32),
                   jax.ShapeDtypeStruct((N_ASSIGN, t), jnp.int32), jax.ShapeDtypeStruct((t, LANES), F32)],
        scratch_shapes=[pltpu.VMEM((ATTN_WIDTH // LANES, tm, LANES), F32), pltpu.VMEM((1, tm, LANES), F32)],
        compiler_params=_params("parallel"),
        name="outproj_route",
    )(*us, *sts, o_h, og, x2d, mod_l, an_l, hn_l, w_out_l, em, el, wr_t, br)


def _one_hots(e_ref):
    e = e_ref[...]
    erow = lax.broadcasted_iota(jnp.int32, (N_EXPERTS, e.shape[1]), 0)
    return (erow == e[0:1]).astype(F32), (erow == e[1:2]).astype(F32)


def _spread(col):
    return jnp.broadcast_to(col, (N_EXPERTS, LANES))


def _count_kernel(e_ref, cnt_ref):
    @pl.when(pl.program_id(0) == 0)
    def _():
        cnt_ref[...] = jnp.zeros_like(cnt_ref)

    oh0, oh1 = _one_hots(e_ref)
    cnt_ref[...] += _spread(jnp.sum(oh0 + oh1, axis=1, keepdims=True))


def _positions_kernel(e_ref, cnt_ref, pos_ref, meta_ref, carry_ref, offs_ref):
    i = pl.program_id(0)
    tp = e_ref.shape[1]
    oh0, oh1 = _one_hots(e_ref)

    @pl.when(i == 0)
    def _():
        cnt = cnt_ref[...]
        padded = jnp.floor((cnt + (MOE_TILE - 1)) * (1.0 / MOE_TILE)) * MOE_TILE
        run = jnp.zeros((1, LANES), F32)
        starts = []
        for ex in range(N_EXPERTS):
            starts.append(run)
            run = run + padded[ex:ex + 1]
        offs = jnp.concatenate(starts, axis=0)
        offs_ref[...] = offs
        carry_ref[...] = jnp.zeros_like(carry_ref)
        ends = offs + padded
        lane = lax.broadcasted_iota(jnp.int32, (N_EXPERTS, LANES), 1)
        srow = lax.broadcasted_iota(jnp.int32, (N_EXPERTS, LANES), 0)
        tile_start = (lane * MOE_TILE).astype(F32)
        tile_expert = jnp.sum((ends <= tile_start).astype(F32), axis=0, keepdims=True)
        tile_expert = jnp.minimum(tile_expert, N_EXPERTS - 1.0)
        on_diag = srow == lane
        ends_lane = jnp.sum(jnp.where(on_diag, ends, 0.0), axis=0, keepdims=True)
        pad_lane = jnp.sum(jnp.where(on_diag, padded, 0.0), axis=0, keepdims=True)
        meta = jnp.concatenate([tile_expert, run * (1.0 / MOE_TILE), ends_lane, pad_lane,
                                jnp.zeros((4, LANES), F32)], axis=0)
        meta_ref[...] = meta.astype(jnp.int32)

    r = lax.broadcasted_iota(jnp.int32, (tp, tp), 0)
    c = lax.broadcasted_iota(jnp.int32, (tp, tp), 1)
    upper = (r <= c).astype(BF16)
    oh = jnp.concatenate([oh0, oh1], axis=0).astype(BF16)
    pre = jnp.dot(oh, upper, preferred_element_type=F32)
    pre0, pre1 = pre[:N_EXPERTS], pre[N_EXPERTS:]
    tot0, tot1 = pre0[:, tp - 1:tp], pre1[:, tp - 1:tp]
    base = offs_ref[...][:, 0:1] + carry_ref[...][:, 0:1]
    p0 = jnp.sum(oh0 * (pre0 - 1.0 + base), axis=0, keepdims=True)
    p1 = jnp.sum(oh1 * (pre1 - 1.0 + (base + tot0)), axis=0, keepdims=True)
    pos_ref[...] = jnp.concatenate([p0, p1], axis=0).astype(jnp.int32)
    carry_ref[...] += _spread(tot0 + tot1)


def _positions(eidx):
    t = eidx.shape[1]
    tp = POS_TILE
    blk = pl.BlockSpec((N_ASSIGN, tp), lambda i: (0, i))
    whole = pl.BlockSpec((N_EXPERTS, LANES), lambda i: (0, 0))
    stat = pltpu.VMEM((N_EXPERTS, LANES), F32)
    cnt = pl.pallas_call(
        _count_kernel,
        grid=(t // tp,),
        in_specs=[blk],
        out_specs=whole,
        out_shape=jax.ShapeDtypeStruct((N_EXPERTS, LANES), F32),
        compiler_params=_params("arbitrary"),
        name="moe_count",
    )(eidx)
    return pl.pallas_call(
        _positions_kernel,
        grid=(t // tp,),
        in_specs=[blk, whole],
        out_specs=[blk, pl.BlockSpec((8, LANES), lambda i: (0, 0))],
        out_shape=[jax.ShapeDtypeStruct((N_ASSIGN, t), jnp.int32), jax.ShapeDtypeStruct((8, LANES), jnp.int32)],
        scratch_shapes=[stat, stat],
        compiler_params=_params("arbitrary"),
        name="moe_positions",
    )(eidx, cnt)


def _row(ref, r):
    return ref.at[pl.ds(r, 1), :]


def _dispatch_kernel(ends_ref, pad_ref, used_ref, pos_ref, h_ref, xs_hbm, zero_ref, sem):
    i = pl.program_id(0)
    tp = pos_ref.shape[1]

    def clear_tile(start):
        cp = pltpu.make_async_copy(zero_ref, xs_hbm.at[pl.ds(pl.multiple_of(start, MOE_TILE), MOE_TILE), :], sem)
        cp.start()
        cp.wait()

    @pl.when(i == 0)
    def _():
        zero_ref[...] = jnp.zeros_like(zero_ref)
        for ex in range(N_EXPERTS):
            @pl.when(pad_ref[ex] > 0)
            def _():
                clear_tile(ends_ref[ex] - MOE_TILE)

        def clear_tail(tile, carry):
            clear_tile(tile * MOE_TILE)
            return carry

        lax.fori_loop(used_ref[0], xs_hbm.shape[0] // MOE_TILE, clear_tail, 0)

    def issue(j, carry):
        for k in range(N_ASSIGN):
            pltpu.make_async_copy(_row(h_ref, j), _row(xs_hbm, pos_ref[k, j]), sem).start()
        return carry

    def drain(j, carry):
        for k in range(N_ASSIGN):
            pltpu.make_async_copy(_row(h_ref, 0), _row(xs_hbm, 0), sem).wait()
        return carry

    lax.fori_loop(0, tp, issue, 0, unroll=8)
    lax.fori_loop(0, tp, drain, 0, unroll=8)


def _dispatch(ends, padded, n_used, pos, h2):
    t, d = h2.shape
    tp = DMA_TILE
    rows = N_ASSIGN * t + N_EXPERTS * MOE_TILE
    return pl.pallas_call(
        _dispatch_kernel,
        grid_spec=pltpu.PrefetchScalarGridSpec(
            num_scalar_prefetch=3,
            grid=(t // tp,),
            in_specs=[pl.BlockSpec((N_ASSIGN, tp), lambda i, *_: (0, i), memory_space=pltpu.SMEM),
                      pl.BlockSpec((tp, d), lambda i, *_: (i, 0))],
            out_specs=pl.BlockSpec(memory_space=pl.ANY),
            scratch_shapes=[pltpu.VMEM((MOE_TILE, d), F32), pltpu.SemaphoreType.DMA(())],
        ),
        out_shape=jax.ShapeDtypeStruct((rows, d), F32),
        compiler_params=_params("arbitrary"),
        name="moe_dispatch",
    )(ends, padded, n_used, pos, h2)


def _experts_kernel(te_ref, nv_ref, xs_ref, wg_ref, wu_ref, wd_ref, ys_ref, wgb_ref, wub_ref, wdb_ref):
    j = pl.program_id(0)
    used = nv_ref[0]
    jc = jnp.minimum(j, used - 1)
    new_expert = jnp.logical_or(j == 0, te_ref[jc] != te_ref[jnp.maximum(jc - 1, 0)])

    @pl.when(jnp.logical_and(j < used, new_expert))
    def _():
        wgb_ref[...] = wg_ref[...].astype(BF16)
        wub_ref[...] = wu_ref[...].astype(BF16)
        wdb_ref[...] = wd_ref[...].astype(BF16)

    @pl.when(j < used)
    def _():
        xb = xs_ref[...].astype(BF16)
        gt = jnp.dot(xb, wgb_ref[...], preferred_element_type=F32)
        up = jnp.dot(xb, wub_ref[...], preferred_element_type=F32)
        a = (gt * _sigmoid(gt) * up).astype(BF16)
        ys_ref[...] = jnp.dot(a, wdb_ref[...], preferred_element_type=F32)

    @pl.when(j >= used)
    def _():
        ys_ref[...] = jnp.zeros_like(ys_ref)


def _experts(tile_expert, n_used, xs, wg_l, wu_l, wd_l):
    rows, d = xs.shape
    f = D_FF_EXPERT
    tile = lambda j, te, nv: (jnp.minimum(j, nv[0] - 1), 0)
    wsel = lambda j, te, nv: (te[jnp.minimum(j, nv[0] - 1)], 0, 0)
    return pl.pallas_call(
        _experts_kernel,
        grid_spec=pltpu.PrefetchScalarGridSpec(
            num_scalar_prefetch=2,
            grid=(rows // MOE_TILE,),
            in_specs=[pl.BlockSpec((MOE_TILE, d), tile),
                      pl.BlockSpec((None, d, f), wsel), pl.BlockSpec((None, d, f), wsel),
                      pl.BlockSpec((None, f, d), wsel)],
            out_specs=pl.BlockSpec((MOE_TILE, d), lambda j, te, nv: (j, 0)),
            scratch_shapes=[pltpu.VMEM((d, f), BF16), pltpu.VMEM((d, f), BF16), pltpu.VMEM((f, d), BF16)],
        ),
        out_shape=jax.ShapeDtypeStruct((rows, d), F32),
        compiler_params=_params("arbitrary"),
        name="moe_experts",
    )(tile_expert, n_used, xs, wg_l, wu_l, wd_l)


def _combine_kernel(final, pos_ref, ys_hbm, wcol_ref, x_ref, mod_ref, fn_ref, o_ref, buf_ref, sem):
    tc = pos_ref.shape[1]

    def issue(j, carry):
        for k in range(N_ASSIGN):
            pltpu.make_async_copy(_row(ys_hbm, pos_ref[k, j]), buf_ref.at[k, pl.ds(j, 1), :], sem).start()
        return carry

    def drain(j, carry):
        for k in range(N_ASSIGN):
            pltpu.make_async_copy(_row(ys_hbm, 0), buf_ref.at[k, pl.ds(0, 1), :], sem).wait()
        return carry

    lax.fori_loop(0, tc, issue, 0, unroll=8)
    lax.fori_loop(0, tc, drain, 0, unroll=8)
    w = wcol_ref[...]
    ffn = w[:, 0:1] * buf_ref[0] + w[:, 1:2] * buf_ref[1]
    y = x_ref[...] + mod_ref[...][5:6] * ffn
    if final:
        y = _rms(y) * fn_ref[...]
    o_ref[...] = y


def _combine(pos, ys, wcol, x1, mod_l, fnorm, seq, final):
    t, d = x1.shape
    tc = DMA_TILE
    per_batch = seq // tc
    row = lambda i: (i, 0)
    return pl.pallas_call(
        functools.partial(_combine_kernel, final),
        grid=(t // tc,),
        in_specs=[pl.BlockSpec((N_ASSIGN, tc), lambda i: (0, i), memory_space=pltpu.SMEM),
                  pl.BlockSpec(memory_space=pl.ANY),
                  pl.BlockSpec((tc, LANES), row), pl.BlockSpec((tc, d), row),
                  pl.BlockSpec((None, 6, d), lambda i: (i // per_batch, 0, 0)),
                  pl.BlockSpec((1, d), lambda i: (0, 0))],
        out_specs=pl.BlockSpec((tc, d), row),
        out_shape=jax.ShapeDtypeStruct((t, d), F32),
        scratch_shapes=[pltpu.VMEM((N_ASSIGN, tc, d), F32), pltpu.SemaphoreType.DMA(())],
        compiler_params=_params("arbitrary"),
        name="moe_combine",
    )(pos, ys, wcol, x1, mod_l, fnorm)


def _moe(h2, eidx, wcol, wg_l, wu_l, wd_l, x1, mod_l, fnorm, seq, final):
    pos, meta = _positions(eidx)
    xs = _dispatch(meta[2], meta[3], meta[1], pos, h2)
    ys = _experts(meta[0], meta[1], xs, wg_l, wu_l, wd_l)
    return _combine(pos, ys, wcol, x1, mod_l, fnorm, seq, final)


def _head_expand(offset):
    m = np.zeros((LANES, ATTN_WIDTH), np.float32)
    for h in range(N_HEADS):
        m[offset + h, h * HEAD_DIM:(h + 1) * HEAD_DIM] = 1.0
    return jnp.asarray(m, BF16)


def kernel(x, c, positions, w_in, w_out, attn_norm, hgrn_norm, lb_params, ada_w, ada_b,
           w_router, b_router, w_gate, w_up, w_down, final_norm):
    batch, seq, d = x.shape
    t = batch * seq
    c_pad = jnp.pad(c, ((0, 8 - batch), (0, 0)))
    mod = _ada_mod(c_pad, ada_w, ada_b)[:, :batch].reshape(DEPTH, batch, 6, d)
    lbs = _lower_bounds(lb_params)
    cos, sin = _rope_tables(positions)
    cos, sin = cos.reshape(t, LANES), sin.reshape(t, LANES)
    head_id = np.arange(MXU_WIDTH) // HEAD_DIM
    ones_bd = jnp.asarray(head_id[:, None] == head_id[None, :], BF16)
    em, el = _head_expand(0), _head_expand(N_HEADS)
    wr_t = w_router.T
    br = b_router.reshape(N_EXPERTS, 1)
    fnorm = final_norm.reshape(1, d)

    x2d = x.reshape(t, d)
    for l in range(DEPTH):
        outs = _inproj(x2d, mod[l], w_in[l].astype(BF16), cos, sin, lbs[l:l + 1], seq)
        qs, ks, vs = outs[0:3], outs[3:6], outs[6:9]
        qh, kh, gh, ih, og = outs[9:]
        us, sts = [], []
        for p, (window, dilation) in enumerate(DILATED_PATTERNS):
            u, st = _attn_pattern(qs[p], ks[p], vs[p], window, dilation)
            us.append(u)
            sts.append(st)
        o_h = _hgrn(qh, kh, gh, ih, ones_bd, batch, seq)
        x1, h2, eidx, wcol = _outproj(us, sts, o_h, og, x2d, mod[l], attn_norm[l:l + 1], hgrn_norm[l:l + 1],
                                      w_out[l].astype(BF16), em, el, wr_t, br, seq)
        x2d = _moe(h2, eidx, wcol, w_gate[l], w_up[l], w_down[l], x1, mod[l], fnorm, seq,
                   final=(l == DEPTH - 1))
    return x2d.reshape(batch, seq, d)
```

```python
import functools

import numpy as np
import jax
import jax.numpy as jnp
from jax import lax
from jax.experimental import pallas as pl
from jax.experimental.pallas import tpu as pltpu

D_MODEL = 1024
DEPTH = 2
ATTN_WIDTH = 512
HGRN_WIDTH = 512
HEAD_DIM = 64
N_HEADS = 8
DILATED_PATTERNS = ((128, 1), (512, 4), (2048, 16))
ATTN_BLOCK = 128
ROPE_THETA = 10000.0
N_EXPERTS = 16
N_GROUPS = 4
EXPERTS_PER_GROUP = 4
D_FF_EXPERT = 512
RMS_EPS = 1e-6
IN_COLS = 3 * ATTN_WIDTH + 4 * HGRN_WIDTH

LANES = 128
SUBLANES = 8
MXU_WIDTH = 256
VMEM_LIMIT_BYTES = 56 * 1024 * 1024

TOKEN_TILE = 512
N_ASSIGN = 2
MOE_TILE = 512
POS_TILE = 1024
DMA_TILE = 512
HGRN_CHUNK = 64
HGRN_SUB = 8
NEG_BIG = -1e30
LOG2E = 1.4426950408889634

F32 = jnp.float32
BF16 = jnp.bfloat16

_NT = (((1,), (1,)), ((), ()))
_TN = (((0,), (0,)), ((), ()))


def _params(*sem):
    return pltpu.CompilerParams(dimension_semantics=sem, vmem_limit_bytes=VMEM_LIMIT_BYTES)


def _sigmoid(x):
    return 1.0 / (1.0 + jnp.exp(-x))


def _rms(x):
    return x * lax.rsqrt(jnp.mean(x * x, axis=-1, keepdims=True) + RMS_EPS)


def _split3(a):
    a1 = a.astype(BF16)
    r1 = a - a1.astype(F32)
    a2 = r1.astype(BF16)
    a3 = (r1 - a2.astype(F32)).astype(BF16)
    return a1, a2, a3


def _dot_hi(a, b, dims):
    a1, a2, _ = _split3(a)
    b1, b2, _ = _split3(b)
    d = lambda p, q: lax.dot_general(p, q, dims, preferred_element_type=F32)
    return d(a1, b1) + (d(a2, b1) + d(a1, b2))


def _ada_kernel(c_ref, w_ref, b_ref, o_ref):
    c = c_ref[...]
    ca = c * _sigmoid(c)
    o_ref[...] = _dot_hi(ca, w_ref[...], (((1,), (0,)), ((), ()))) + b_ref[...]


def _ada_mod(c_pad, ada_w, ada_b):
    depth, d, n = ada_w.shape
    tn = 1536
    rows = c_pad.shape[0]
    return pl.pallas_call(
        _ada_kernel,
        grid=(depth, n // tn),
        in_specs=[
            pl.BlockSpec((rows, d), lambda l, j: (0, 0)),
            pl.BlockSpec((None, d, tn), lambda l, j: (l, 0, j)),
            pl.BlockSpec((None, 1, tn), lambda l, j: (l, 0, j)),
        ],
        out_specs=pl.BlockSpec((None, rows, tn), lambda l, j: (l, 0, j)),
        out_shape=jax.ShapeDtypeStruct((depth, rows, n), F32),
        compiler_params=_params("parallel", "parallel"),
        name="ada_mod",
    )(c_pad, ada_w, ada_b.reshape(depth, 1, n))


def _lb_kernel(p_ref, o_ref):
    p = p_ref[...]
    e = jnp.exp(p - jnp.max(p, axis=0, keepdims=True))
    sm = e / jnp.sum(e, axis=0, keepdims=True)
    run = jnp.zeros_like(sm[0:1])
    for l in range(p.shape[0]):
        run = run + sm[l:l + 1]
        o_ref[l:l + 1, :] = run - sm[0:1]


def _lower_bounds(lb_params):
    return pl.pallas_call(
        _lb_kernel,
        out_shape=jax.ShapeDtypeStruct(lb_params.shape, F32),
        name="lower_bounds",
    )(lb_params)


def _rope_kernel(pos_ref, inv_ref, cos_ref, sin_ref):
    ang = pos_ref[...].astype(F32) * inv_ref[...]
    lane = lax.broadcasted_iota(jnp.int32, ang.shape, 1)
    first = (lane % HEAD_DIM) < (HEAD_DIM // 2)
    s = jnp.sin(ang)
    cos_ref[...] = jnp.cos(ang)
    sin_ref[...] = jnp.where(first, -s, s)


def _rope_tables(positions):
    b, s = positions.shape
    ts = 1024
    half = HEAD_DIM // 2
    inv = ROPE_THETA ** (-jnp.arange(half, dtype=F32) / half)
    inv = jnp.tile(inv, LANES // half).reshape(1, LANES)
    out = jax.ShapeDtypeStruct((b, s, LANES), F32)
    return pl.pallas_call(
        _rope_kernel,
        grid=(b, s // ts),
        in_specs=[
            pl.BlockSpec((None, ts, 1), lambda i, j: (i, j, 0)),
            pl.BlockSpec((1, LANES), lambda i, j: (0, 0)),
        ],
        out_specs=[pl.BlockSpec((None, ts, LANES), lambda i, j: (i, j, 0))] * 2,
        out_shape=[out, out],
        compiler_params=_params("parallel", "parallel"),
        name="rope_tables",
    )(positions.reshape(b, s, 1), inv)


def _store_streams(scr_ref, val, refs):
    tm = val.shape[0]
    slabs = scr_ref.shape[0]
    for c in range(slabs):
        scr_ref[c] = val[:, c * LANES:(c + 1) * LANES]
    for (_, d), ref in zip(DILATED_PATTERNS, refs):
        if d == 1:
            ref[0] = val.astype(BF16)
        else:
            for r in range(d):
                rows = [scr_ref[c, pl.ds(r, tm // d, stride=d), :] for c in range(slabs)]
                ref[r] = jnp.concatenate(rows, axis=1).astype(BF16)


def _inproj_kernel(x_ref, mod_ref, w_ref, cos_ref, sin_ref, lb_ref,
                   q1_ref, q4_ref, q16_ref, k1_ref, k4_ref, k16_ref, v1_ref, v4_ref, v16_ref,
                   qh_ref, kh_ref, gh_ref, ih_ref, og_ref, scr_ref):
    x = x_ref[...]
    mod = mod_ref[...]
    h = _rms(x) * (1.0 + mod[1:2]) + mod[0:1]
    hb = h.astype(BF16)
    reps = ATTN_WIDTH // LANES
    cos = jnp.concatenate([cos_ref[...]] * reps, axis=1)
    sin = jnp.concatenate([sin_ref[...]] * reps, axis=1)
    lane = lax.broadcasted_iota(jnp.int32, cos.shape, 1)
    first = (lane % HEAD_DIM) < (HEAD_DIM // 2)
    half = HEAD_DIM // 2

    def proj(j):
        return jnp.dot(hb, w_ref[:, j * 512:(j + 1) * 512], preferred_element_type=F32)

    def rot(t):
        swapped = jnp.where(first, pltpu.roll(t, ATTN_WIDTH - half, 1), pltpu.roll(t, half, 1))
        return t * cos + swapped * sin

    _store_streams(scr_ref, rot(proj(0)) * (HEAD_DIM ** -0.5 * LOG2E), (q1_ref, q4_ref, q16_ref))
    _store_streams(scr_ref, rot(proj(1)), (k1_ref, k4_ref, k16_ref))
    _store_streams(scr_ref, proj(2), (v1_ref, v4_ref, v16_ref))
    qh = proj(3)
    qh_ref[...] = (qh * _sigmoid(qh)).astype(BF16)
    lb = lb_ref[...]
    f = lb + (1.0 - lb) * _sigmoid(proj(4))
    kh_ref[...] = (1.0 - f).astype(BF16)
    gh_ref[...] = jnp.log(f)
    ih_ref[...] = proj(5).astype(BF16)
    og_ref[...] = _sigmoid(proj(6)).astype(BF16)


def _stream_spec(tm, dil, per_batch, width):
    return pl.BlockSpec((None, dil, tm // dil, width), lambda i: (i // per_batch, 0, i % per_batch, 0))


def _inproj(x2d, mod_l, w_in_l, cos, sin, lb_l, seq):
    t, d = x2d.shape
    tm = TOKEN_TILE
    per_batch = seq // tm
    batch = t // seq
    row = lambda i: (i, 0)
    half_spec = pl.BlockSpec((tm, 512), row)
    bf = jax.ShapeDtypeStruct((t, 512), BF16)
    stream_specs = [_stream_spec(tm, dil, per_batch, ATTN_WIDTH) for _, dil in DILATED_PATTERNS]
    stream_shapes = [jax.ShapeDtypeStruct((batch, dil, seq // dil, ATTN_WIDTH), BF16)
                     for _, dil in DILATED_PATTERNS]
    return pl.pallas_call(
        _inproj_kernel,
        grid=(t // tm,),
        in_specs=[
            pl.BlockSpec((tm, d), row),
            pl.BlockSpec((None, 6, d), lambda i: (i // per_batch, 0, 0)),
            pl.BlockSpec((d, IN_COLS), lambda i: (0, 0)),
            pl.BlockSpec((tm, LANES), row),
            pl.BlockSpec((tm, LANES), row),
            pl.BlockSpec((1, 512), lambda i: (0, 0)),
        ],
        out_specs=stream_specs * 3 + [half_spec] * 5,
        out_shape=stream_shapes * 3 + [bf, bf, jax.ShapeDtypeStruct((t, 512), F32), bf, bf],
        scratch_shapes=[pltpu.VMEM((ATTN_WIDTH // LANES, tm, LANES), F32)],
        compiler_params=_params("parallel"),
        name="inproj",
    )(x2d, mod_l, w_in_l, cos, sin, lb_l)


def _attn_kernel(q_ref, kc_ref, kp_ref, vc_ref, vp_ref, bias_ref, u_ref, st_ref):
    blk = ATTN_BLOCK
    q = q_ref[...]
    kk = jnp.concatenate([kp_ref[...], kc_ref[...]], axis=0)
    vv = jnp.concatenate([vp_ref[...], vc_ref[...]], axis=0)
    bias = bias_ref[...]
    lane = lax.broadcasted_iota(jnp.int32, (blk, LANES), 1)
    low = lane < HEAD_DIM
    st = jnp.zeros((blk, LANES), F32)
    zero = jnp.zeros((), BF16)
    ones = jnp.ones((2 * blk, LANES), BF16)
    slabs = [slice(pair * LANES, (pair + 1) * LANES) for pair in range(N_HEADS // 2)]
    scores = []
    for sl in slabs:
        qp = q[:, sl]
        qst = jnp.concatenate([jnp.where(low, qp, zero), jnp.where(low, zero, qp)], axis=0)
        scores.append(lax.dot_general(qst, kk[:, sl], _NT, preferred_element_type=F32) + bias)
    maxes, probs = [], []
    for s in scores:
        m = jnp.max(s, axis=1, keepdims=True).astype(BF16).astype(F32)
        maxes.append(m)
        probs.append(jnp.exp2((s - m).astype(BF16)))
    for pair, (sl, m, p) in enumerate(zip(slabs, maxes, probs)):
        ul = jnp.dot(p, jnp.concatenate([vv[:, sl], ones], axis=1), preferred_element_type=F32)
        u, l = ul[:, :LANES], ul[:, LANES:]
        for hh in range(2):
            head = 2 * pair + hh
            st = jnp.where(lane == head, m[hh * blk:(hh + 1) * blk], st)
            st = jnp.where(lane == N_HEADS + head, l[hh * blk:(hh + 1) * blk], st)
        u_ref[:, sl] = jnp.where(low, u[:blk], u[blk:]).astype(BF16)
    st_ref[...] = st


def _band_bias(steps):
    blk = ATTN_BLOCK
    qi = np.arange(2 * blk)[:, None] % blk
    kj = np.arange(2 * blk)[None, :]
    dist = qi + blk - kj
    band = (dist >= 0) & (dist <= steps)
    first = band & (kj >= blk)
    return jnp.asarray(np.where(np.stack([first, band]), 0.0, NEG_BIG), F32)


def _attn_pattern(qs, ks, vs, window, dilation):
    batch, _, m, _ = qs.shape
    nb = m // ATTN_BLOCK
    bias = _band_bias(window // dilation)
    cur = pl.BlockSpec((None, None, ATTN_BLOCK, ATTN_WIDTH), lambda b, r, n: (b, r, n, 0))
    prev = pl.BlockSpec((None, None, ATTN_BLOCK, ATTN_WIDTH),
                        lambda b, r, n: (b, r, jnp.maximum(n - 1, 0), 0))
    return pl.pallas_call(
        _attn_kernel,
        grid=(batch, dilation, nb),
        in_specs=[cur, cur, prev, cur, prev,
                  pl.BlockSpec((None,) + bias.shape[1:], lambda b, r, n: (jnp.minimum(n, 1), 0, 0))],
        out_specs=[cur, pl.BlockSpec((None, None, ATTN_BLOCK, LANES), lambda b, r, n: (b, r, n, 0))],
        out_shape=[jax.ShapeDtypeStruct((batch, dilation, m, ATTN_WIDTH), BF16),
                   jax.ShapeDtypeStruct((batch, dilation, m, LANES), F32)],
        compiler_params=_params("parallel", "parallel", "arbitrary"),
        name=f"attn_d{dilation}",
    )(qs, ks, ks, vs, vs, bias)


def _hgrn_kernel(q_ref, k_ref, g_ref, v_ref, ones_ref, o_ref, st_ref):
    c, sub, w = HGRN_CHUNK, HGRN_SUB, HGRN_WIDTH

    @pl.when(pl.program_id(1) == 0)
    def _():
        st_ref[...] = jnp.zeros_like(st_ref)

    q = q_ref[...].astype(F32)
    k = k_ref[...].astype(F32)
    vb = v_ref[...]
    v = vb.astype(F32)
    g = g_ref[...]

    ri = lax.broadcasted_iota(jnp.int32, (c, c), 0)
    ci = lax.broadcasted_iota(jnp.int32, (c, c), 1)
    tri = (ci <= ri).astype(BF16)
    g1, g2, g3 = _split3(g)
    cs = lambda t: jnp.dot(tri, t, preferred_element_type=F32)
    b = (cs(g1) + (cs(g2) + cs(g3))) * LOG2E
    b_last = b[c - 1:c, :]

    state = st_ref[...]
    o_inter = lax.dot_general((q * jnp.exp2(b)).astype(BF16), state.astype(BF16), _NT,
                              preferred_element_type=F32)
    kt = (k * jnp.exp2(b_last - b)).astype(BF16)
    upd = lax.dot_general(vb, kt, _TN, preferred_element_type=F32)
    r0i = lax.broadcasted_iota(jnp.int32, (w, w), 0) // HEAD_DIM
    c0i = lax.broadcasted_iota(jnp.int32, (w, w), 1) // HEAD_DIM
    st_ref[...] = state * jnp.exp2(b_last) + jnp.where(r0i == c0i, upd, 0.0)

    hrow = lax.broadcasted_iota(jnp.int32, (N_HEADS * sub, w), 0) // sub
    hlane = lax.broadcasted_iota(jnp.int32, (N_HEADS * sub, w), 1) // HEAD_DIM
    hmask = hrow == hlane
    trow = lax.broadcasted_iota(jnp.int32, (sub, w), 0)
    ones = ones_ref[...]
    ow = ones.shape[0]

    nblk = c // sub
    ws = []
    for blk in range(nblk):
        rows = slice(blk * sub, (blk + 1) * sub)
        bi, qi, ki = b[rows], q[rows], k[rows]
        for s in range(sub):
            e = jnp.exp2(bi - bi[s:s + 1])
            ws.append(jnp.where(trow >= s, qi * (ki[s:s + 1] * e), 0.0))
    wcat = jnp.concatenate(ws, axis=0).astype(BF16)
    sc = jnp.concatenate(
        [jnp.dot(wcat[:, j * ow:(j + 1) * ow], ones, preferred_element_type=F32) for j in range(w // ow)],
        axis=1)

    inter_scores = [None]
    for blk in range(1, nblk):
        r0 = blk * sub
        rows = slice(r0, r0 + sub)
        bref = b[r0:r0 + 1]
        qs = q[rows] * jnp.exp2(b[rows] - bref)
        kp = (k[0:r0] * jnp.exp2(bref - b[0:r0])).astype(BF16)
        qexp = jnp.where(hmask, jnp.concatenate([qs] * N_HEADS, axis=0), 0.0).astype(BF16)
        inter_scores.append(lax.dot_general(qexp, kp, _NT, preferred_element_type=F32))

    for blk in range(nblk):
        r0 = blk * sub
        rows = slice(r0, r0 + sub)
        acc = o_inter[rows]
        if blk > 0:
            oexp = jnp.dot(inter_scores[blk].astype(BF16), vb[0:r0], preferred_element_type=F32)
            oexp = jnp.where(hmask, oexp, 0.0)
            for hd in range(N_HEADS):
                acc = acc + oexp[hd * sub:(hd + 1) * sub]
        vi = v[rows]
        for s in range(sub):
            at = (blk * sub + s) * sub
            acc = acc + sc[at:at + sub] * vi[s:s + 1]
        o_ref[rows, :] = acc


def _hgrn(qh, kh, gh, ih, ones_bd, batch, seq):
    c, w = HGRN_CHUNK, HGRN_WIDTH
    view = lambda a: a.reshape(batch, seq, w)
    blk = pl.BlockSpec((None, c, w), lambda b, n: (b, n, 0))
    o = pl.pallas_call(
        _hgrn_kernel,
        grid=(batch, seq // c),
        in_specs=[blk, blk, blk, blk, pl.BlockSpec(ones_bd.shape, lambda b, n: (0, 0))],
        out_specs=blk,
        out_shape=jax.ShapeDtypeStruct((batch, seq, w), F32),
        scratch_shapes=[pltpu.VMEM((w, w), F32)],
        compiler_params=_params("parallel", "arbitrary"),
        name="hgrn2",
    )(view(qh), view(kh), view(gh), view(ih), ones_bd)
    return o.reshape(batch * seq, w)


def _route(probs):
    rows = [probs[i:i + 1, :] for i in range(N_EXPERTS)]
    gsum = []
    for gidx in range(N_GROUPS):
        a, b_, c_, d_ = rows[4 * gidx:4 * gidx + 4]
        hi1, lo1 = jnp.maximum(a, b_), jnp.minimum(a, b_)
        hi2, lo2 = jnp.maximum(c_, d_), jnp.minimum(c_, d_)
        top1 = jnp.maximum(hi1, hi2)
        second = jnp.maximum(jnp.minimum(hi1, hi2), jnp.maximum(lo1, lo2))
        gsum.append(top1 + second)
    best, gi = gsum[0], jnp.zeros(gsum[0].shape, jnp.int32)
    for gidx in range(1, N_GROUPS):
        upd = gsum[gidx] > best
        best = jnp.where(upd, gsum[gidx], best)
        gi = jnp.where(upd, gidx, gi)
    vals = []
    for j in range(EXPERTS_PER_GROUP):
        vj = rows[j]
        for gidx in range(1, N_GROUPS):
            vj = jnp.where(gi == gidx, rows[4 * gidx + j], vj)
        vals.append(vj)
    v1, i1 = vals[0], jnp.zeros(gi.shape, jnp.int32)
    for j in range(1, EXPERTS_PER_GROUP):
        upd = vals[j] > v1
        v1 = jnp.where(upd, vals[j], v1)
        i1 = jnp.where(upd, j, i1)
    v2, i2 = jnp.full(v1.shape, -1.0, F32), jnp.zeros(gi.shape, jnp.int32)
    for j in range(EXPERTS_PER_GROUP):
        upd = (i1 != j) & (vals[j] > v2)
        v2 = jnp.where(upd, vals[j], v2)
        i2 = jnp.where(upd, j, i2)
    tot = v1 + v2
    base = gi * EXPERTS_PER_GROUP
    return base + i1, base + i2, v1 / tot, v2 / tot


def _outproj_kernel(u1_ref, u2_ref, u3_ref, s1_ref, s2_ref, s3_ref, o_ref, og_ref, x_ref, mod_ref,
                    an_ref, hn_ref, w_ref, em_ref, el_ref, wr_ref, br_ref,
                    x1_ref, h2_ref, eidx_ref, wcol_ref, uscr_ref, sscr_ref):
    us = (u1_ref, u2_ref, u3_ref)
    ss = (s1_ref, s2_ref, s3_ref)
    em, el = em_ref[...], el_ref[...]

    def natural(ref, scr, dil):
        if dil == 1:
            return ref[0].astype(F32)
        slabs = scr.shape[0]
        for r in range(dil):
            val = ref[r].astype(F32)
            for c in range(slabs):
                scr[c, pl.ds(r, ref.shape[1], stride=dil), :] = val[:, c * LANES:(c + 1) * LANES]
        return jnp.concatenate([scr[c] for c in range(slabs)], axis=1)

    ms, ls = [], []
    for s_ref, (_, dil) in zip(ss, DILATED_PATTERNS):
        m1, m2, _ = _split3(natural(s_ref, sscr_ref, dil))
        ms.append(jnp.dot(m1, em, preferred_element_type=F32))
        ls.append(jnp.dot(m1, el, preferred_element_type=F32) + jnp.dot(m2, el, preferred_element_type=F32))
    mmax = jnp.maximum(jnp.maximum(ms[0], ms[1]), ms[2])
    num = jnp.zeros_like(mmax)
    den = jnp.zeros_like(mmax)
    for p in range(3):
        wgt = jnp.exp2(ms[p] - mmax)
        num = num + wgt * natural(us[p], uscr_ref, DILATED_PATTERNS[p][1])
        den = den + wgt * ls[p]
    attn = num / den
    mod = mod_ref[...]
    a_n = _rms(attn) * an_ref[...]
    rec = _rms(o_ref[...]) * hn_ref[...] * og_ref[...].astype(F32)
    merged = jnp.concatenate([a_n, rec], axis=1).astype(BF16)
    mix = jnp.dot(merged, w_ref[...], preferred_element_type=F32)
    x1 = x_ref[...] + mod[2:3] * mix
    x1_ref[...] = x1
    h2 = _rms(x1) * (1.0 + mod[4:5]) + mod[3:4]
    h2_ref[...] = h2

    logits = _dot_hi(wr_ref[...], h2, _NT) + br_ref[...]
    e = jnp.exp(logits - jnp.max(logits, axis=0, keepdims=True))
    probs = e / jnp.sum(e, axis=0, keepdims=True)
    e1, e2, w1, w2 = _route(probs)
    eidx_ref[...] = jnp.concatenate([e1, e2], axis=0)
    tm = probs.shape[1]
    srow = lax.broadcasted_iota(jnp.int32, (LANES, tm), 0)
    w_t = jnp.where(srow == 0, w1, jnp.where(srow == 1, w2, 0.0))
    wcol_ref[...] = w_t.T


def _outproj(us, sts, o_h, og, x2d, mod_l, an_l, hn_l, w_out_l, em, el, wr_t, br, seq):
    t, d = x2d.shape
    tm = TOKEN_TILE
    per_batch = seq // tm
    row = lambda i: (i, 0)
    const = lambda i: (0, 0)
    half = pl.BlockSpec((tm, 512), row)
    stat = pl.BlockSpec((tm, LANES), row)
    full = pl.BlockSpec((tm, d), row)
    u_specs = [_stream_spec(tm, dil, per_batch, ATTN_WIDTH) for _, dil in DILATED_PATTERNS]
    s_specs = [_stream_spec(tm, dil, per_batch, LANES) for _, dil in DILATED_PATTERNS]
    return pl.pallas_call(
        _outproj_kernel,
        grid=(t // tm,),
        in_specs=u_specs + s_specs + [half, half, full,
                  pl.BlockSpec((None, 6, d), lambda i: (i // per_batch, 0, 0)),
                  pl.BlockSpec((1, 512), const), pl.BlockSpec((1, 512), const),
                  pl.BlockSpec((d, d), const),
                  pl.BlockSpec((LANES, 512), const), pl.BlockSpec((LANES, 512), const),
                  pl.BlockSpec((N_EXPERTS, d), const), pl.BlockSpec((N_EXPERTS, 1), const)],
        out_specs=[full, full, pl.BlockSpec((N_ASSIGN, tm), lambda i: (0, i)), stat],
        out_shape=[jax.ShapeDtypeStruct((t, d), F32), jax.ShapeDtypeStruct((t, d), F32),
                   jax.ShapeDtypeStruct((N_ASSIGN, t), jnp.int32), jax.ShapeDtypeStruct((t, LANES), F32)],
        scratch_shapes=[pltpu.VMEM((ATTN_WIDTH // LANES, tm, LANES), F32), pltpu.VMEM((1, tm, LANES), F32)],
        compiler_params=_params("parallel"),
        name="outproj_route",
    )(*us, *sts, o_h, og, x2d, mod_l, an_l, hn_l, w_out_l, em, el, wr_t, br)


def _one_hots(e_ref):
    e = e_ref[...]
    erow = lax.broadcasted_iota(jnp.int32, (N_EXPERTS, e.shape[1]), 0)
    return (erow == e[0:1]).astype(F32), (erow == e[1:2]).astype(F32)


def _spread(col):
    return jnp.broadcast_to(col, (N_EXPERTS, LANES))


def _count_kernel(e_ref, cnt_ref):
    @pl.when(pl.program_id(0) == 0)
    def _():
        cnt_ref[...] = jnp.zeros_like(cnt_ref)

    oh0, oh1 = _one_hots(e_ref)
    cnt_ref[...] += _spread(jnp.sum(oh0 + oh1, axis=1, keepdims=True))


def _positions_kernel(e_ref, cnt_ref, pos_ref, meta_ref, carry_ref, offs_ref):
    i = pl.program_id(0)
    tp = e_ref.shape[1]
    oh0, oh1 = _one_hots(e_ref)

    @pl.when(i == 0)
    def _():
        cnt = cnt_ref[...]
        padded = jnp.floor((cnt + (MOE_TILE - 1)) * (1.0 / MOE_TILE)) * MOE_TILE
        run = jnp.zeros((1, LANES), F32)
        starts = []
        for ex in range(N_EXPERTS):
            starts.append(run)
            run = run + padded[ex:ex + 1]
        offs = jnp.concatenate(starts, axis=0)
        offs_ref[...] = offs
        carry_ref[...] = jnp.zeros_like(carry_ref)
        ends = offs + padded
        lane = lax.broadcasted_iota(jnp.int32, (N_EXPERTS, LANES), 1)
        srow = lax.broadcasted_iota(jnp.int32, (N_EXPERTS, LANES), 0)
        tile_start = (lane * MOE_TILE).astype(F32)
        tile_expert = jnp.sum((ends <= tile_start).astype(F32), axis=0, keepdims=True)
        tile_expert = jnp.minimum(tile_expert, N_EXPERTS - 1.0)
        on_diag = srow == lane
        ends_lane = jnp.sum(jnp.where(on_diag, ends, 0.0), axis=0, keepdims=True)
        pad_lane = jnp.sum(jnp.where(on_diag, padded, 0.0), axis=0, keepdims=True)
        meta = jnp.concatenate([tile_expert, run * (1.0 / MOE_TILE), ends_lane, pad_lane,
                                jnp.zeros((4, LANES), F32)], axis=0)
        meta_ref[...] = meta.astype(jnp.int32)

    r = lax.broadcasted_iota(jnp.int32, (tp, tp), 0)
    c = lax.broadcasted_iota(jnp.int32, (tp, tp), 1)
    upper = (r <= c).astype(BF16)
    oh = jnp.concatenate([oh0, oh1], axis=0).astype(BF16)
    pre = jnp.dot(oh, upper, preferred_element_type=F32)
    pre0, pre1 = pre[:N_EXPERTS], pre[N_EXPERTS:]
    tot0, tot1 = pre0[:, tp - 1:tp], pre1[:, tp - 1:tp]
    base = offs_ref[...][:, 0:1] + carry_ref[...][:, 0:1]
    p0 = jnp.sum(oh0 * (pre0 - 1.0 + base), axis=0, keepdims=True)
    p1 = jnp.sum(oh1 * (pre1 - 1.0 + (base + tot0)), axis=0, keepdims=True)
    pos_ref[...] = jnp.concatenate([p0, p1], axis=0).astype(jnp.int32)
    carry_ref[...] += _spread(tot0 + tot1)


def _positions(eidx):
    t = eidx.shape[1]
    tp = POS_TILE
    blk = pl.BlockSpec((N_ASSIGN, tp), lambda i: (0, i))
    whole = pl.BlockSpec((N_EXPERTS, LANES), lambda i: (0, 0))
    stat = pltpu.VMEM((N_EXPERTS, LANES), F32)
    cnt = pl.pallas_call(
        _count_kernel,
        grid=(t // tp,),
        in_specs=[blk],
        out_specs=whole,
        out_shape=jax.ShapeDtypeStruct((N_EXPERTS, LANES), F32),
        compiler_params=_params("arbitrary"),
        name="moe_count",
    )(eidx)
    return pl.pallas_call(
        _positions_kernel,
        grid=(t // tp,),
        in_specs=[blk, whole],
        out_specs=[blk, pl.BlockSpec((8, LANES), lambda i: (0, 0))],
        out_shape=[jax.ShapeDtypeStruct((N_ASSIGN, t), jnp.int32), jax.ShapeDtypeStruct((8, LANES), jnp.int32)],
        scratch_shapes=[stat, stat],
        compiler_params=_params("arbitrary"),
        name="moe_positions",
    )(eidx, cnt)


def _row(ref, r):
    return ref.at[r >> 3, pl.ds(r & (SUBLANES - 1), 1), :]


def _dispatch_kernel(ends_ref, pad_ref, used_ref, pos0_ref, pos1_ref, h_ref, xs_hbm, zero_ref, sem):
    i = pl.program_id(0)
    groups = h_ref.shape[0]
    tile_groups = MOE_TILE // SUBLANES

    def clear_tile(start):
        first = pl.multiple_of(start // SUBLANES, tile_groups)
        cp = pltpu.make_async_copy(zero_ref, xs_hbm.at[pl.ds(first, tile_groups)], sem)
        cp.start()
        cp.wait()

    @pl.when(i == 0)
    def _():
        zero_ref[...] = jnp.zeros_like(zero_ref)
        for ex in range(N_EXPERTS):
            @pl.when(pad_ref[ex] > 0)
            def _():
                clear_tile(ends_ref[ex] - MOE_TILE)

        def clear_tail(tile, carry):
            clear_tile(tile * MOE_TILE)
            return carry

        lax.fori_loop(used_ref[0], xs_hbm.shape[0] // tile_groups, clear_tail, 0)

    def issue(g, carry):
        for u in range(SUBLANES):
            src = h_ref.at[g, pl.ds(u, 1), :]
            for pos_ref in (pos0_ref, pos1_ref):
                pltpu.make_async_copy(src, _row(xs_hbm, pos_ref[g * SUBLANES + u]), sem).start()
        return carry

    def drain(g, carry):
        for _ in range(SUBLANES * N_ASSIGN):
            pltpu.make_async_copy(h_ref.at[0, pl.ds(0, 1), :], _row(xs_hbm, 0), sem).wait()
        return carry

    lax.fori_loop(0, groups, issue, 0)
    lax.fori_loop(0, groups, drain, 0)


def _dispatch(ends, padded, n_used, pos, h2):
    t, d = h2.shape
    tp = DMA_TILE
    rows = N_ASSIGN * t + N_EXPERTS * MOE_TILE
    slot = pl.BlockSpec((tp,), lambda i, *_: (i,), memory_space=pltpu.SMEM)
    xs = pl.pallas_call(
        _dispatch_kernel,
        grid_spec=pltpu.PrefetchScalarGridSpec(
            num_scalar_prefetch=3,
            grid=(t // tp,),
            in_specs=[slot, slot, pl.BlockSpec((tp // SUBLANES, SUBLANES, d), lambda i, *_: (i, 0, 0))],
            out_specs=pl.BlockSpec(memory_space=pl.ANY),
            scratch_shapes=[pltpu.VMEM((MOE_TILE // SUBLANES, SUBLANES, d), F32), pltpu.SemaphoreType.DMA(())],
        ),
        out_shape=jax.ShapeDtypeStruct((rows // SUBLANES, SUBLANES, d), F32),
        compiler_params=_params("arbitrary"),
        name="moe_dispatch",
    )(ends, padded, n_used, pos[0], pos[1], h2.reshape(t // SUBLANES, SUBLANES, d))
    return xs.reshape(rows, d)


def _experts_kernel(te_ref, nv_ref, xs_ref, wg_ref, wu_ref, wd_ref, ys_ref, wgb_ref, wub_ref, wdb_ref):
    j = pl.program_id(0)
    used = nv_ref[0]
    jc = jnp.minimum(j, used - 1)
    new_expert = jnp.logical_or(j == 0, te_ref[jc] != te_ref[jnp.maximum(jc - 1, 0)])

    @pl.when(jnp.logical_and(j < used, new_expert))
    def _():
        wgb_ref[...] = wg_ref[...].astype(BF16)
        wub_ref[...] = wu_ref[...].astype(BF16)
        wdb_ref[...] = wd_ref[...].astype(BF16)

    @pl.when(j < used)
    def _():
        xb = xs_ref[...].astype(BF16)
        gt = jnp.dot(xb, wgb_ref[...], preferred_element_type=F32)
        up = jnp.dot(xb, wub_ref[...], preferred_element_type=F32)
        a = (gt * _sigmoid(gt) * up).astype(BF16)
        ys_ref[...] = jnp.dot(a, wdb_ref[...], preferred_element_type=F32)

    @pl.when(j >= used)
    def _():
        ys_ref[...] = jnp.zeros_like(ys_ref)


def _experts(tile_expert, n_used, xs, w_gate, w_up, w_down, layer):
    rows, d = xs.shape
    f = D_FF_EXPERT
    tile = lambda j, te, nv: (jnp.minimum(j, nv[0] - 1), 0)
    wsel = lambda j, te, nv: (layer, te[jnp.minimum(j, nv[0] - 1)], 0, 0)
    return pl.pallas_call(
        _experts_kernel,
        grid_spec=pltpu.PrefetchScalarGridSpec(
            num_scalar_prefetch=2,
            grid=(rows // MOE_TILE,),
            in_specs=[pl.BlockSpec((MOE_TILE, d), tile),
                      pl.BlockSpec((None, None, d, f), wsel), pl.BlockSpec((None, None, d, f), wsel),
                      pl.BlockSpec((None, None, f, d), wsel)],
            out_specs=pl.BlockSpec((MOE_TILE, d), lambda j, te, nv: (j, 0)),
            scratch_shapes=[pltpu.VMEM((d, f), BF16), pltpu.VMEM((d, f), BF16), pltpu.VMEM((f, d), BF16)],
        ),
        out_shape=jax.ShapeDtypeStruct((rows, d), F32),
        compiler_params=_params("arbitrary"),
        name="moe_experts",
    )(tile_expert, n_used, xs, w_gate, w_up, w_down)


def _combine_kernel(final, pos0_ref, pos1_ref, ys_hbm, wcol_ref, x_ref, mod_ref, fn_ref, o_ref, buf_ref, sem):
    tc, d = x_ref.shape
    groups = tc // SUBLANES

    def issue(g, carry):
        for u in range(SUBLANES):
            for k, pos_ref in enumerate((pos0_ref, pos1_ref)):
                pltpu.make_async_copy(_row(ys_hbm, pos_ref[g * SUBLANES + u]),
                                      buf_ref.at[k, g, pl.ds(u, 1), :], sem).start()
        return carry

    def drain(g, carry):
        for _ in range(SUBLANES * N_ASSIGN):
            pltpu.make_async_copy(_row(ys_hbm, 0), buf_ref.at[0, 0, pl.ds(0, 1), :], sem).wait()
        return carry

    lax.fori_loop(0, groups, issue, 0)
    lax.fori_loop(0, groups, drain, 0)
    w = wcol_ref[...]
    ffn = w[:, 0:1] * buf_ref[0].reshape(tc, d) + w[:, 1:2] * buf_ref[1].reshape(tc, d)
    y = x_ref[...] + mod_ref[...][5:6] * ffn
    if final:
        y = _rms(y) * fn_ref[...]
    o_ref[...] = y


def _combine(pos, ys, wcol, x1, mod_l, fnorm, seq, final):
    t, d = x1.shape
    tc = DMA_TILE
    per_batch = seq // tc
    row = lambda i: (i, 0)
    slot = pl.BlockSpec((tc,), lambda i: (i,), memory_space=pltpu.SMEM)
    return pl.pallas_call(
        functools.partial(_combine_kernel, final),
        grid=(t // tc,),
        in_specs=[slot, slot,
                  pl.BlockSpec(memory_space=pl.ANY),
                  pl.BlockSpec((tc, LANES), row), pl.BlockSpec((tc, d), row),
                  pl.BlockSpec((None, 6, d), lambda i: (i // per_batch, 0, 0)),
                  pl.BlockSpec((1, d), lambda i: (0, 0))],
        out_specs=pl.BlockSpec((tc, d), row),
        out_shape=jax.ShapeDtypeStruct((t, d), F32),
        scratch_shapes=[pltpu.VMEM((N_ASSIGN, tc // SUBLANES, SUBLANES, d), F32), pltpu.SemaphoreType.DMA(())],
        compiler_params=_params("arbitrary"),
        name="moe_combine",
    )(pos[0], pos[1], ys.reshape(ys.shape[0] // SUBLANES, SUBLANES, d), wcol, x1, mod_l, fnorm)


def _moe(h2, eidx, wcol, w_gate, w_up, w_down, layer, x1, mod_l, fnorm, seq, final):
    pos, meta = _positions(eidx)
    xs = _dispatch(meta[2], meta[3], meta[1], pos, h2)
    ys = _experts(meta[0], meta[1], xs, w_gate, w_up, w_down, layer)
    return _combine(pos, ys, wcol, x1, mod_l, fnorm, seq, final)


def _head_expand(offset):
    m = np.zeros((LANES, ATTN_WIDTH), np.float32)
    for h in range(N_HEADS):
        m[offset + h, h * HEAD_DIM:(h + 1) * HEAD_DIM] = 1.0
    return jnp.asarray(m, BF16)


def kernel(x, c, positions, w_in, w_out, attn_norm, hgrn_norm, lb_params, ada_w, ada_b,
           w_router, b_router, w_gate, w_up, w_down, final_norm):
    batch, seq, d = x.shape
    t = batch * seq
    c_pad = jnp.pad(c, ((0, 8 - batch), (0, 0)))
    mod = _ada_mod(c_pad, ada_w, ada_b)[:, :batch].reshape(DEPTH, batch, 6, d)
    lbs = _lower_bounds(lb_params)
    cos, sin = _rope_tables(positions)
    cos, sin = cos.reshape(t, LANES), sin.reshape(t, LANES)
    head_id = np.arange(MXU_WIDTH) // HEAD_DIM
    ones_bd = jnp.asarray(head_id[:, None] == head_id[None, :], BF16)
    em, el = _head_expand(0), _head_expand(N_HEADS)
    wr_t = w_router.T
    br = b_router.reshape(N_EXPERTS, 1)
    fnorm = final_norm.reshape(1, d)

    x2d = x.reshape(t, d)
    for l in range(DEPTH):
        outs = _inproj(x2d, mod[l], w_in[l].astype(BF16), cos, sin, lbs[l:l + 1], seq)
        qs, ks, vs = outs[0:3], outs[3:6], outs[6:9]
        qh, kh, gh, ih, og = outs[9:]
        us, sts = [], []
        for p, (window, dilation) in enumerate(DILATED_PATTERNS):
            u, st = _attn_pattern(qs[p], ks[p], vs[p], window, dilation)
            us.append(u)
            sts.append(st)
        o_h = _hgrn(qh, kh, gh, ih, ones_bd, batch, seq)
        x1, h2, eidx, wcol = _outproj(us, sts, o_h, og, x2d, mod[l], attn_norm[l:l + 1], hgrn_norm[l:l + 1],
                                      w_out[l].astype(BF16), em, el, wr_t, br, seq)
        x2d = _moe(h2, eidx, wcol, w_gate, w_up, w_down, l, x1, mod[l], fnorm, seq,
                   final=(l == DEPTH - 1))
    return x2d.reshape(batch, seq, d)
```

```python
import functools

import numpy as np
import jax
import jax.numpy as jnp
from jax import lax
from jax.experimental import pallas as pl
from jax.experimental.pallas import tpu as pltpu

D_MODEL = 1024
DEPTH = 2
ATTN_WIDTH = 512
HGRN_WIDTH = 512
HEAD_DIM = 64
N_HEADS = 8
DILATED_PATTERNS = ((128, 1), (512, 4), (2048, 16))
ATTN_BLOCK = 128
ATTN_STEP_BLOCKS = 4
ROPE_THETA = 10000.0
N_EXPERTS = 16
N_GROUPS = 4
EXPERTS_PER_GROUP = 4
D_FF_EXPERT = 512
RMS_EPS = 1e-6
IN_COLS = 3 * ATTN_WIDTH + 4 * HGRN_WIDTH

LANES = 128
SUBLANES = 8
MXU_WIDTH = 256
VMEM_LIMIT_BYTES = 56 * 1024 * 1024

TOKEN_TILE = 512
N_ASSIGN = 2
MOE_TILE = 512
POS_TILE = 1024
DMA_TILE = 512
HGRN_CHUNK = 64
HGRN_STEP_CHUNKS = 4
HGRN_SUB = 8
NEG_BIG = -1e30
LOG2E = 1.4426950408889634

F32 = jnp.float32
BF16 = jnp.bfloat16

_NT = (((1,), (1,)), ((), ()))
_TN = (((0,), (0,)), ((), ()))


def _params(*sem):
    return pltpu.CompilerParams(dimension_semantics=sem, vmem_limit_bytes=VMEM_LIMIT_BYTES)


def _sigmoid(x):
    return 1.0 / (1.0 + jnp.exp(-x))


def _rms(x):
    return x * lax.rsqrt(jnp.mean(x * x, axis=-1, keepdims=True) + RMS_EPS)


def _split3(a):
    a1 = a.astype(BF16)
    r1 = a - a1.astype(F32)
    a2 = r1.astype(BF16)
    a3 = (r1 - a2.astype(F32)).astype(BF16)
    return a1, a2, a3


def _dot_hi(a, b, dims):
    a1, a2, _ = _split3(a)
    b1, b2, _ = _split3(b)
    d = lambda p, q: lax.dot_general(p, q, dims, preferred_element_type=F32)
    return d(a1, b1) + (d(a2, b1) + d(a1, b2))


def _ada_kernel(c_ref, w_ref, b_ref, o_ref):
    c = c_ref[...]
    ca = c * _sigmoid(c)
    o_ref[...] = _dot_hi(ca, w_ref[...], (((1,), (0,)), ((), ()))) + b_ref[...]


def _ada_mod(c_pad, ada_w, ada_b):
    depth, d, n = ada_w.shape
    tn = 1536
    rows = c_pad.shape[0]
    return pl.pallas_call(
        _ada_kernel,
        grid=(depth, n // tn),
        in_specs=[
            pl.BlockSpec((rows, d), lambda l, j: (0, 0)),
            pl.BlockSpec((None, d, tn), lambda l, j: (l, 0, j)),
            pl.BlockSpec((None, 1, tn), lambda l, j: (l, 0, j)),
        ],
        out_specs=pl.BlockSpec((None, rows, tn), lambda l, j: (l, 0, j)),
        out_shape=jax.ShapeDtypeStruct((depth, rows, n), F32),
        compiler_params=_params("parallel", "parallel"),
        name="ada_mod",
    )(c_pad, ada_w, ada_b.reshape(depth, 1, n))


def _lb_kernel(p_ref, o_ref):
    p = p_ref[...]
    e = jnp.exp(p - jnp.max(p, axis=0, keepdims=True))
    sm = e / jnp.sum(e, axis=0, keepdims=True)
    run = jnp.zeros_like(sm[0:1])
    for l in range(p.shape[0]):
        run = run + sm[l:l + 1]
        o_ref[l:l + 1, :] = run - sm[0:1]


def _lower_bounds(lb_params):
    return pl.pallas_call(
        _lb_kernel,
        out_shape=jax.ShapeDtypeStruct(lb_params.shape, F32),
        name="lower_bounds",
    )(lb_params)


def _rope_kernel(pos_ref, inv_ref, cos_ref, sin_ref):
    ang = pos_ref[...].astype(F32) * inv_ref[...]
    lane = lax.broadcasted_iota(jnp.int32, ang.shape, 1)
    first = (lane % HEAD_DIM) < (HEAD_DIM // 2)
    s = jnp.sin(ang)
    cos_ref[...] = jnp.cos(ang)
    sin_ref[...] = jnp.where(first, -s, s)


def _rope_tables(positions):
    b, s = positions.shape
    ts = 1024
    half = HEAD_DIM // 2
    inv = ROPE_THETA ** (-jnp.arange(half, dtype=F32) / half)
    inv = jnp.tile(inv, LANES // half).reshape(1, LANES)
    out = jax.ShapeDtypeStruct((b, s, LANES), F32)
    return pl.pallas_call(
        _rope_kernel,
        grid=(b, s // ts),
        in_specs=[
            pl.BlockSpec((None, ts, 1), lambda i, j: (i, j, 0)),
            pl.BlockSpec((1, LANES), lambda i, j: (0, 0)),
        ],
        out_specs=[pl.BlockSpec((None, ts, LANES), lambda i, j: (i, j, 0))] * 2,
        out_shape=[out, out],
        compiler_params=_params("parallel", "parallel"),
        name="rope_tables",
    )(positions.reshape(b, s, 1), inv)


def _store_streams(val, refs):
    tm = val.shape[0]
    for (_, d), ref in zip(DILATED_PATTERNS, refs):
        if d == 1:
            ref[0] = val.astype(BF16)
        else:
            sw = jnp.swapaxes(val.reshape(tm // d, d, val.shape[1]), 0, 1)
            ref[...] = sw.astype(BF16)


def _inproj_kernel(x_ref, mod_ref, w_ref, cos_ref, sin_ref, lb_ref,
                   q1_ref, q4_ref, q16_ref, k1_ref, k4_ref, k16_ref, v1_ref, v4_ref, v16_ref,
                   qh_ref, kh_ref, gh_ref, ih_ref, og_ref):
    x = x_ref[...]
    mod = mod_ref[...]
    h = _rms(x) * (1.0 + mod[1:2]) + mod[0:1]
    hb = h.astype(BF16)
    reps = ATTN_WIDTH // LANES
    cos = jnp.concatenate([cos_ref[...]] * reps, axis=1)
    sin = jnp.concatenate([sin_ref[...]] * reps, axis=1)
    lane = lax.broadcasted_iota(jnp.int32, cos.shape, 1)
    first = (lane % HEAD_DIM) < (HEAD_DIM // 2)
    half = HEAD_DIM // 2

    def proj(j):
        return jnp.dot(hb, w_ref[:, j * 512:(j + 1) * 512], preferred_element_type=F32)

    def rot(t):
        swapped = jnp.where(first, pltpu.roll(t, ATTN_WIDTH - half, 1), pltpu.roll(t, half, 1))
        return t * cos + swapped * sin

    ps = [proj(j) for j in range(IN_COLS // 512)]
    _store_streams(rot(ps[0]) * (HEAD_DIM ** -0.5 * LOG2E), (q1_ref, q4_ref, q16_ref))
    _store_streams(rot(ps[1]), (k1_ref, k4_ref, k16_ref))
    _store_streams(ps[2], (v1_ref, v4_ref, v16_ref))
    qh_ref[...] = (ps[3] * _sigmoid(ps[3])).astype(BF16)
    lb = lb_ref[...]
    f = lb + (1.0 - lb) * _sigmoid(ps[4])
    kh_ref[...] = (1.0 - f).astype(BF16)
    gh_ref[...] = jnp.log(f)
    ih_ref[...] = ps[5].astype(BF16)
    og_ref[...] = _sigmoid(ps[6]).astype(BF16)


def _stream_spec(tm, dil, per_batch, width):
    return pl.BlockSpec((None, dil, tm // dil, width), lambda i: (i // per_batch, 0, i % per_batch, 0))


def _inproj(x2d, mod_l, w_in_l, cos, sin, lb_l, seq):
    t, d = x2d.shape
    tm = TOKEN_TILE
    per_batch = seq // tm
    batch = t // seq
    row = lambda i: (i, 0)
    half_spec = pl.BlockSpec((tm, 512), row)
    bf = jax.ShapeDtypeStruct((t, 512), BF16)
    stream_specs = [_stream_spec(tm, dil, per_batch, ATTN_WIDTH) for _, dil in DILATED_PATTERNS]
    stream_shapes = [jax.ShapeDtypeStruct((batch, dil, seq // dil, ATTN_WIDTH), BF16)
                     for _, dil in DILATED_PATTERNS]
    return pl.pallas_call(
        _inproj_kernel,
        grid=(t // tm,),
        in_specs=[
            pl.BlockSpec((tm, d), row),
            pl.BlockSpec((None, 6, d), lambda i: (i // per_batch, 0, 0)),
            pl.BlockSpec((d, IN_COLS), lambda i: (0, 0)),
            pl.BlockSpec((tm, LANES), row),
            pl.BlockSpec((tm, LANES), row),
            pl.BlockSpec((1, 512), lambda i: (0, 0)),
        ],
        out_specs=stream_specs * 3 + [half_spec] * 5,
        out_shape=stream_shapes * 3 + [bf, bf, jax.ShapeDtypeStruct((t, 512), F32), bf, bf],
        compiler_params=_params("parallel"),
        name="inproj",
    )(x2d, mod_l, w_in_l, cos, sin, lb_l)


def _attn_kernel(q_ref, kc_ref, kp_ref, vc_ref, vp_ref, bias_ref, u_ref, st_ref):
    blk = ATTN_BLOCK
    n = pl.program_id(2)
    keys = jnp.concatenate([kp_ref[...], kc_ref[...]], axis=0)
    vals = jnp.concatenate([vp_ref[...], vc_ref[...]], axis=0)
    biases = [bias_ref[jnp.minimum(n, 1)]] + [bias_ref[1]] * (ATTN_STEP_BLOCKS - 1)
    lane = lax.broadcasted_iota(jnp.int32, (blk, LANES), 1)
    low = lane < HEAD_DIM
    zero = jnp.zeros((), BF16)
    ones = jnp.ones((2 * blk, LANES), BF16)
    slabs = [slice(pair * LANES, (pair + 1) * LANES) for pair in range(N_HEADS // 2)]
    work = [(j, pair) for j in range(ATTN_STEP_BLOCKS) for pair in range(N_HEADS // 2)]

    scores = []
    for j, pair in work:
        qp = q_ref[j * blk:(j + 1) * blk, slabs[pair]]
        qst = jnp.concatenate([jnp.where(low, qp, zero), jnp.where(low, zero, qp)], axis=0)
        kwin = keys[j * blk:(j + 2) * blk, slabs[pair]]
        scores.append(lax.dot_general(qst, kwin, _NT, preferred_element_type=F32) + biases[j])
    maxes, probs = [], []
    for s in scores:
        m = jnp.max(s, axis=1, keepdims=True).astype(BF16).astype(F32)
        maxes.append(m)
        probs.append(jnp.exp2((s - m).astype(BF16)))
    stats = [jnp.zeros((blk, LANES), F32) for _ in range(ATTN_STEP_BLOCKS)]
    for (j, pair), m, p in zip(work, maxes, probs):
        vwin = vals[j * blk:(j + 2) * blk, slabs[pair]]
        ul = jnp.dot(p, jnp.concatenate([vwin, ones], axis=1), preferred_element_type=F32)
        u, l = ul[:, :LANES], ul[:, LANES:]
        for hh in range(2):
            head = 2 * pair + hh
            stats[j] = jnp.where(lane == head, m[hh * blk:(hh + 1) * blk], stats[j])
            stats[j] = jnp.where(lane == N_HEADS + head, l[hh * blk:(hh + 1) * blk], stats[j])
        u_ref[j * blk:(j + 1) * blk, slabs[pair]] = jnp.where(low, u[:blk], u[blk:]).astype(BF16)
    for j in range(ATTN_STEP_BLOCKS):
        st_ref[j * blk:(j + 1) * blk, :] = stats[j]


def _band_bias(steps):
    blk = ATTN_BLOCK
    qi = np.arange(2 * blk)[:, None] % blk
    kj = np.arange(2 * blk)[None, :]
    dist = qi + blk - kj
    band = (dist >= 0) & (dist <= steps)
    first = band & (kj >= blk)
    return jnp.asarray(np.where(np.stack([first, band]), 0.0, NEG_BIG), F32)


def _attn_pattern(qs, ks, vs, window, dilation):
    batch, _, m, _ = qs.shape
    rows = ATTN_STEP_BLOCKS * ATTN_BLOCK
    nb = m // rows
    bias = _band_bias(window // dilation)
    cur = pl.BlockSpec((None, None, rows, ATTN_WIDTH), lambda b, r, n: (b, r, n, 0))
    prev = pl.BlockSpec((None, None, ATTN_BLOCK, ATTN_WIDTH),
                        lambda b, r, n: (b, r, jnp.maximum(n * ATTN_STEP_BLOCKS - 1, 0), 0))
    return pl.pallas_call(
        _attn_kernel,
        grid=(batch, dilation, nb),
        in_specs=[cur, cur, prev, cur, prev,
                  pl.BlockSpec(bias.shape, lambda b, r, n: (0, 0, 0))],
        out_specs=[cur, pl.BlockSpec((None, None, rows, LANES), lambda b, r, n: (b, r, n, 0))],
        out_shape=[jax.ShapeDtypeStruct((batch, dilation, m, ATTN_WIDTH), BF16),
                   jax.ShapeDtypeStruct((batch, dilation, m, LANES), F32)],
        compiler_params=_params("parallel", "parallel", "arbitrary"),
        name=f"attn_d{dilation}",
    )(qs, ks, ks, vs, vs, bias)


def _hgrn_kernel(q_ref, k_ref, g_ref, v_ref, ones_ref, o_ref, st_ref):
    @pl.when(pl.program_id(1) == 0)
    def _():
        st_ref[...] = jnp.zeros_like(st_ref)

    state = st_ref[...]
    for chunk in range(HGRN_STEP_CHUNKS):
        state = _hgrn_chunk(chunk * HGRN_CHUNK, q_ref, k_ref, g_ref, v_ref, ones_ref, o_ref, state)
    st_ref[...] = state


def _hgrn_chunk(off, q_ref, k_ref, g_ref, v_ref, ones_ref, o_ref, state):
    c, sub, w = HGRN_CHUNK, HGRN_SUB, HGRN_WIDTH
    span = slice(off, off + c)
    q = q_ref[span, :].astype(F32)
    k = k_ref[span, :].astype(F32)
    vb = v_ref[span, :]
    v = vb.astype(F32)
    g = g_ref[span, :]

    ri = lax.broadcasted_iota(jnp.int32, (c, c), 0)
    ci = lax.broadcasted_iota(jnp.int32, (c, c), 1)
    tri = (ci <= ri).astype(BF16)
    g1, g2, g3 = _split3(g)
    cs = lambda t: jnp.dot(tri, t, preferred_element_type=F32)
    b = (cs(g1) + (cs(g2) + cs(g3))) * LOG2E
    b_last = b[c - 1:c, :]

    o_inter = lax.dot_general((q * jnp.exp2(b)).astype(BF16), state.astype(BF16), _NT,
                              preferred_element_type=F32)
    kt = (k * jnp.exp2(b_last - b)).astype(BF16)
    upd = lax.dot_general(vb, kt, _TN, preferred_element_type=F32)
    r0i = lax.broadcasted_iota(jnp.int32, (w, w), 0) // HEAD_DIM
    c0i = lax.broadcasted_iota(jnp.int32, (w, w), 1) // HEAD_DIM
    new_state = state * jnp.exp2(b_last) + jnp.where(r0i == c0i, upd, 0.0)

    hrow = lax.broadcasted_iota(jnp.int32, (N_HEADS * sub, w), 0) // sub
    hlane = lax.broadcasted_iota(jnp.int32, (N_HEADS * sub, w), 1) // HEAD_DIM
    hmask = hrow == hlane
    trow = lax.broadcasted_iota(jnp.int32, (sub, w), 0)
    ones = ones_ref[...]
    ow = ones.shape[0]

    nblk = c // sub
    ws = []
    for blk in range(nblk):
        rows = slice(blk * sub, (blk + 1) * sub)
        bi, qi, ki = b[rows], q[rows], k[rows]
        for s in range(sub):
            e = jnp.exp2(bi - bi[s:s + 1])
            ws.append(jnp.where(trow >= s, qi * (ki[s:s + 1] * e), 0.0))
    wcat = jnp.concatenate(ws, axis=0).astype(BF16)
    sc = jnp.concatenate(
        [jnp.dot(wcat[:, j * ow:(j + 1) * ow], ones, preferred_element_type=F32) for j in range(w // ow)],
        axis=1)

    inter_scores = [None]
    for blk in range(1, nblk):
        r0 = blk * sub
        rows = slice(r0, r0 + sub)
        bref = b[r0:r0 + 1]
        qs = q[rows] * jnp.exp2(b[rows] - bref)
        kp = (k[0:r0] * jnp.exp2(bref - b[0:r0])).astype(BF16)
        qexp = jnp.where(hmask, jnp.concatenate([qs] * N_HEADS, axis=0), 0.0).astype(BF16)
        inter_scores.append(lax.dot_general(qexp, kp, _NT, preferred_element_type=F32))

    for blk in range(nblk):
        r0 = blk * sub
        rows = slice(r0, r0 + sub)
        acc = o_inter[rows]
        if blk > 0:
            oexp = jnp.dot(inter_scores[blk].astype(BF16), vb[0:r0], preferred_element_type=F32)
            oexp = jnp.where(hmask, oexp, 0.0)
            for hd in range(N_HEADS):
                acc = acc + oexp[hd * sub:(hd + 1) * sub]
        vi = v[rows]
        for s in range(sub):
            at = (blk * sub + s) * sub
            acc = acc + sc[at:at + sub] * vi[s:s + 1]
        o_ref[off + r0:off + r0 + sub, :] = acc
    return new_state


def _hgrn(qh, kh, gh, ih, ones_bd, batch, seq):
    c, w = HGRN_CHUNK, HGRN_WIDTH
    view = lambda a: a.reshape(batch, seq, w)
    rows = HGRN_STEP_CHUNKS * c
    blk = pl.BlockSpec((None, rows, w), lambda b, n: (b, n, 0))
    o = pl.pallas_call(
        _hgrn_kernel,
        grid=(batch, seq // rows),
        in_specs=[blk, blk, blk, blk, pl.BlockSpec(ones_bd.shape, lambda b, n: (0, 0))],
        out_specs=blk,
        out_shape=jax.ShapeDtypeStruct((batch, seq, w), F32),
        scratch_shapes=[pltpu.VMEM((w, w), F32)],
        compiler_params=_params("parallel", "arbitrary"),
        name="hgrn2",
    )(view(qh), view(kh), view(gh), view(ih), ones_bd)
    return o.reshape(batch * seq, w)


def _route(probs):
    rows = [probs[i:i + 1, :] for i in range(N_EXPERTS)]
    gsum = []
    for gidx in range(N_GROUPS):
        a, b_, c_, d_ = rows[4 * gidx:4 * gidx + 4]
        hi1, lo1 = jnp.maximum(a, b_), jnp.minimum(a, b_)
        hi2, lo2 = jnp.maximum(c_, d_), jnp.minimum(c_, d_)
        top1 = jnp.maximum(hi1, hi2)
        second = jnp.maximum(jnp.minimum(hi1, hi2), jnp.maximum(lo1, lo2))
        gsum.append(top1 + second)
    best, gi = gsum[0], jnp.zeros(gsum[0].shape, jnp.int32)
    for gidx in range(1, N_GROUPS):
        upd = gsum[gidx] > best
        best = jnp.where(upd, gsum[gidx], best)
        gi = jnp.where(upd, gidx, gi)
    vals = []
    for j in range(EXPERTS_PER_GROUP):
        vj = rows[j]
        for gidx in range(1, N_GROUPS):
            vj = jnp.where(gi == gidx, rows[4 * gidx + j], vj)
        vals.append(vj)
    v1, i1 = vals[0], jnp.zeros(gi.shape, jnp.int32)
    for j in range(1, EXPERTS_PER_GROUP):
        upd = vals[j] > v1
        v1 = jnp.where(upd, vals[j], v1)
        i1 = jnp.where(upd, j, i1)
    v2, i2 = jnp.full(v1.shape, -1.0, F32), jnp.zeros(gi.shape, jnp.int32)
    for j in range(EXPERTS_PER_GROUP):
        upd = (i1 != j) & (vals[j] > v2)
        v2 = jnp.where(upd, vals[j], v2)
        i2 = jnp.where(upd, j, i2)
    tot = v1 + v2
    base = gi * EXPERTS_PER_GROUP
    return base + i1, base + i2, v1 / tot, v2 / tot


def _outproj_kernel(u1_ref, u2_ref, u3_ref, s1_ref, s2_ref, s3_ref, o_ref, og_ref, x_ref, mod_ref,
                    an_ref, hn_ref, w_ref, em_ref, el_ref, wr_ref, br_ref,
                    x1_ref, h2_ref, eidx_ref, wcol_ref, uscr_ref, sscr_ref):
    us = (u1_ref, u2_ref, u3_ref)
    ss = (s1_ref, s2_ref, s3_ref)
    em, el = em_ref[...], el_ref[...]

    def natural(ref):
        dil, per, width = ref.shape
        if dil == 1:
            return ref[0].astype(F32)
        scr = uscr_ref if width == ATTN_WIDTH else sscr_ref
        slabs = scr.shape[0]
        for r in range(dil):
            val = ref[r].astype(F32)
            for c in range(slabs):
                scr[c, pl.ds(r, per, stride=dil), :] = val[:, c * LANES:(c + 1) * LANES]
        return jnp.concatenate([scr[c] for c in range(slabs)], axis=1)

    ms, ls = [], []
    for s_ref in ss:
        m1, m2, _ = _split3(natural(s_ref))
        ms.append(jnp.dot(m1, em, preferred_element_type=F32))
        ls.append(jnp.dot(m1, el, preferred_element_type=F32) + jnp.dot(m2, el, preferred_element_type=F32))
    mmax = jnp.maximum(jnp.maximum(ms[0], ms[1]), ms[2])
    num = jnp.zeros_like(mmax)
    den = jnp.zeros_like(mmax)
    for p in range(3):
        wgt = jnp.exp2(ms[p] - mmax)
        num = num + wgt * natural(us[p])
        den = den + wgt * ls[p]
    attn = num / den
    mod = mod_ref[...]
    a_n = _rms(attn) * an_ref[...]
    rec = _rms(o_ref[...]) * hn_ref[...] * og_ref[...].astype(F32)
    merged = jnp.concatenate([a_n, rec], axis=1).astype(BF16)
    mix = jnp.dot(merged, w_ref[...], preferred_element_type=F32)
    x1 = x_ref[...] + mod[2:3] * mix
    x1_ref[...] = x1
    h2 = _rms(x1) * (1.0 + mod[4:5]) + mod[3:4]
    h2_ref[...] = h2

    logits = _dot_hi(wr_ref[...], h2, _NT) + br_ref[...]
    e = jnp.exp(logits - jnp.max(logits, axis=0, keepdims=True))
    probs = e / jnp.sum(e, axis=0, keepdims=True)
    e1, e2, w1, w2 = _route(probs)
    eidx_ref[...] = jnp.concatenate([e1, e2], axis=0)
    tm = probs.shape[1]
    srow = lax.broadcasted_iota(jnp.int32, (LANES, tm), 0)
    w_t = jnp.where(srow == 0, w1, jnp.where(srow == 1, w2, 0.0))
    wcol_ref[...] = w_t.T


def _outproj(us, sts, o_h, og, x2d, mod_l, an_l, hn_l, w_out_l, em, el, wr_t, br, seq):
    t, d = x2d.shape
    tm = TOKEN_TILE
    per_batch = seq // tm
    row = lambda i: (i, 0)
    const = lambda i: (0, 0)
    half = pl.BlockSpec((tm, 512), row)
    stat = pl.BlockSpec((tm, LANES), row)
    full = pl.BlockSpec((tm, d), row)
    u_specs = [_stream_spec(tm, dil, per_batch, ATTN_WIDTH) for _, dil in DILATED_PATTERNS]
    s_specs = [_stream_spec(tm, dil, per_batch, LANES) for _, dil in DILATED_PATTERNS]
    return pl.pallas_call(
        _outproj_kernel,
        grid=(t // tm,),
        in_specs=u_specs + s_specs + [half, half, full,
                  pl.BlockSpec((None, 6, d), lambda i: (i // per_batch, 0, 0)),
                  pl.BlockSpec((1, 512), const), pl.BlockSpec((1, 512), const),
                  pl.BlockSpec((d, d), const),
                  pl.BlockSpec((LANES, 512), const), pl.BlockSpec((LANES, 512), const),
                  pl.BlockSpec((N_EXPERTS, d), const), pl.BlockSpec((N_EXPERTS, 1), const)],
        out_specs=[full, full, pl.BlockSpec((N_ASSIGN, tm), lambda i: (0, i)), stat],
        out_shape=[jax.ShapeDtypeStruct((t, d), F32), jax.ShapeDtypeStruct((t, d), F32),
                   jax.ShapeDtypeStruct((N_ASSIGN, t), jnp.int32), jax.ShapeDtypeStruct((t, LANES), F32)],
        scratch_shapes=[pltpu.VMEM((ATTN_WIDTH // LANES, tm, LANES), F32), pltpu.VMEM((1, tm, LANES), F32)],
        compiler_params=_params("parallel"),
        name="outproj_route",
    )(*us, *sts, o_h, og, x2d, mod_l, an_l, hn_l, w_out_l, em, el, wr_t, br)


def _one_hots(e_ref):
    e = e_ref[...]
    erow = lax.broadcasted_iota(jnp.int32, (N_EXPERTS, e.shape[1]), 0)
    return (erow == e[0:1]).astype(F32), (erow == e[1:2]).astype(F32)


def _spread(col):
    return jnp.broadcast_to(col, (N_EXPERTS, LANES))


def _count_kernel(e_ref, cnt_ref):
    @pl.when(pl.program_id(0) == 0)
    def _():
        cnt_ref[...] = jnp.zeros_like(cnt_ref)

    oh0, oh1 = _one_hots(e_ref)
    cnt_ref[...] += _spread(jnp.sum(oh0 + oh1, axis=1, keepdims=True))


def _positions_kernel(e_ref, cnt_ref, pos_ref, meta_ref, carry_ref, offs_ref):
    i = pl.program_id(0)
    tp = e_ref.shape[1]
    oh0, oh1 = _one_hots(e_ref)

    @pl.when(i == 0)
    def _():
        cnt = cnt_ref[...]
        padded = jnp.floor((cnt + (MOE_TILE - 1)) * (1.0 / MOE_TILE)) * MOE_TILE
        run = jnp.zeros((1, LANES), F32)
        starts = []
        for ex in range(N_EXPERTS):
            starts.append(run)
            run = run + padded[ex:ex + 1]
        offs = jnp.concatenate(starts, axis=0)
        offs_ref[...] = offs
        carry_ref[...] = jnp.zeros_like(carry_ref)
        ends = offs + padded
        lane = lax.broadcasted_iota(jnp.int32, (N_EXPERTS, LANES), 1)
        srow = lax.broadcasted_iota(jnp.int32, (N_EXPERTS, LANES), 0)
        tile_start = (lane * MOE_TILE).astype(F32)
        tile_expert = jnp.sum((ends <= tile_start).astype(F32), axis=0, keepdims=True)
        tile_expert = jnp.minimum(tile_expert, N_EXPERTS - 1.0)
        on_diag = srow == lane
        ends_lane = jnp.sum(jnp.where(on_diag, ends, 0.0), axis=0, keepdims=True)
        pad_lane = jnp.sum(jnp.where(on_diag, padded, 0.0), axis=0, keepdims=True)
        meta = jnp.concatenate([tile_expert, run * (1.0 / MOE_TILE), ends_lane, pad_lane,
                                jnp.zeros((4, LANES), F32)], axis=0)
        meta_ref[...] = meta.astype(jnp.int32)

    r = lax.broadcasted_iota(jnp.int32, (tp, tp), 0)
    c = lax.broadcasted_iota(jnp.int32, (tp, tp), 1)
    upper = (r <= c).astype(BF16)
    oh = jnp.concatenate([oh0, oh1], axis=0).astype(BF16)
    pre = jnp.dot(oh, upper, preferred_element_type=F32)
    pre0, pre1 = pre[:N_EXPERTS], pre[N_EXPERTS:]
    tot0, tot1 = pre0[:, tp - 1:tp], pre1[:, tp - 1:tp]
    base = offs_ref[...][:, 0:1] + carry_ref[...][:, 0:1]
    p0 = jnp.sum(oh0 * (pre0 - 1.0 + base), axis=0, keepdims=True)
    p1 = jnp.sum(oh1 * (pre1 - 1.0 + (base + tot0)), axis=0, keepdims=True)
    pos_ref[...] = jnp.concatenate([p0, p1], axis=0).astype(jnp.int32)
    carry_ref[...] += _spread(tot0 + tot1)


def _positions(eidx):
    t = eidx.shape[1]
    tp = POS_TILE
    blk = pl.BlockSpec((N_ASSIGN, tp), lambda i: (0, i))
    whole = pl.BlockSpec((N_EXPERTS, LANES), lambda i: (0, 0))
    stat = pltpu.VMEM((N_EXPERTS, LANES), F32)
    cnt = pl.pallas_call(
        _count_kernel,
        grid=(t // tp,),
        in_specs=[blk],
        out_specs=whole,
        out_shape=jax.ShapeDtypeStruct((N_EXPERTS, LANES), F32),
        compiler_params=_params("arbitrary"),
        name="moe_count",
    )(eidx)
    return pl.pallas_call(
        _positions_kernel,
        grid=(t // tp,),
        in_specs=[blk, whole],
        out_specs=[blk, pl.BlockSpec((8, LANES), lambda i: (0, 0))],
        out_shape=[jax.ShapeDtypeStruct((N_ASSIGN, t), jnp.int32), jax.ShapeDtypeStruct((8, LANES), jnp.int32)],
        scratch_shapes=[stat, stat],
        compiler_params=_params("arbitrary"),
        name="moe_positions",
    )(eidx, cnt)


def _row(ref, r):
    return ref.at[r >> 3, pl.ds(r & (SUBLANES - 1), 1), :]


def _dispatch_kernel(ends_ref, pad_ref, used_ref, pos0_ref, pos1_ref, h_ref, xs_hbm, zero_ref, sem):
    i = pl.program_id(0)
    groups = h_ref.shape[0]
    tile_groups = MOE_TILE // SUBLANES

    def clear_tile(start):
        first = pl.multiple_of(start // SUBLANES, tile_groups)
        cp = pltpu.make_async_copy(zero_ref, xs_hbm.at[pl.ds(first, tile_groups)], sem)
        cp.start()
        cp.wait()

    @pl.when(i == 0)
    def _():
        zero_ref[...] = jnp.zeros_like(zero_ref)
        for ex in range(N_EXPERTS):
            @pl.when(pad_ref[ex] > 0)
            def _():
                clear_tile(ends_ref[ex] - MOE_TILE)

        def clear_tail(tile, carry):
            clear_tile(tile * MOE_TILE)
            return carry

        lax.fori_loop(used_ref[0], xs_hbm.shape[0] // tile_groups, clear_tail, 0)

    def issue(g, carry):
        for u in range(SUBLANES):
            src = h_ref.at[g, pl.ds(u, 1), :]
            for pos_ref in (pos0_ref, pos1_ref):
                pltpu.make_async_copy(src, _row(xs_hbm, pos_ref[g * SUBLANES + u]), sem).start()
        return carry

    def drain(g, carry):
        for _ in range(SUBLANES * N_ASSIGN):
            pltpu.make_async_copy(h_ref.at[0, pl.ds(0, 1), :], _row(xs_hbm, 0), sem).wait()
        return carry

    lax.fori_loop(0, groups, issue, 0)
    lax.fori_loop(0, groups, drain, 0)


def _dispatch(ends, padded, n_used, pos, h2):
    t, d = h2.shape
    tp = DMA_TILE
    rows = N_ASSIGN * t + N_EXPERTS * MOE_TILE
    slot = pl.BlockSpec((tp,), lambda i, *_: (i,), memory_space=pltpu.SMEM)
    xs = pl.pallas_call(
        _dispatch_kernel,
        grid_spec=pltpu.PrefetchScalarGridSpec(
            num_scalar_prefetch=3,
            grid=(t // tp,),
            in_specs=[slot, slot, pl.BlockSpec((tp // SUBLANES, SUBLANES, d), lambda i, *_: (i, 0, 0))],
            out_specs=pl.BlockSpec(memory_space=pl.ANY),
            scratch_shapes=[pltpu.VMEM((MOE_TILE // SUBLANES, SUBLANES, d), F32), pltpu.SemaphoreType.DMA(())],
        ),
        out_shape=jax.ShapeDtypeStruct((rows // SUBLANES, SUBLANES, d), F32),
        compiler_params=_params("arbitrary"),
        name="moe_dispatch",
    )(ends, padded, n_used, pos[0], pos[1], h2.reshape(t // SUBLANES, SUBLANES, d))
    return xs.reshape(rows, d)


def _experts_kernel(te_ref, nv_ref, xs_ref, wg_ref, wu_ref, wd_ref, ys_ref, wgb_ref, wub_ref, wdb_ref):
    j = pl.program_id(0)
    used = nv_ref[0]
    jc = jnp.minimum(j, used - 1)
    new_expert = jnp.logical_or(j == 0, te_ref[jc] != te_ref[jnp.maximum(jc - 1, 0)])

    @pl.when(jnp.logical_and(j < used, new_expert))
    def _():
        wgb_ref[...] = wg_ref[...].astype(BF16)
        wub_ref[...] = wu_ref[...].astype(BF16)
        wdb_ref[...] = wd_ref[...].astype(BF16)

    @pl.when(j < used)
    def _():
        xb = xs_ref[...].astype(BF16)
        gt = jnp.dot(xb, wgb_ref[...], preferred_element_type=F32)
        up = jnp.dot(xb, wub_ref[...], preferred_element_type=F32)
        a = (gt * _sigmoid(gt) * up).astype(BF16)
        ys_ref[...] = jnp.dot(a, wdb_ref[...], preferred_element_type=F32)

    @pl.when(j >= used)
    def _():
        ys_ref[...] = jnp.zeros_like(ys_ref)


def _experts(tile_expert, n_used, xs, w_gate, w_up, w_down, layer):
    rows, d = xs.shape
    f = D_FF_EXPERT
    tile = lambda j, te, nv: (jnp.minimum(j, nv[0] - 1), 0)
    wsel = lambda j, te, nv: (layer, te[jnp.minimum(j, nv[0] - 1)], 0, 0)
    return pl.pallas_call(
        _experts_kernel,
        grid_spec=pltpu.PrefetchScalarGridSpec(
            num_scalar_prefetch=2,
            grid=(rows // MOE_TILE,),
            in_specs=[pl.BlockSpec((MOE_TILE, d), tile),
                      pl.BlockSpec((None, None, d, f), wsel), pl.BlockSpec((None, None, d, f), wsel),
                      pl.BlockSpec((None, None, f, d), wsel)],
            out_specs=pl.BlockSpec((MOE_TILE, d), lambda j, te, nv: (j, 0)),
            scratch_shapes=[pltpu.VMEM((d, f), BF16), pltpu.VMEM((d, f), BF16), pltpu.VMEM((f, d), BF16)],
        ),
        out_shape=jax.ShapeDtypeStruct((rows, d), F32),
        compiler_params=_params("arbitrary"),
        name="moe_experts",
    )(tile_expert, n_used, xs, w_gate, w_up, w_down)


def _combine_kernel(final, pos0_ref, pos1_ref, ys_hbm, wcol_ref, x_ref, mod_ref, fn_ref, o_ref, buf_ref, sem):
    tc, d = x_ref.shape
    groups = tc // SUBLANES

    def issue(g, carry):
        for u in range(SUBLANES):
            for k, pos_ref in enumerate((pos0_ref, pos1_ref)):
                pltpu.make_async_copy(_row(ys_hbm, pos_ref[g * SUBLANES + u]),
                                      buf_ref.at[k, g, pl.ds(u, 1), :], sem).start()
        return carry

    def drain(g, carry):
        for _ in range(SUBLANES * N_ASSIGN):
            pltpu.make_async_copy(_row(ys_hbm, 0), buf_ref.at[0, 0, pl.ds(0, 1), :], sem).wait()
        return carry

    lax.fori_loop(0, groups, issue, 0)
    lax.fori_loop(0, groups, drain, 0)
    w = wcol_ref[...]
    ffn = w[:, 0:1] * buf_ref[0].reshape(tc, d) + w[:, 1:2] * buf_ref[1].reshape(tc, d)
    y = x_ref[...] + mod_ref[...][5:6] * ffn
    if final:
        y = _rms(y) * fn_ref[...]
    o_ref[...] = y


def _combine(pos, ys, wcol, x1, mod_l, fnorm, seq, final):
    t, d = x1.shape
    tc = DMA_TILE
    per_batch = seq // tc
    row = lambda i: (i, 0)
    slot = pl.BlockSpec((tc,), lambda i: (i,), memory_space=pltpu.SMEM)
    return pl.pallas_call(
        functools.partial(_combine_kernel, final),
        grid=(t // tc,),
        in_specs=[slot, slot,
                  pl.BlockSpec(memory_space=pl.ANY),
                  pl.BlockSpec((tc, LANES), row), pl.BlockSpec((tc, d), row),
                  pl.BlockSpec((None, 6, d), lambda i: (i // per_batch, 0, 0)),
                  pl.BlockSpec((1, d), lambda i: (0, 0))],
        out_specs=pl.BlockSpec((tc, d), row),
        out_shape=jax.ShapeDtypeStruct((t, d), F32),
        scratch_shapes=[pltpu.VMEM((N_ASSIGN, tc // SUBLANES, SUBLANES, d), F32), pltpu.SemaphoreType.DMA(())],
        compiler_params=_params("arbitrary"),
        name="moe_combine",
    )(pos[0], pos[1], ys.reshape(ys.shape[0] // SUBLANES, SUBLANES, d), wcol, x1, mod_l, fnorm)


def _moe(h2, eidx, wcol, w_gate, w_up, w_down, layer, x1, mod_l, fnorm, seq, final):
    pos, meta = _positions(eidx)
    xs = _dispatch(meta[2], meta[3], meta[1], pos, h2)
    ys = _experts(meta[0], meta[1], xs, w_gate, w_up, w_down, layer)
    return _combine(pos, ys, wcol, x1, mod_l, fnorm, seq, final)


def _head_expand(offset):
    m = np.zeros((LANES, ATTN_WIDTH), np.float32)
    for h in range(N_HEADS):
        m[offset + h, h * HEAD_DIM:(h + 1) * HEAD_DIM] = 1.0
    return jnp.asarray(m, BF16)


def kernel(x, c, positions, w_in, w_out, attn_norm, hgrn_norm, lb_params, ada_w, ada_b,
           w_router, b_router, w_gate, w_up, w_down, final_norm):
    batch, seq, d = x.shape
    t = batch * seq
    c_pad = jnp.pad(c, ((0, 8 - batch), (0, 0)))
    mod = _ada_mod(c_pad, ada_w, ada_b)[:, :batch].reshape(DEPTH, batch, 6, d)
    lbs = _lower_bounds(lb_params)
    cos, sin = _rope_tables(positions)
    cos, sin = cos.reshape(t, LANES), sin.reshape(t, LANES)
    head_id = np.arange(MXU_WIDTH) // HEAD_DIM
    ones_bd = jnp.asarray(head_id[:, None] == head_id[None, :], BF16)
    em, el = _head_expand(0), _head_expand(N_HEADS)
    wr_t = w_router.T
    br = b_router.reshape(N_EXPERTS, 1)
    fnorm = final_norm.reshape(1, d)

    x2d = x.reshape(t, d)
    for l in range(DEPTH):
        outs = _inproj(x2d, mod[l], w_in[l].astype(BF16), cos, sin, lbs[l:l + 1], seq)
        qs, ks, vs = outs[0:3], outs[3:6], outs[6:9]
        qh, kh, gh, ih, og = outs[9:]
        us, sts = [], []
        for p, (window, dilation) in enumerate(DILATED_PATTERNS):
            u, st = _attn_pattern(qs[p], ks[p], vs[p], window, dilation)
            us.append(u)
            sts.append(st)
        o_h = _hgrn(qh, kh, gh, ih, ones_bd, batch, seq)
        x1, h2, eidx, wcol = _outproj(us, sts, o_h, og, x2d, mod[l], attn_norm[l:l + 1], hgrn_norm[l:l + 1],
                                      w_out[l].astype(BF16), em, el, wr_t, br, seq)
        x2d = _moe(h2, eidx, wcol, w_gate, w_up, w_down, l, x1, mod[l], fnorm, seq,
                   final=(l == DEPTH - 1))
    return x2d.reshape(batch, seq, d)
```

```python
import functools

import numpy as np
import jax
import jax.numpy as jnp
from jax import lax
from jax.experimental import pallas as pl
from jax.experimental.pallas import tpu as pltpu

D_MODEL = 1024
DEPTH = 2
ATTN_WIDTH = 512
HGRN_WIDTH = 512
HEAD_DIM = 64
N_HEADS = 8
DILATED_PATTERNS = ((128, 1), (512, 4), (2048, 16))
ATTN_BLOCK = 128
ATTN_STEP_BLOCKS = 4
ROPE_THETA = 10000.0
N_EXPERTS = 16
N_GROUPS = 4
EXPERTS_PER_GROUP = 4
D_FF_EXPERT = 512
RMS_EPS = 1e-6
IN_COLS = 3 * ATTN_WIDTH + 4 * HGRN_WIDTH

LANES = 128
SUBLANES = 8
MXU_WIDTH = 256
VMEM_LIMIT_BYTES = 56 * 1024 * 1024

TOKEN_TILE = 512
N_ASSIGN = 2
MOE_TILE = 512
POS_TILE = 1024
DMA_TILE = 512
HGRN_CHUNK = 64
HGRN_STEP_CHUNKS = 4
HGRN_SUB = 8
NEG_BIG = -1e30
LOG2E = 1.4426950408889634

F32 = jnp.float32
BF16 = jnp.bfloat16

_NT = (((1,), (1,)), ((), ()))
_TN = (((0,), (0,)), ((), ()))


def _params(*sem):
    return pltpu.CompilerParams(dimension_semantics=sem, vmem_limit_bytes=VMEM_LIMIT_BYTES)


def _sigmoid(x):
    return 1.0 / (1.0 + jnp.exp(-x))


def _rms(x):
    return x * lax.rsqrt(jnp.mean(x * x, axis=-1, keepdims=True) + RMS_EPS)


def _split3(a):
    a1 = a.astype(BF16)
    r1 = a - a1.astype(F32)
    a2 = r1.astype(BF16)
    a3 = (r1 - a2.astype(F32)).astype(BF16)
    return a1, a2, a3


def _dot_hi(a, b, dims):
    a1, a2, _ = _split3(a)
    b1, b2, _ = _split3(b)
    d = lambda p, q: lax.dot_general(p, q, dims, preferred_element_type=F32)
    return d(a1, b1) + (d(a2, b1) + d(a1, b2))


def _ada_kernel(c_ref, w_ref, b_ref, o_ref):
    c = c_ref[...]
    ca = c * _sigmoid(c)
    o_ref[...] = _dot_hi(ca, w_ref[...], (((1,), (0,)), ((), ()))) + b_ref[...]


def _ada_mod(c_pad, ada_w, ada_b):
    depth, d, n = ada_w.shape
    tn = 1536
    rows = c_pad.shape[0]
    return pl.pallas_call(
        _ada_kernel,
        grid=(depth, n // tn),
        in_specs=[
            pl.BlockSpec((rows, d), lambda l, j: (0, 0)),
            pl.BlockSpec((None, d, tn), lambda l, j: (l, 0, j)),
            pl.BlockSpec((None, 1, tn), lambda l, j: (l, 0, j)),
        ],
        out_specs=pl.BlockSpec((None, rows, tn), lambda l, j: (l, 0, j)),
        out_shape=jax.ShapeDtypeStruct((depth, rows, n), F32),
        compiler_params=_params("parallel", "parallel"),
        name="ada_mod",
    )(c_pad, ada_w, ada_b.reshape(depth, 1, n))


def _lb_kernel(p_ref, o_ref):
    p = p_ref[...]
    e = jnp.exp(p - jnp.max(p, axis=0, keepdims=True))
    sm = e / jnp.sum(e, axis=0, keepdims=True)
    run = jnp.zeros_like(sm[0:1])
    for l in range(p.shape[0]):
        run = run + sm[l:l + 1]
        o_ref[l:l + 1, :] = run - sm[0:1]


def _lower_bounds(lb_params):
    return pl.pallas_call(
        _lb_kernel,
        out_shape=jax.ShapeDtypeStruct(lb_params.shape, F32),
        name="lower_bounds",
    )(lb_params)


def _rope_kernel(pos_ref, inv_ref, cos_ref, sin_ref):
    ang = pos_ref[...].astype(F32) * inv_ref[...]
    lane = lax.broadcasted_iota(jnp.int32, ang.shape, 1)
    first = (lane % HEAD_DIM) < (HEAD_DIM // 2)
    s = jnp.sin(ang)
    cos_ref[...] = jnp.cos(ang)
    sin_ref[...] = jnp.where(first, -s, s)


def _rope_tables(positions):
    b, s = positions.shape
    ts = 1024
    half = HEAD_DIM // 2
    inv = ROPE_THETA ** (-jnp.arange(half, dtype=F32) / half)
    inv = jnp.tile(inv, LANES // half).reshape(1, LANES)
    out = jax.ShapeDtypeStruct((b, s, LANES), F32)
    return pl.pallas_call(
        _rope_kernel,
        grid=(b, s // ts),
        in_specs=[
            pl.BlockSpec((None, ts, 1), lambda i, j: (i, j, 0)),
            pl.BlockSpec((1, LANES), lambda i, j: (0, 0)),
        ],
        out_specs=[pl.BlockSpec((None, ts, LANES), lambda i, j: (i, j, 0))] * 2,
        out_shape=[out, out],
        compiler_params=_params("parallel", "parallel"),
        name="rope_tables",
    )(positions.reshape(b, s, 1), inv)


def _store_streams(val, refs):
    tm = val.shape[0]
    for (_, d), ref in zip(DILATED_PATTERNS, refs):
        if d == 1:
            ref[0] = val.astype(BF16)
        else:
            sw = jnp.swapaxes(val.reshape(tm // d, d, val.shape[1]), 0, 1)
            ref[...] = sw.astype(BF16)


def _inproj_kernel(x_ref, mod_ref, w_ref, cos_ref, sin_ref, lb_ref,
                   q1_ref, q4_ref, q16_ref, k1_ref, k4_ref, k16_ref, v1_ref, v4_ref, v16_ref,
                   qh_ref, kh_ref, gh_ref, ih_ref, og_ref):
    x = x_ref[...]
    mod = mod_ref[...]
    h = _rms(x) * (1.0 + mod[1:2]) + mod[0:1]
    hb = h.astype(BF16)
    reps = ATTN_WIDTH // LANES
    cos = jnp.concatenate([cos_ref[...]] * reps, axis=1)
    sin = jnp.concatenate([sin_ref[...]] * reps, axis=1)
    lane = lax.broadcasted_iota(jnp.int32, cos.shape, 1)
    first = (lane % HEAD_DIM) < (HEAD_DIM // 2)
    half = HEAD_DIM // 2

    def proj(j):
        return jnp.dot(hb, w_ref[:, j * 512:(j + 1) * 512], preferred_element_type=F32)

    def rot(t):
        swapped = jnp.where(first, pltpu.roll(t, ATTN_WIDTH - half, 1), pltpu.roll(t, half, 1))
        return t * cos + swapped * sin

    ps = [proj(j) for j in range(IN_COLS // 512)]
    _store_streams(rot(ps[0]) * (HEAD_DIM ** -0.5 * LOG2E), (q1_ref, q4_ref, q16_ref))
    _store_streams(rot(ps[1]), (k1_ref, k4_ref, k16_ref))
    _store_streams(ps[2], (v1_ref, v4_ref, v16_ref))
    qh_ref[...] = (ps[3] * _sigmoid(ps[3])).astype(BF16)
    lb = lb_ref[...]
    f = lb + (1.0 - lb) * _sigmoid(ps[4])
    kh_ref[...] = (1.0 - f).astype(BF16)
    gh_ref[...] = jnp.log(f)
    ih_ref[...] = ps[5].astype(BF16)
    og_ref[...] = _sigmoid(ps[6]).astype(BF16)


def _stream_spec(tm, dil, per_batch, width):
    return pl.BlockSpec((None, dil, tm // dil, width), lambda i: (i // per_batch, 0, i % per_batch, 0))


def _inproj(x2d, mod_l, w_in_l, cos, sin, lb_l, seq):
    t, d = x2d.shape
    tm = TOKEN_TILE
    per_batch = seq // tm
    batch = t // seq
    row = lambda i: (i, 0)
    half_spec = pl.BlockSpec((tm, 512), row)
    bf = jax.ShapeDtypeStruct((t, 512), BF16)
    stream_specs = [_stream_spec(tm, dil, per_batch, ATTN_WIDTH) for _, dil in DILATED_PATTERNS]
    stream_shapes = [jax.ShapeDtypeStruct((batch, dil, seq // dil, ATTN_WIDTH), BF16)
                     for _, dil in DILATED_PATTERNS]
    return pl.pallas_call(
        _inproj_kernel,
        grid=(t // tm,),
        in_specs=[
            pl.BlockSpec((tm, d), row),
            pl.BlockSpec((None, 6, d), lambda i: (i // per_batch, 0, 0)),
            pl.BlockSpec((d, IN_COLS), lambda i: (0, 0)),
            pl.BlockSpec((tm, LANES), row),
            pl.BlockSpec((tm, LANES), row),
            pl.BlockSpec((1, 512), lambda i: (0, 0)),
        ],
        out_specs=stream_specs * 3 + [half_spec] * 5,
        out_shape=stream_shapes * 3 + [bf, bf, jax.ShapeDtypeStruct((t, 512), F32), bf, bf],
        compiler_params=_params("parallel"),
        name="inproj",
    )(x2d, mod_l, w_in_l, cos, sin, lb_l)


def _attn_kernel(q_ref, kc_ref, kp_ref, vc_ref, vp_ref, bias_ref, u_ref, st_ref):
    blk = ATTN_BLOCK
    n = pl.program_id(2)
    keys = jnp.concatenate([kp_ref[...], kc_ref[...]], axis=0)
    vals = jnp.concatenate([vp_ref[...], vc_ref[...]], axis=0)
    biases = [bias_ref[jnp.minimum(n, 1)]] + [bias_ref[1]] * (ATTN_STEP_BLOCKS - 1)
    lane = lax.broadcasted_iota(jnp.int32, (blk, LANES), 1)
    low = lane < HEAD_DIM
    zero = jnp.zeros((), BF16)
    ones = jnp.ones((2 * blk, LANES), BF16)
    slabs = [slice(pair * LANES, (pair + 1) * LANES) for pair in range(N_HEADS // 2)]
    work = [(j, pair) for j in range(ATTN_STEP_BLOCKS) for pair in range(N_HEADS // 2)]

    scores = []
    for j, pair in work:
        qp = q_ref[j * blk:(j + 1) * blk, slabs[pair]]
        qst = jnp.concatenate([jnp.where(low, qp, zero), jnp.where(low, zero, qp)], axis=0)
        kwin = keys[j * blk:(j + 2) * blk, slabs[pair]]
        scores.append(lax.dot_general(qst, kwin, _NT, preferred_element_type=F32) + biases[j])
    maxes, probs = [], []
    for s in scores:
        m = jnp.max(s, axis=1, keepdims=True).astype(BF16).astype(F32)
        maxes.append(m)
        probs.append(jnp.exp2((s - m).astype(BF16)))
    stats = [jnp.zeros((blk, LANES), F32) for _ in range(ATTN_STEP_BLOCKS)]
    for (j, pair), m, p in zip(work, maxes, probs):
        vwin = vals[j * blk:(j + 2) * blk, slabs[pair]]
        ul = jnp.dot(p, jnp.concatenate([vwin, ones], axis=1), preferred_element_type=F32)
        u, l = ul[:, :LANES], ul[:, LANES:]
        for hh in range(2):
            head = 2 * pair + hh
            stats[j] = jnp.where(lane == head, m[hh * blk:(hh + 1) * blk], stats[j])
            stats[j] = jnp.where(lane == N_HEADS + head, l[hh * blk:(hh + 1) * blk], stats[j])
        u_ref[j * blk:(j + 1) * blk, slabs[pair]] = jnp.where(low, u[:blk], u[blk:]).astype(BF16)
    for j in range(ATTN_STEP_BLOCKS):
        st_ref[j * blk:(j + 1) * blk, :] = stats[j]


def _band_bias(steps):
    blk = ATTN_BLOCK
    qi = np.arange(2 * blk)[:, None] % blk
    kj = np.arange(2 * blk)[None, :]
    dist = qi + blk - kj
    band = (dist >= 0) & (dist <= steps)
    first = band & (kj >= blk)
    return jnp.asarray(np.where(np.stack([first, band]), 0.0, NEG_BIG), F32)


def _attn_pattern(qs, ks, vs, window, dilation):
    batch, _, m, _ = qs.shape
    rows = ATTN_STEP_BLOCKS * ATTN_BLOCK
    nb = m // rows
    bias = _band_bias(window // dilation)
    cur = pl.BlockSpec((None, None, rows, ATTN_WIDTH), lambda b, r, n: (b, r, n, 0))
    prev = pl.BlockSpec((None, None, ATTN_BLOCK, ATTN_WIDTH),
                        lambda b, r, n: (b, r, jnp.maximum(n * ATTN_STEP_BLOCKS - 1, 0), 0))
    return pl.pallas_call(
        _attn_kernel,
        grid=(batch, dilation, nb),
        in_specs=[cur, cur, prev, cur, prev,
                  pl.BlockSpec(bias.shape, lambda b, r, n: (0, 0, 0))],
        out_specs=[cur, pl.BlockSpec((None, None, rows, LANES), lambda b, r, n: (b, r, n, 0))],
        out_shape=[jax.ShapeDtypeStruct((batch, dilation, m, ATTN_WIDTH), BF16),
                   jax.ShapeDtypeStruct((batch, dilation, m, LANES), F32)],
        compiler_params=_params("parallel", "parallel", "arbitrary"),
        name=f"attn_d{dilation}",
    )(qs, ks, ks, vs, vs, bias)


def _hgrn_kernel(q_ref, k_ref, g_ref, v_ref, ones_ref, o_ref, st_ref):
    @pl.when(pl.program_id(1) == 0)
    def _():
        st_ref[...] = jnp.zeros_like(st_ref)

    state = [st_ref[grp] for grp in range(st_ref.shape[0])]
    for chunk in range(HGRN_STEP_CHUNKS):
        state = _hgrn_chunk(chunk * HGRN_CHUNK, q_ref, k_ref, g_ref, v_ref, ones_ref, o_ref, state)
    for grp, st in enumerate(state):
        st_ref[grp] = st


def _hgrn_chunk(off, q_ref, k_ref, g_ref, v_ref, ones_ref, o_ref, state):
    c, sub, w = HGRN_CHUNK, HGRN_SUB, HGRN_WIDTH
    span = slice(off, off + c)
    q = q_ref[span, :].astype(F32)
    k = k_ref[span, :].astype(F32)
    vb = v_ref[span, :]
    v = vb.astype(F32)
    g = g_ref[span, :]

    ri = lax.broadcasted_iota(jnp.int32, (c, c), 0)
    ci = lax.broadcasted_iota(jnp.int32, (c, c), 1)
    tri = (ci <= ri).astype(BF16)
    g1, g2, g3 = _split3(g)
    cs = lambda t: jnp.dot(tri, t, preferred_element_type=F32)
    b = (cs(g1) + (cs(g2) + cs(g3))) * LOG2E
    b_last = b[c - 1:c, :]

    hw = state[0].shape[0]
    qd = (q * jnp.exp2(b)).astype(BF16)
    kt = (k * jnp.exp2(b_last - b)).astype(BF16)
    decay = jnp.exp2(b_last)
    r0i = lax.broadcasted_iota(jnp.int32, (hw, hw), 0) // HEAD_DIM
    c0i = lax.broadcasted_iota(jnp.int32, (hw, hw), 1) // HEAD_DIM
    same_head = r0i == c0i
    o_parts, new_state = [], []
    for grp, st in enumerate(state):
        sl = slice(grp * hw, (grp + 1) * hw)
        o_parts.append(lax.dot_general(qd[:, sl], st.astype(BF16), _NT, preferred_element_type=F32))
        upd = lax.dot_general(vb[:, sl], kt[:, sl], _TN, preferred_element_type=F32)
        new_state.append(st * decay[:, sl] + jnp.where(same_head, upd, 0.0))
    o_inter = jnp.concatenate(o_parts, axis=1)

    gheads = hw // HEAD_DIM
    groups = [slice(grp * hw, (grp + 1) * hw) for grp in range(w // hw)]
    hrow = lax.broadcasted_iota(jnp.int32, (gheads * sub, hw), 0) // sub
    hlane = lax.broadcasted_iota(jnp.int32, (gheads * sub, hw), 1) // HEAD_DIM
    hmask = hrow == hlane
    trow = lax.broadcasted_iota(jnp.int32, (sub, w), 0)
    ones = ones_ref[...]
    ow = ones.shape[0]

    nblk = c // sub
    ws = []
    for blk in range(nblk):
        rows = slice(blk * sub, (blk + 1) * sub)
        bi, qi, ki = b[rows], q[rows], k[rows]
        for s in range(sub):
            e = jnp.exp2(bi - bi[s:s + 1])
            ws.append(jnp.where(trow >= s, qi * (ki[s:s + 1] * e), 0.0))
    wcat = jnp.concatenate(ws, axis=0).astype(BF16)
    sc = jnp.concatenate(
        [jnp.dot(wcat[:, j * ow:(j + 1) * ow], ones, preferred_element_type=F32) for j in range(w // ow)],
        axis=1)

    inter_scores = [None]
    for blk in range(1, nblk):
        r0 = blk * sub
        rows = slice(r0, r0 + sub)
        bref = b[r0:r0 + 1]
        qs = q[rows] * jnp.exp2(b[rows] - bref)
        kp = (k[0:r0] * jnp.exp2(bref - b[0:r0])).astype(BF16)
        per_group = []
        for sl in groups:
            qexp = jnp.where(hmask, jnp.concatenate([qs[:, sl]] * gheads, axis=0), 0.0).astype(BF16)
            per_group.append(lax.dot_general(qexp, kp[:, sl], _NT, preferred_element_type=F32))
        inter_scores.append(per_group)

    for blk in range(nblk):
        r0 = blk * sub
        rows = slice(r0, r0 + sub)
        acc = o_inter[rows]
        if blk > 0:
            parts = []
            for sl, a in zip(groups, inter_scores[blk]):
                oexp = jnp.dot(a.astype(BF16), vb[0:r0, sl], preferred_element_type=F32)
                oexp = jnp.where(hmask, oexp, 0.0)
                part = oexp[0:sub]
                for hd in range(1, gheads):
                    part = part + oexp[hd * sub:(hd + 1) * sub]
                parts.append(part)
            acc = acc + jnp.concatenate(parts, axis=1)
        vi = v[rows]
        for s in range(sub):
            at = (blk * sub + s) * sub
            acc = acc + sc[at:at + sub] * vi[s:s + 1]
        o_ref[off + r0:off + r0 + sub, :] = acc
    return new_state


def _hgrn(qh, kh, gh, ih, ones_bd, batch, seq):
    c, w = HGRN_CHUNK, HGRN_WIDTH
    view = lambda a: a.reshape(batch, seq, w)
    rows = HGRN_STEP_CHUNKS * c
    blk = pl.BlockSpec((None, rows, w), lambda b, n: (b, n, 0))
    o = pl.pallas_call(
        _hgrn_kernel,
        grid=(batch, seq // rows),
        in_specs=[blk, blk, blk, blk, pl.BlockSpec(ones_bd.shape, lambda b, n: (0, 0))],
        out_specs=blk,
        out_shape=jax.ShapeDtypeStruct((batch, seq, w), F32),
        scratch_shapes=[pltpu.VMEM((w // MXU_WIDTH, MXU_WIDTH, MXU_WIDTH), F32)],
        compiler_params=_params("parallel", "arbitrary"),
        name="hgrn2",
    )(view(qh), view(kh), view(gh), view(ih), ones_bd)
    return o.reshape(batch * seq, w)


def _route(probs):
    rows = [probs[i:i + 1, :] for i in range(N_EXPERTS)]
    gsum = []
    for gidx in range(N_GROUPS):
        a, b_, c_, d_ = rows[4 * gidx:4 * gidx + 4]
        hi1, lo1 = jnp.maximum(a, b_), jnp.minimum(a, b_)
        hi2, lo2 = jnp.maximum(c_, d_), jnp.minimum(c_, d_)
        top1 = jnp.maximum(hi1, hi2)
        second = jnp.maximum(jnp.minimum(hi1, hi2), jnp.maximum(lo1, lo2))
        gsum.append(top1 + second)
    best, gi = gsum[0], jnp.zeros(gsum[0].shape, jnp.int32)
    for gidx in range(1, N_GROUPS):
        upd = gsum[gidx] > best
        best = jnp.where(upd, gsum[gidx], best)
        gi = jnp.where(upd, gidx, gi)
    vals = []
    for j in range(EXPERTS_PER_GROUP):
        vj = rows[j]
        for gidx in range(1, N_GROUPS):
            vj = jnp.where(gi == gidx, rows[4 * gidx + j], vj)
        vals.append(vj)
    v1, i1 = vals[0], jnp.zeros(gi.shape, jnp.int32)
    for j in range(1, EXPERTS_PER_GROUP):
        upd = vals[j] > v1
        v1 = jnp.where(upd, vals[j], v1)
        i1 = jnp.where(upd, j, i1)
    v2, i2 = jnp.full(v1.shape, -1.0, F32), jnp.zeros(gi.shape, jnp.int32)
    for j in range(EXPERTS_PER_GROUP):
        upd = (i1 != j) & (vals[j] > v2)
        v2 = jnp.where(upd, vals[j], v2)
        i2 = jnp.where(upd, j, i2)
    tot = v1 + v2
    base = gi * EXPERTS_PER_GROUP
    return base + i1, base + i2, v1 / tot, v2 / tot


def _outproj_kernel(u1_ref, u2_ref, u3_ref, s1_ref, s2_ref, s3_ref, o_ref, og_ref, x_ref, mod_ref,
                    an_ref, hn_ref, w_ref, em_ref, wr_ref, br_ref,
                    x1_ref, h2_ref, eidx_ref, wcol_ref, uscr_ref, sscr_ref):
    us = (u1_ref, u2_ref, u3_ref)
    ss = (s1_ref, s2_ref, s3_ref)
    em = em_ref[...]

    def natural(ref):
        dil, per, width = ref.shape
        if dil == 1:
            return ref[0].astype(F32)
        scr = uscr_ref if width == ATTN_WIDTH else sscr_ref
        slabs = scr.shape[0]
        for r in range(dil):
            val = ref[r].astype(F32)
            for c in range(slabs):
                scr[c, pl.ds(r, per, stride=dil), :] = val[:, c * LANES:(c + 1) * LANES]
        return jnp.concatenate([scr[c] for c in range(slabs)], axis=1)

    stats = [natural(s_ref) for s_ref in ss]
    mmax = jnp.maximum(jnp.maximum(stats[0], stats[1]), stats[2])
    wts = [jnp.exp2(s - mmax) for s in stats]
    den = jnp.zeros_like(mmax)
    for s, wgt in zip(stats, wts):
        den = den + pltpu.roll(wgt, N_HEADS, 1) * s
    inv = 1.0 / pltpu.roll(den, LANES - N_HEADS, 1)
    head_lane = lax.broadcasted_iota(jnp.int32, mmax.shape, 1) < N_HEADS
    attn = jnp.zeros((mmax.shape[0], ATTN_WIDTH), F32)
    for p, wgt in enumerate(wts):
        share = jnp.where(head_lane, wgt * inv, 0.0).astype(BF16)
        attn = attn + jnp.dot(share, em, preferred_element_type=F32) * natural(us[p])
    mod = mod_ref[...]
    a_n = _rms(attn) * an_ref[...]
    rec = _rms(o_ref[...]) * hn_ref[...] * og_ref[...].astype(F32)
    merged = jnp.concatenate([a_n, rec], axis=1).astype(BF16)
    mix = jnp.dot(merged, w_ref[...], preferred_element_type=F32)
    x1 = x_ref[...] + mod[2:3] * mix
    x1_ref[...] = x1
    h2 = _rms(x1) * (1.0 + mod[4:5]) + mod[3:4]
    h2_ref[...] = h2

    logits = _dot_hi(wr_ref[...], h2, _NT) + br_ref[...]
    e = jnp.exp(logits - jnp.max(logits, axis=0, keepdims=True))
    probs = e / jnp.sum(e, axis=0, keepdims=True)
    e1, e2, w1, w2 = _route(probs)
    eidx_ref[...] = jnp.concatenate([e1, e2], axis=0)
    tm = probs.shape[1]
    srow = lax.broadcasted_iota(jnp.int32, (LANES, tm), 0)
    w_t = jnp.where(srow == 0, w1, jnp.where(srow == 1, w2, 0.0))
    wcol_ref[...] = w_t.T


def _outproj(us, sts, o_h, og, x2d, mod_l, an_l, hn_l, w_out_l, em, wr_t, br, seq):
    t, d = x2d.shape
    tm = TOKEN_TILE
    per_batch = seq // tm
    row = lambda i: (i, 0)
    const = lambda i: (0, 0)
    half = pl.BlockSpec((tm, 512), row)
    stat = pl.BlockSpec((tm, LANES), row)
    full = pl.BlockSpec((tm, d), row)
    u_specs = [_stream_spec(tm, dil, per_batch, ATTN_WIDTH) for _, dil in DILATED_PATTERNS]
    s_specs = [_stream_spec(tm, dil, per_batch, LANES) for _, dil in DILATED_PATTERNS]
    return pl.pallas_call(
        _outproj_kernel,
        grid=(t // tm,),
        in_specs=u_specs + s_specs + [half, half, full,
                  pl.BlockSpec((None, 6, d), lambda i: (i // per_batch, 0, 0)),
                  pl.BlockSpec((1, 512), const), pl.BlockSpec((1, 512), const),
                  pl.BlockSpec((d, d), const),
                  pl.BlockSpec((LANES, 512), const),
                  pl.BlockSpec((N_EXPERTS, d), const), pl.BlockSpec((N_EXPERTS, 1), const)],
        out_specs=[full, full, pl.BlockSpec((N_ASSIGN, tm), lambda i: (0, i)), stat],
        out_shape=[jax.ShapeDtypeStruct((t, d), F32), jax.ShapeDtypeStruct((t, d), F32),
                   jax.ShapeDtypeStruct((N_ASSIGN, t), jnp.int32), jax.ShapeDtypeStruct((t, LANES), F32)],
        scratch_shapes=[pltpu.VMEM((ATTN_WIDTH // LANES, tm, LANES), F32), pltpu.VMEM((1, tm, LANES), F32)],
        compiler_params=_params("parallel"),
        name="outproj_route",
    )(*us, *sts, o_h, og, x2d, mod_l, an_l, hn_l, w_out_l, em, wr_t, br)


def _one_hots(e_ref):
    e = e_ref[...]
    erow = lax.broadcasted_iota(jnp.int32, (N_EXPERTS, e.shape[1]), 0)
    return (erow == e[0:1]).astype(F32), (erow == e[1:2]).astype(F32)


def _spread(col):
    return jnp.broadcast_to(col, (N_EXPERTS, LANES))


def _count_kernel(e_ref, cnt_ref):
    @pl.when(pl.program_id(0) == 0)
    def _():
        cnt_ref[...] = jnp.zeros_like(cnt_ref)

    oh0, oh1 = _one_hots(e_ref)
    cnt_ref[...] += _spread(jnp.sum(oh0 + oh1, axis=1, keepdims=True))


def _positions_kernel(e_ref, cnt_ref, pos_ref, meta_ref, carry_ref, offs_ref):
    i = pl.program_id(0)
    tp = e_ref.shape[1]
    oh0, oh1 = _one_hots(e_ref)

    @pl.when(i == 0)
    def _():
        cnt = cnt_ref[...]
        padded = jnp.floor((cnt + (MOE_TILE - 1)) * (1.0 / MOE_TILE)) * MOE_TILE
        run = jnp.zeros((1, LANES), F32)
        starts = []
        for ex in range(N_EXPERTS):
            starts.append(run)
            run = run + padded[ex:ex + 1]
        offs = jnp.concatenate(starts, axis=0)
        offs_ref[...] = offs
        carry_ref[...] = jnp.zeros_like(carry_ref)
        ends = offs + padded
        lane = lax.broadcasted_iota(jnp.int32, (N_EXPERTS, LANES), 1)
        srow = lax.broadcasted_iota(jnp.int32, (N_EXPERTS, LANES), 0)
        tile_start = (lane * MOE_TILE).astype(F32)
        tile_expert = jnp.sum((ends <= tile_start).astype(F32), axis=0, keepdims=True)
        tile_expert = jnp.minimum(tile_expert, N_EXPERTS - 1.0)
        on_diag = srow == lane
        ends_lane = jnp.sum(jnp.where(on_diag, ends, 0.0), axis=0, keepdims=True)
        pad_lane = jnp.sum(jnp.where(on_diag, padded, 0.0), axis=0, keepdims=True)
        meta = jnp.concatenate([tile_expert, run * (1.0 / MOE_TILE), ends_lane, pad_lane,
                                jnp.zeros((4, LANES), F32)], axis=0)
        meta_ref[...] = meta.astype(jnp.int32)

    r = lax.broadcasted_iota(jnp.int32, (tp, tp), 0)
    c = lax.broadcasted_iota(jnp.int32, (tp, tp), 1)
    upper = (r <= c).astype(BF16)
    oh = jnp.concatenate([oh0, oh1], axis=0).astype(BF16)
    pre = jnp.dot(oh, upper, preferred_element_type=F32)
    pre0, pre1 = pre[:N_EXPERTS], pre[N_EXPERTS:]
    tot0, tot1 = pre0[:, tp - 1:tp], pre1[:, tp - 1:tp]
    base = offs_ref[...][:, 0:1] + carry_ref[...][:, 0:1]
    p0 = jnp.sum(oh0 * (pre0 - 1.0 + base), axis=0, keepdims=True)
    p1 = jnp.sum(oh1 * (pre1 - 1.0 + (base + tot0)), axis=0, keepdims=True)
    pos_ref[...] = jnp.concatenate([p0, p1], axis=0).astype(jnp.int32)
    carry_ref[...] += _spread(tot0 + tot1)


def _positions(eidx):
    t = eidx.shape[1]
    tp = POS_TILE
    blk = pl.BlockSpec((N_ASSIGN, tp), lambda i: (0, i))
    whole = pl.BlockSpec((N_EXPERTS, LANES), lambda i: (0, 0))
    stat = pltpu.VMEM((N_EXPERTS, LANES), F32)
    cnt = pl.pallas_call(
        _count_kernel,
        grid=(t // tp,),
        in_specs=[blk],
        out_specs=whole,
        out_shape=jax.ShapeDtypeStruct((N_EXPERTS, LANES), F32),
        compiler_params=_params("arbitrary"),
        name="moe_count",
    )(eidx)
    return pl.pallas_call(
        _positions_kernel,
        grid=(t // tp,),
        in_specs=[blk, whole],
        out_specs=[blk, pl.BlockSpec((8, LANES), lambda i: (0, 0))],
        out_shape=[jax.ShapeDtypeStruct((N_ASSIGN, t), jnp.int32), jax.ShapeDtypeStruct((8, LANES), jnp.int32)],
        scratch_shapes=[stat, stat],
        compiler_params=_params("arbitrary"),
        name="moe_positions",
    )(eidx, cnt)


def _row(ref, r):
    return ref.at[r >> 3, pl.ds(r & (SUBLANES - 1), 1), :]


def _dispatch_kernel(ends_ref, pad_ref, used_ref, pos0_ref, pos1_ref, h_ref, xs_hbm, zero_ref, sem):
    i = pl.program_id(0)
    groups = h_ref.shape[0]
    tile_groups = MOE_TILE // SUBLANES

    def clear_tile(start):
        first = pl.multiple_of(start // SUBLANES, tile_groups)
        cp = pltpu.make_async_copy(zero_ref, xs_hbm.at[pl.ds(first, tile_groups)], sem)
        cp.start()
        cp.wait()

    @pl.when(i == 0)
    def _():
        zero_ref[...] = jnp.zeros_like(zero_ref)
        for ex in range(N_EXPERTS):
            @pl.when(pad_ref[ex] > 0)
            def _():
                clear_tile(ends_ref[ex] - MOE_TILE)

        def clear_tail(tile, carry):
            clear_tile(tile * MOE_TILE)
            return carry

        lax.fori_loop(used_ref[0], xs_hbm.shape[0] // tile_groups, clear_tail, 0)

    def issue(g, carry):
        for u in range(SUBLANES):
            src = h_ref.at[g, pl.ds(u, 1), :]
            for pos_ref in (pos0_ref, pos1_ref):
                pltpu.make_async_copy(src, _row(xs_hbm, pos_ref[g * SUBLANES + u]), sem).start()
        return carry

    def drain(g, carry):
        for _ in range(SUBLANES * N_ASSIGN):
            pltpu.make_async_copy(h_ref.at[0, pl.ds(0, 1), :], _row(xs_hbm, 0), sem).wait()
        return carry

    lax.fori_loop(0, groups, issue, 0)
    lax.fori_loop(0, groups, drain, 0)


def _dispatch(ends, padded, n_used, pos, h2):
    t, d = h2.shape
    tp = DMA_TILE
    rows = N_ASSIGN * t + N_EXPERTS * MOE_TILE
    slot = pl.BlockSpec((tp,), lambda i, *_: (i,), memory_space=pltpu.SMEM)
    xs = pl.pallas_call(
        _dispatch_kernel,
        grid_spec=pltpu.PrefetchScalarGridSpec(
            num_scalar_prefetch=3,
            grid=(t // tp,),
            in_specs=[slot, slot, pl.BlockSpec((tp // SUBLANES, SUBLANES, d), lambda i, *_: (i, 0, 0))],
            out_specs=pl.BlockSpec(memory_space=pl.ANY),
            scratch_shapes=[pltpu.VMEM((MOE_TILE // SUBLANES, SUBLANES, d), F32), pltpu.SemaphoreType.DMA(())],
        ),
        out_shape=jax.ShapeDtypeStruct((rows // SUBLANES, SUBLANES, d), F32),
        compiler_params=_params("arbitrary"),
        name="moe_dispatch",
    )(ends, padded, n_used, pos[0], pos[1], h2.reshape(t // SUBLANES, SUBLANES, d))
    return xs.reshape(rows, d)


def _experts_kernel(te_ref, nv_ref, xs_ref, wg_ref, wu_ref, wd_ref, ys_ref, wgb_ref, wub_ref, wdb_ref):
    j = pl.program_id(0)
    used = nv_ref[0]
    jc = jnp.minimum(j, used - 1)
    new_expert = jnp.logical_or(j == 0, te_ref[jc] != te_ref[jnp.maximum(jc - 1, 0)])

    @pl.when(jnp.logical_and(j < used, new_expert))
    def _():
        wgb_ref[...] = wg_ref[...].astype(BF16)
        wub_ref[...] = wu_ref[...].astype(BF16)
        wdb_ref[...] = wd_ref[...].astype(BF16)

    @pl.when(j < used)
    def _():
        xb = xs_ref[...].astype(BF16)
        gt = jnp.dot(xb, wgb_ref[...], preferred_element_type=F32)
        up = jnp.dot(xb, wub_ref[...], preferred_element_type=F32)
        a = (gt * _sigmoid(gt) * up).astype(BF16)
        ys_ref[...] = jnp.dot(a, wdb_ref[...], preferred_element_type=F32)

    @pl.when(j >= used)
    def _():
        ys_ref[...] = jnp.zeros_like(ys_ref)


def _experts(tile_expert, n_used, xs, w_gate, w_up, w_down, layer):
    rows, d = xs.shape
    f = D_FF_EXPERT
    tile = lambda j, te, nv: (jnp.minimum(j, nv[0] - 1), 0)
    wsel = lambda j, te, nv: (layer, te[jnp.minimum(j, nv[0] - 1)], 0, 0)
    return pl.pallas_call(
        _experts_kernel,
        grid_spec=pltpu.PrefetchScalarGridSpec(
            num_scalar_prefetch=2,
            grid=(rows // MOE_TILE,),
            in_specs=[pl.BlockSpec((MOE_TILE, d), tile),
                      pl.BlockSpec((None, None, d, f), wsel), pl.BlockSpec((None, None, d, f), wsel),
                      pl.BlockSpec((None, None, f, d), wsel)],
            out_specs=pl.BlockSpec((MOE_TILE, d), lambda j, te, nv: (j, 0)),
            scratch_shapes=[pltpu.VMEM((d, f), BF16), pltpu.VMEM((d, f), BF16), pltpu.VMEM((f, d), BF16)],
        ),
        out_shape=jax.ShapeDtypeStruct((rows, d), F32),
        compiler_params=_params("arbitrary"),
        name="moe_experts",
    )(tile_expert, n_used, xs, w_gate, w_up, w_down)


def _combine_kernel(final, pos0_ref, pos1_ref, ys_hbm, wcol_ref, x_ref, mod_ref, fn_ref, o_ref, buf_ref, sem):
    tc, d = x_ref.shape
    groups = tc // SUBLANES

    def issue(g, carry):
        for u in range(SUBLANES):
            for k, pos_ref in enumerate((pos0_ref, pos1_ref)):
                pltpu.make_async_copy(_row(ys_hbm, pos_ref[g * SUBLANES + u]),
                                      buf_ref.at[k, g, pl.ds(u, 1), :], sem).start()
        return carry

    def drain(g, carry):
        for _ in range(SUBLANES * N_ASSIGN):
            pltpu.make_async_copy(_row(ys_hbm, 0), buf_ref.at[0, 0, pl.ds(0, 1), :], sem).wait()
        return carry

    lax.fori_loop(0, groups, issue, 0)
    lax.fori_loop(0, groups, drain, 0)
    w = wcol_ref[...]
    ffn = w[:, 0:1] * buf_ref[0].reshape(tc, d) + w[:, 1:2] * buf_ref[1].reshape(tc, d)
    y = x_ref[...] + mod_ref[...][5:6] * ffn
    if final:
        y = _rms(y) * fn_ref[...]
    o_ref[...] = y


def _combine(pos, ys, wcol, x1, mod_l, fnorm, seq, final):
    t, d = x1.shape
    tc = DMA_TILE
    per_batch = seq // tc
    row = lambda i: (i, 0)
    slot = pl.BlockSpec((tc,), lambda i: (i,), memory_space=pltpu.SMEM)
    return pl.pallas_call(
        functools.partial(_combine_kernel, final),
        grid=(t // tc,),
        in_specs=[slot, slot,
                  pl.BlockSpec(memory_space=pl.ANY),
                  pl.BlockSpec((tc, LANES), row), pl.BlockSpec((tc, d), row),
                  pl.BlockSpec((None, 6, d), lambda i: (i // per_batch, 0, 0)),
                  pl.BlockSpec((1, d), lambda i: (0, 0))],
        out_specs=pl.BlockSpec((tc, d), row),
        out_shape=jax.ShapeDtypeStruct((t, d), F32),
        scratch_shapes=[pltpu.VMEM((N_ASSIGN, tc // SUBLANES, SUBLANES, d), F32), pltpu.SemaphoreType.DMA(())],
        compiler_params=_params("arbitrary"),
        name="moe_combine",
    )(pos[0], pos[1], ys.reshape(ys.shape[0] // SUBLANES, SUBLANES, d), wcol, x1, mod_l, fnorm)


def _moe(h2, eidx, wcol, w_gate, w_up, w_down, layer, x1, mod_l, fnorm, seq, final):
    pos, meta = _positions(eidx)
    xs = _dispatch(meta[2], meta[3], meta[1], pos, h2)
    ys = _experts(meta[0], meta[1], xs, w_gate, w_up, w_down, layer)
    return _combine(pos, ys, wcol, x1, mod_l, fnorm, seq, final)


def _head_expand():
    m = np.zeros((LANES, ATTN_WIDTH), np.float32)
    for h in range(N_HEADS):
        m[h, h * HEAD_DIM:(h + 1) * HEAD_DIM] = 1.0
    return jnp.asarray(m, BF16)


def kernel(x, c, positions, w_in, w_out, attn_norm, hgrn_norm, lb_params, ada_w, ada_b,
           w_router, b_router, w_gate, w_up, w_down, final_norm):
    batch, seq, d = x.shape
    t = batch * seq
    c_pad = jnp.pad(c, ((0, 8 - batch), (0, 0)))
    mod = _ada_mod(c_pad, ada_w, ada_b)[:, :batch].reshape(DEPTH, batch, 6, d)
    lbs = _lower_bounds(lb_params)
    cos, sin = _rope_tables(positions)
    cos, sin = cos.reshape(t, LANES), sin.reshape(t, LANES)
    head_id = np.arange(MXU_WIDTH) // HEAD_DIM
    ones_bd = jnp.asarray(head_id[:, None] == head_id[None, :], BF16)
    em = _head_expand()
    wr_t = w_router.T
    br = b_router.reshape(N_EXPERTS, 1)
    fnorm = final_norm.reshape(1, d)

    x2d = x.reshape(t, d)
    for l in range(DEPTH):
        outs = _inproj(x2d, mod[l], w_in[l].astype(BF16), cos, sin, lbs[l:l + 1], seq)
        qs, ks, vs = outs[0:3], outs[3:6], outs[6:9]
        qh, kh, gh, ih, og = outs[9:]
        us, sts = [], []
        for p, (window, dilation) in enumerate(DILATED_PATTERNS):
            u, st = _attn_pattern(qs[p], ks[p], vs[p], window, dilation)
            us.append(u)
            sts.append(st)
        o_h = _hgrn(qh, kh, gh, ih, ones_bd, batch, seq)
        x1, h2, eidx, wcol = _outproj(us, sts, o_h, og, x2d, mod[l], attn_norm[l:l + 1], hgrn_norm[l:l + 1],
                                      w_out[l].astype(BF16), em, wr_t, br, seq)
        x2d = _moe(h2, eidx, wcol, w_gate, w_up, w_down, l, x1, mod[l], fnorm, seq,
                   final=(l == DEPTH - 1))
    return x2d.reshape(batch, seq, d)
```

```python
import functools

import numpy as np
import jax
import jax.numpy as jnp
from jax import lax
from jax.experimental import pallas as pl
from jax.experimental.pallas import tpu as pltpu

D_MODEL = 1024
DEPTH = 2
ATTN_WIDTH = 512
HGRN_WIDTH = 512
HEAD_DIM = 64
N_HEADS = 8
DILATED_PATTERNS = ((128, 1), (512, 4), (2048, 16))
ATTN_BLOCK = 128
ATTN_STEP_BLOCKS = 4
ROPE_THETA = 10000.0
N_EXPERTS = 16
N_GROUPS = 4
EXPERTS_PER_GROUP = 4
D_FF_EXPERT = 512
RMS_EPS = 1e-6
IN_COLS = 3 * ATTN_WIDTH + 4 * HGRN_WIDTH

LANES = 128
SUBLANES = 8
MXU_WIDTH = 256
VMEM_LIMIT_BYTES = 56 * 1024 * 1024

TOKEN_TILE = 512
N_ASSIGN = 2
MOE_TILE = 512
POS_TILE = 1024
DMA_TILE = 512
HGRN_CHUNK = 64
HGRN_STEP_CHUNKS = 4
HGRN_SUB = 8
NEG_BIG = -1e30
LOG2E = 1.4426950408889634

F32 = jnp.float32
BF16 = jnp.bfloat16

_NT = (((1,), (1,)), ((), ()))
_TN = (((0,), (0,)), ((), ()))


def _params(*sem):
    return pltpu.CompilerParams(dimension_semantics=sem, vmem_limit_bytes=VMEM_LIMIT_BYTES)


def _sigmoid(x):
    return 1.0 / (1.0 + jnp.exp(-x))


def _rms(x):
    return x * lax.rsqrt(jnp.mean(x * x, axis=-1, keepdims=True) + RMS_EPS)


def _split3(a):
    a1 = a.astype(BF16)
    r1 = a - a1.astype(F32)
    a2 = r1.astype(BF16)
    a3 = (r1 - a2.astype(F32)).astype(BF16)
    return a1, a2, a3


def _dot_hi(a, b, dims):
    a1, a2, _ = _split3(a)
    b1, b2, _ = _split3(b)
    d = lambda p, q: lax.dot_general(p, q, dims, preferred_element_type=F32)
    return d(a1, b1) + (d(a2, b1) + d(a1, b2))


def _ada_kernel(c_ref, w_ref, b_ref, o_ref):
    c = c_ref[...]
    ca = c * _sigmoid(c)
    o_ref[...] = _dot_hi(ca, w_ref[...], (((1,), (0,)), ((), ()))) + b_ref[...]


def _ada_mod(c_pad, ada_w, ada_b):
    depth, d, n = ada_w.shape
    tn = 1536
    rows = c_pad.shape[0]
    return pl.pallas_call(
        _ada_kernel,
        grid=(depth, n // tn),
        in_specs=[
            pl.BlockSpec((rows, d), lambda l, j: (0, 0)),
            pl.BlockSpec((None, d, tn), lambda l, j: (l, 0, j)),
            pl.BlockSpec((None, 1, tn), lambda l, j: (l, 0, j)),
        ],
        out_specs=pl.BlockSpec((None, rows, tn), lambda l, j: (l, 0, j)),
        out_shape=jax.ShapeDtypeStruct((depth, rows, n), F32),
        compiler_params=_params("parallel", "parallel"),
        name="ada_mod",
    )(c_pad, ada_w, ada_b.reshape(depth, 1, n))


def _lb_kernel(p_ref, o_ref):
    p = p_ref[...]
    e = jnp.exp(p - jnp.max(p, axis=0, keepdims=True))
    sm = e / jnp.sum(e, axis=0, keepdims=True)
    run = jnp.zeros_like(sm[0:1])
    for l in range(p.shape[0]):
        run = run + sm[l:l + 1]
        o_ref[l:l + 1, :] = run - sm[0:1]


def _lower_bounds(lb_params):
    return pl.pallas_call(
        _lb_kernel,
        out_shape=jax.ShapeDtypeStruct(lb_params.shape, F32),
        name="lower_bounds",
    )(lb_params)


def _rope_kernel(pos_ref, inv_ref, cos_ref, sin_ref):
    ang = pos_ref[...].astype(F32) * inv_ref[...]
    lane = lax.broadcasted_iota(jnp.int32, ang.shape, 1)
    first = (lane % HEAD_DIM) < (HEAD_DIM // 2)
    s = jnp.sin(ang)
    cos_ref[...] = jnp.cos(ang)
    sin_ref[...] = jnp.where(first, -s, s)


def _rope_tables(positions):
    b, s = positions.shape
    ts = 1024
    half = HEAD_DIM // 2
    inv = ROPE_THETA ** (-jnp.arange(half, dtype=F32) / half)
    inv = jnp.tile(inv, LANES // half).reshape(1, LANES)
    out = jax.ShapeDtypeStruct((b, s, LANES), F32)
    return pl.pallas_call(
        _rope_kernel,
        grid=(b, s // ts),
        in_specs=[
            pl.BlockSpec((None, ts, 1), lambda i, j: (i, j, 0)),
            pl.BlockSpec((1, LANES), lambda i, j: (0, 0)),
        ],
        out_specs=[pl.BlockSpec((None, ts, LANES), lambda i, j: (i, j, 0))] * 2,
        out_shape=[out, out],
        compiler_params=_params("parallel", "parallel"),
        name="rope_tables",
    )(positions.reshape(b, s, 1), inv)


def _store_streams(val, refs):
    tm = val.shape[0]
    for (_, d), ref in zip(DILATED_PATTERNS, refs):
        if d == 1:
            ref[0] = val.astype(BF16)
        else:
            sw = jnp.swapaxes(val.reshape(tm // d, d, val.shape[1]), 0, 1)
            ref[...] = sw.astype(BF16)


def _inproj_kernel(x_ref, mod_ref, w_ref, cos_ref, sin_ref, lb_ref,
                   q1_ref, q4_ref, q16_ref, k1_ref, k4_ref, k16_ref, v1_ref, v4_ref, v16_ref,
                   qh_ref, kh_ref, gh_ref, ih_ref, og_ref):
    x = x_ref[...]
    mod = mod_ref[...]
    h = _rms(x) * (1.0 + mod[1:2]) + mod[0:1]
    hb = h.astype(BF16)
    reps = ATTN_WIDTH // LANES
    cos = jnp.concatenate([cos_ref[...]] * reps, axis=1)
    sin = jnp.concatenate([sin_ref[...]] * reps, axis=1)
    lane = lax.broadcasted_iota(jnp.int32, cos.shape, 1)
    first = (lane % HEAD_DIM) < (HEAD_DIM // 2)
    half = HEAD_DIM // 2

    def proj(j):
        return jnp.dot(hb, w_ref[:, j * 512:(j + 1) * 512], preferred_element_type=F32)

    def rot(t):
        swapped = jnp.where(first, pltpu.roll(t, ATTN_WIDTH - half, 1), pltpu.roll(t, half, 1))
        return t * cos + swapped * sin

    ps = [proj(j) for j in range(IN_COLS // 512)]
    _store_streams(rot(ps[0]) * (HEAD_DIM ** -0.5 * LOG2E), (q1_ref, q4_ref, q16_ref))
    _store_streams(rot(ps[1]), (k1_ref, k4_ref, k16_ref))
    _store_streams(ps[2], (v1_ref, v4_ref, v16_ref))
    qh_ref[...] = (ps[3] * _sigmoid(ps[3])).astype(BF16)
    lb = lb_ref[...]
    f = lb + (1.0 - lb) * _sigmoid(ps[4])
    kh_ref[...] = (1.0 - f).astype(BF16)
    gh_ref[...] = jnp.log(f)
    ih_ref[...] = ps[5].astype(BF16)
    og_ref[...] = _sigmoid(ps[6]).astype(BF16)


def _stream_spec(tm, dil, per_batch, width):
    return pl.BlockSpec((None, dil, tm // dil, width), lambda i: (i // per_batch, 0, i % per_batch, 0))


def _inproj(x2d, mod_l, w_in_l, cos, sin, lb_l, seq):
    t, d = x2d.shape
    tm = TOKEN_TILE
    per_batch = seq // tm
    batch = t // seq
    row = lambda i: (i, 0)
    half_spec = pl.BlockSpec((tm, 512), row)
    bf = jax.ShapeDtypeStruct((t, 512), BF16)
    stream_specs = [_stream_spec(tm, dil, per_batch, ATTN_WIDTH) for _, dil in DILATED_PATTERNS]
    stream_shapes = [jax.ShapeDtypeStruct((batch, dil, seq // dil, ATTN_WIDTH), BF16)
                     for _, dil in DILATED_PATTERNS]
    return pl.pallas_call(
        _inproj_kernel,
        grid=(t // tm,),
        in_specs=[
            pl.BlockSpec((tm, d), row),
            pl.BlockSpec((None, 6, d), lambda i: (i // per_batch, 0, 0)),
            pl.BlockSpec((d, IN_COLS), lambda i: (0, 0)),
            pl.BlockSpec((tm, LANES), row),
            pl.BlockSpec((tm, LANES), row),
            pl.BlockSpec((1, 512), lambda i: (0, 0)),
        ],
        out_specs=stream_specs * 3 + [half_spec] * 5,
        out_shape=stream_shapes * 3 + [bf, bf, jax.ShapeDtypeStruct((t, 512), F32), bf, bf],
        compiler_params=_params("parallel"),
        name="inproj",
    )(x2d, mod_l, w_in_l, cos, sin, lb_l)


def _attn_kernel(q_ref, kc_ref, kp_ref, vc_ref, vp_ref, bias_ref, u_ref, st_ref):
    blk = ATTN_BLOCK
    n = pl.program_id(2)
    keys = jnp.concatenate([kp_ref[...], kc_ref[...]], axis=0)
    vals = jnp.concatenate([vp_ref[...], vc_ref[...]], axis=0)
    biases = [bias_ref[jnp.minimum(n, 1)]] + [bias_ref[1]] * (ATTN_STEP_BLOCKS - 1)
    lane = lax.broadcasted_iota(jnp.int32, (blk, LANES), 1)
    low = lane < HEAD_DIM
    zero = jnp.zeros((), BF16)
    ones = jnp.ones((2 * blk, LANES), BF16)
    slabs = [slice(pair * LANES, (pair + 1) * LANES) for pair in range(N_HEADS // 2)]
    work = [(j, pair) for j in range(ATTN_STEP_BLOCKS) for pair in range(N_HEADS // 2)]

    scores = []
    for j, pair in work:
        qp = q_ref[j * blk:(j + 1) * blk, slabs[pair]]
        qst = jnp.concatenate([jnp.where(low, qp, zero), jnp.where(low, zero, qp)], axis=0)
        kwin = keys[j * blk:(j + 2) * blk, slabs[pair]]
        scores.append(lax.dot_general(qst, kwin, _NT, preferred_element_type=F32) + biases[j])
    maxes, probs = [], []
    for s in scores:
        m = jnp.max(s, axis=1, keepdims=True).astype(BF16).astype(F32)
        maxes.append(m)
        probs.append(jnp.exp2((s - m).astype(BF16)))
    stats = [jnp.zeros((blk, LANES), F32) for _ in range(ATTN_STEP_BLOCKS)]
    for (j, pair), m, p in zip(work, maxes, probs):
        vwin = vals[j * blk:(j + 2) * blk, slabs[pair]]
        ul = jnp.dot(p, jnp.concatenate([vwin, ones], axis=1), preferred_element_type=F32)
        u, l = ul[:, :LANES], ul[:, LANES:]
        for hh in range(2):
            head = 2 * pair + hh
            stats[j] = jnp.where(lane == head, m[hh * blk:(hh + 1) * blk], stats[j])
            stats[j] = jnp.where(lane == N_HEADS + head, l[hh * blk:(hh + 1) * blk], stats[j])
        u_ref[j * blk:(j + 1) * blk, slabs[pair]] = jnp.where(low, u[:blk], u[blk:]).astype(BF16)
    for j in range(ATTN_STEP_BLOCKS):
        st_ref[j * blk:(j + 1) * blk, :] = stats[j]


def _band_bias(steps):
    blk = ATTN_BLOCK
    qi = np.arange(2 * blk)[:, None] % blk
    kj = np.arange(2 * blk)[None, :]
    dist = qi + blk - kj
    band = (dist >= 0) & (dist <= steps)
    first = band & (kj >= blk)
    return jnp.asarray(np.where(np.stack([first, band]), 0.0, NEG_BIG), F32)


def _attn_pattern(qs, ks, vs, window, dilation):
    batch, _, m, _ = qs.shape
    rows = ATTN_STEP_BLOCKS * ATTN_BLOCK
    nb = m // rows
    bias = _band_bias(window // dilation)
    cur = pl.BlockSpec((None, None, rows, ATTN_WIDTH), lambda b, r, n: (b, r, n, 0))
    prev = pl.BlockSpec((None, None, ATTN_BLOCK, ATTN_WIDTH),
                        lambda b, r, n: (b, r, jnp.maximum(n * ATTN_STEP_BLOCKS - 1, 0), 0))
    return pl.pallas_call(
        _attn_kernel,
        grid=(batch, dilation, nb),
        in_specs=[cur, cur, prev, cur, prev,
                  pl.BlockSpec(bias.shape, lambda b, r, n: (0, 0, 0))],
        out_specs=[cur, pl.BlockSpec((None, None, rows, LANES), lambda b, r, n: (b, r, n, 0))],
        out_shape=[jax.ShapeDtypeStruct((batch, dilation, m, ATTN_WIDTH), BF16),
                   jax.ShapeDtypeStruct((batch, dilation, m, LANES), F32)],
        compiler_params=_params("parallel", "parallel", "arbitrary"),
        name=f"attn_d{dilation}",
    )(qs, ks, ks, vs, vs, bias)


def _hgrn_kernel(q_ref, k_ref, g_ref, v_ref, ones_ref, o_ref, st_ref):
    @pl.when(pl.program_id(1) == 0)
    def _():
        st_ref[...] = jnp.zeros_like(st_ref)

    state = [st_ref[grp] for grp in range(st_ref.shape[0])]
    for chunk in range(HGRN_STEP_CHUNKS):
        state = _hgrn_chunk(chunk * HGRN_CHUNK, q_ref, k_ref, g_ref, v_ref, ones_ref, o_ref, state)
    for grp, st in enumerate(state):
        st_ref[grp] = st


def _hgrn_chunk(off, q_ref, k_ref, g_ref, v_ref, ones_ref, o_ref, state):
    c, sub, w = HGRN_CHUNK, HGRN_SUB, HGRN_WIDTH
    span = slice(off, off + c)
    q = q_ref[span, :].astype(F32)
    k = k_ref[span, :].astype(F32)
    vb = v_ref[span, :]
    v = vb.astype(F32)
    g = g_ref[span, :]

    ri = lax.broadcasted_iota(jnp.int32, (c, c), 0)
    ci = lax.broadcasted_iota(jnp.int32, (c, c), 1)
    tri = (ci <= ri).astype(BF16)
    g1, g2, g3 = _split3(g)
    cs = lambda t: jnp.dot(tri, t, preferred_element_type=F32)
    b = (cs(g1) + (cs(g2) + cs(g3))) * LOG2E
    b_last = b[c - 1:c, :]

    hw = state[0].shape[0]
    qd = (q * jnp.exp2(b)).astype(BF16)
    kt = (k * jnp.exp2(b_last - b)).astype(BF16)
    decay = jnp.exp2(b_last)
    r0i = lax.broadcasted_iota(jnp.int32, (hw, hw), 0) // HEAD_DIM
    c0i = lax.broadcasted_iota(jnp.int32, (hw, hw), 1) // HEAD_DIM
    same_head = r0i == c0i
    o_parts, new_state = [], []
    for grp, st in enumerate(state):
        sl = slice(grp * hw, (grp + 1) * hw)
        o_parts.append(lax.dot_general(qd[:, sl], st.astype(BF16), _NT, preferred_element_type=F32))
        upd = lax.dot_general(vb[:, sl], kt[:, sl], _TN, preferred_element_type=F32)
        new_state.append(st * decay[:, sl] + jnp.where(same_head, upd, 0.0))
    o_inter = jnp.concatenate(o_parts, axis=1)

    gheads = hw // HEAD_DIM
    groups = [slice(grp * hw, (grp + 1) * hw) for grp in range(w // hw)]
    hrow = lax.broadcasted_iota(jnp.int32, (gheads * sub, hw), 0) // sub
    hlane = lax.broadcasted_iota(jnp.int32, (gheads * sub, hw), 1) // HEAD_DIM
    hmask = hrow == hlane
    trow = lax.broadcasted_iota(jnp.int32, (sub, w), 0)
    ones = ones_ref[...]
    ow = ones.shape[0]

    nblk = c // sub
    ws = []
    for blk in range(nblk):
        rows = slice(blk * sub, (blk + 1) * sub)
        bi, qi, ki = b[rows], q[rows], k[rows]
        for s in range(sub):
            e = jnp.exp2(bi - bi[s:s + 1])
            ws.append(jnp.where(trow >= s, qi * (ki[s:s + 1] * e), 0.0))
    wcat = jnp.concatenate(ws, axis=0).astype(BF16)
    sc = jnp.concatenate(
        [jnp.dot(wcat[:, j * ow:(j + 1) * ow], ones, preferred_element_type=F32) for j in range(w // ow)],
        axis=1)

    inter_scores = [None]
    for blk in range(1, nblk):
        r0 = blk * sub
        rows = slice(r0, r0 + sub)
        bref = b[r0:r0 + 1]
        qs = q[rows] * jnp.exp2(b[rows] - bref)
        kp = (k[0:r0] * jnp.exp2(bref - b[0:r0])).astype(BF16)
        per_group = []
        for sl in groups:
            qexp = jnp.where(hmask, jnp.concatenate([qs[:, sl]] * gheads, axis=0), 0.0).astype(BF16)
            per_group.append(lax.dot_general(qexp, kp[:, sl], _NT, preferred_element_type=F32))
        inter_scores.append(per_group)

    for blk in range(nblk):
        r0 = blk * sub
        rows = slice(r0, r0 + sub)
        acc = o_inter[rows]
        if blk > 0:
            parts = []
            for sl, a in zip(groups, inter_scores[blk]):
                oexp = jnp.dot(a.astype(BF16), vb[0:r0, sl], preferred_element_type=F32)
                oexp = jnp.where(hmask, oexp, 0.0)
                part = oexp[0:sub]
                for hd in range(1, gheads):
                    part = part + oexp[hd * sub:(hd + 1) * sub]
                parts.append(part)
            acc = acc + jnp.concatenate(parts, axis=1)
        vi = v[rows]
        for s in range(sub):
            at = (blk * sub + s) * sub
            acc = acc + sc[at:at + sub] * vi[s:s + 1]
        o_ref[off + r0:off + r0 + sub, :] = acc
    return new_state


def _hgrn(qh, kh, gh, ih, ones_bd, batch, seq):
    c, w = HGRN_CHUNK, HGRN_WIDTH
    view = lambda a: a.reshape(batch, seq, w)
    rows = HGRN_STEP_CHUNKS * c
    blk = pl.BlockSpec((None, rows, w), lambda b, n: (b, n, 0))
    o = pl.pallas_call(
        _hgrn_kernel,
        grid=(batch, seq // rows),
        in_specs=[blk, blk, blk, blk, pl.BlockSpec(ones_bd.shape, lambda b, n: (0, 0))],
        out_specs=blk,
        out_shape=jax.ShapeDtypeStruct((batch, seq, w), F32),
        scratch_shapes=[pltpu.VMEM((w // MXU_WIDTH, MXU_WIDTH, MXU_WIDTH), F32)],
        compiler_params=_params("parallel", "arbitrary"),
        name="hgrn2",
    )(view(qh), view(kh), view(gh), view(ih), ones_bd)
    return o.reshape(batch * seq, w)


def _route(probs):
    rows = [probs[i:i + 1, :] for i in range(N_EXPERTS)]
    gsum = []
    for gidx in range(N_GROUPS):
        a, b_, c_, d_ = rows[4 * gidx:4 * gidx + 4]
        hi1, lo1 = jnp.maximum(a, b_), jnp.minimum(a, b_)
        hi2, lo2 = jnp.maximum(c_, d_), jnp.minimum(c_, d_)
        top1 = jnp.maximum(hi1, hi2)
        second = jnp.maximum(jnp.minimum(hi1, hi2), jnp.maximum(lo1, lo2))
        gsum.append(top1 + second)
    best, gi = gsum[0], jnp.zeros(gsum[0].shape, jnp.int32)
    for gidx in range(1, N_GROUPS):
        upd = gsum[gidx] > best
        best = jnp.where(upd, gsum[gidx], best)
        gi = jnp.where(upd, gidx, gi)
    vals = []
    for j in range(EXPERTS_PER_GROUP):
        vj = rows[j]
        for gidx in range(1, N_GROUPS):
            vj = jnp.where(gi == gidx, rows[4 * gidx + j], vj)
        vals.append(vj)
    v1, i1 = vals[0], jnp.zeros(gi.shape, jnp.int32)
    for j in range(1, EXPERTS_PER_GROUP):
        upd = vals[j] > v1
        v1 = jnp.where(upd, vals[j], v1)
        i1 = jnp.where(upd, j, i1)
    v2, i2 = jnp.full(v1.shape, -1.0, F32), jnp.zeros(gi.shape, jnp.int32)
    for j in range(EXPERTS_PER_GROUP):
        upd = (i1 != j) & (vals[j] > v2)
        v2 = jnp.where(upd, vals[j], v2)
        i2 = jnp.where(upd, j, i2)
    tot = v1 + v2
    base = gi * EXPERTS_PER_GROUP
    return base + i1, base + i2, v1 / tot, v2 / tot


def _outproj_kernel(u1_ref, u2_ref, u3_ref, s1_ref, s2_ref, s3_ref, o_ref, og_ref, x_ref, mod_ref,
                    an_ref, hn_ref, w_ref, em_ref, wr_ref, br_ref,
                    x1_ref, h2_ref, eidx_ref, wcol_ref, uscr_ref, sscr_ref):
    us = (u1_ref, u2_ref, u3_ref)
    ss = (s1_ref, s2_ref, s3_ref)
    em = em_ref[...]

    def natural(ref):
        dil, per, width = ref.shape
        if dil == 1:
            return ref[0].astype(F32)
        scr = uscr_ref if width == ATTN_WIDTH else sscr_ref
        slabs = scr.shape[0]
        for r in range(dil):
            val = ref[r].astype(F32)
            for c in range(slabs):
                scr[c, pl.ds(r, per, stride=dil), :] = val[:, c * LANES:(c + 1) * LANES]
        return jnp.concatenate([scr[c] for c in range(slabs)], axis=1)

    stats = [natural(s_ref) for s_ref in ss]
    mmax = jnp.maximum(jnp.maximum(stats[0], stats[1]), stats[2])
    wts = [jnp.exp2(s - mmax) for s in stats]
    den = jnp.zeros_like(mmax)
    for s, wgt in zip(stats, wts):
        den = den + pltpu.roll(wgt, N_HEADS, 1) * s
    inv = 1.0 / pltpu.roll(den, LANES - N_HEADS, 1)
    head_lane = lax.broadcasted_iota(jnp.int32, mmax.shape, 1) < N_HEADS
    attn = jnp.zeros((mmax.shape[0], ATTN_WIDTH), F32)
    for p, wgt in enumerate(wts):
        share = jnp.where(head_lane, wgt * inv, 0.0).astype(BF16)
        attn = attn + jnp.dot(share, em, preferred_element_type=F32) * natural(us[p])
    mod = mod_ref[...]
    a_n = _rms(attn) * an_ref[...]
    rec = _rms(o_ref[...]) * hn_ref[...] * og_ref[...].astype(F32)
    merged = jnp.concatenate([a_n, rec], axis=1).astype(BF16)
    mix = jnp.dot(merged, w_ref[...], preferred_element_type=F32)
    x1 = x_ref[...] + mod[2:3] * mix
    x1_ref[...] = x1
    h2 = _rms(x1) * (1.0 + mod[4:5]) + mod[3:4]
    h2_ref[...] = h2

    logits = _dot_hi(wr_ref[...], h2, _NT) + br_ref[...]
    e = jnp.exp(logits - jnp.max(logits, axis=0, keepdims=True))
    probs = e / jnp.sum(e, axis=0, keepdims=True)
    e1, e2, w1, w2 = _route(probs)
    eidx_ref[...] = jnp.concatenate([e1, e2], axis=0)
    tm = probs.shape[1]
    srow = lax.broadcasted_iota(jnp.int32, (LANES, tm), 0)
    w_t = jnp.where(srow == 0, w1, jnp.where(srow == 1, w2, 0.0))
    wcol_ref[...] = w_t.T


def _outproj(us, sts, o_h, og, x2d, mod_l, an_l, hn_l, w_out_l, em, wr_t, br, seq):
    t, d = x2d.shape
    tm = TOKEN_TILE
    per_batch = seq // tm
    row = lambda i: (i, 0)
    const = lambda i: (0, 0)
    half = pl.BlockSpec((tm, 512), row)
    stat = pl.BlockSpec((tm, LANES), row)
    full = pl.BlockSpec((tm, d), row)
    u_specs = [_stream_spec(tm, dil, per_batch, ATTN_WIDTH) for _, dil in DILATED_PATTERNS]
    s_specs = [_stream_spec(tm, dil, per_batch, LANES) for _, dil in DILATED_PATTERNS]
    return pl.pallas_call(
        _outproj_kernel,
        grid=(t // tm,),
        in_specs=u_specs + s_specs + [half, half, full,
                  pl.BlockSpec((None, 6, d), lambda i: (i // per_batch, 0, 0)),
                  pl.BlockSpec((1, 512), const), pl.BlockSpec((1, 512), const),
                  pl.BlockSpec((d, d), const),
                  pl.BlockSpec((LANES, 512), const),
                  pl.BlockSpec((N_EXPERTS, d), const), pl.BlockSpec((N_EXPERTS, 1), const)],
        out_specs=[full, full, pl.BlockSpec((N_ASSIGN, tm), lambda i: (0, i)), stat],
        out_shape=[jax.ShapeDtypeStruct((t, d), F32), jax.ShapeDtypeStruct((t, d), F32),
                   jax.ShapeDtypeStruct((N_ASSIGN, t), jnp.int32), jax.ShapeDtypeStruct((t, LANES), F32)],
        scratch_shapes=[pltpu.VMEM((ATTN_WIDTH // LANES, tm, LANES), F32), pltpu.VMEM((1, tm, LANES), F32)],
        compiler_params=_params("parallel"),
        name="outproj_route",
    )(*us, *sts, o_h, og, x2d, mod_l, an_l, hn_l, w_out_l, em, wr_t, br)


def _one_hots(e_ref):
    e = e_ref[...]
    erow = lax.broadcasted_iota(jnp.int32, (N_EXPERTS, e.shape[1]), 0)
    return (erow == e[0:1]).astype(F32), (erow == e[1:2]).astype(F32)


def _spread(col):
    return jnp.broadcast_to(col, (N_EXPERTS, LANES))


def _count_kernel(e_ref, cnt_ref):
    @pl.when(pl.program_id(0) == 0)
    def _():
        cnt_ref[...] = jnp.zeros_like(cnt_ref)

    oh0, oh1 = _one_hots(e_ref)
    cnt_ref[...] += _spread(jnp.sum(oh0 + oh1, axis=1, keepdims=True))


def _positions_kernel(e_ref, cnt_ref, pos_ref, meta_ref, carry_ref, offs_ref):
    i = pl.program_id(0)
    tp = e_ref.shape[1]
    oh0, oh1 = _one_hots(e_ref)

    @pl.when(i == 0)
    def _():
        cnt = cnt_ref[...]
        padded = jnp.floor((cnt + (MOE_TILE - 1)) * (1.0 / MOE_TILE)) * MOE_TILE
        run = jnp.zeros((1, LANES), F32)
        starts = []
        for ex in range(N_EXPERTS):
            starts.append(run)
            run = run + padded[ex:ex + 1]
        offs = jnp.concatenate(starts, axis=0)
        offs_ref[...] = offs
        carry_ref[...] = jnp.zeros_like(carry_ref)
        ends = offs + padded
        lane = lax.broadcasted_iota(jnp.int32, (N_EXPERTS, LANES), 1)
        srow = lax.broadcasted_iota(jnp.int32, (N_EXPERTS, LANES), 0)
        tile_start = (lane * MOE_TILE).astype(F32)
        tile_expert = jnp.sum((ends <= tile_start).astype(F32), axis=0, keepdims=True)
        tile_expert = jnp.minimum(tile_expert, N_EXPERTS - 1.0)
        on_diag = srow == lane
        ends_lane = jnp.sum(jnp.where(on_diag, ends, 0.0), axis=0, keepdims=True)
        pad_lane = jnp.sum(jnp.where(on_diag, padded, 0.0), axis=0, keepdims=True)
        meta = jnp.concatenate([tile_expert, run * (1.0 / MOE_TILE), ends_lane, pad_lane,
                                jnp.zeros((4, LANES), F32)], axis=0)
        meta_ref[...] = meta.astype(jnp.int32)

    r = lax.broadcasted_iota(jnp.int32, (tp, tp), 0)
    c = lax.broadcasted_iota(jnp.int32, (tp, tp), 1)
    upper = (r <= c).astype(BF16)
    oh = jnp.concatenate([oh0, oh1], axis=0).astype(BF16)
    pre = jnp.dot(oh, upper, preferred_element_type=F32)
    pre0, pre1 = pre[:N_EXPERTS], pre[N_EXPERTS:]
    tot0, tot1 = pre0[:, tp - 1:tp], pre1[:, tp - 1:tp]
    base = offs_ref[...][:, 0:1] + carry_ref[...][:, 0:1]
    p0 = jnp.sum(oh0 * (pre0 - 1.0 + base), axis=0, keepdims=True)
    p1 = jnp.sum(oh1 * (pre1 - 1.0 + (base + tot0)), axis=0, keepdims=True)
    pos_ref[...] = jnp.concatenate([p0, p1], axis=0).astype(jnp.int32)
    carry_ref[...] += _spread(tot0 + tot1)


def _positions(eidx):
    t = eidx.shape[1]
    tp = POS_TILE
    blk = pl.BlockSpec((N_ASSIGN, tp), lambda i: (0, i))
    whole = pl.BlockSpec((N_EXPERTS, LANES), lambda i: (0, 0))
    stat = pltpu.VMEM((N_EXPERTS, LANES), F32)
    cnt = pl.pallas_call(
        _count_kernel,
        grid=(t // tp,),
        in_specs=[blk],
        out_specs=whole,
        out_shape=jax.ShapeDtypeStruct((N_EXPERTS, LANES), F32),
        compiler_params=_params("arbitrary"),
        name="moe_count",
    )(eidx)
    return pl.pallas_call(
        _positions_kernel,
        grid=(t // tp,),
        in_specs=[blk, whole],
        out_specs=[blk, pl.BlockSpec((8, LANES), lambda i: (0, 0))],
        out_shape=[jax.ShapeDtypeStruct((N_ASSIGN, t), jnp.int32), jax.ShapeDtypeStruct((8, LANES), jnp.int32)],
        scratch_shapes=[stat, stat],
        compiler_params=_params("arbitrary"),
        name="moe_positions",
    )(eidx, cnt)


def _row(ref, r):
    return ref.at[r >> 3, pl.ds(r & (SUBLANES - 1), 1), :]


def _dispatch_kernel(ends_ref, pad_ref, used_ref, pos0_ref, pos1_ref, h_ref, xs_hbm, zero_ref, src_ref, sem, sems):
    i = pl.program_id(0)
    groups = h_ref.shape[0]
    tile_groups = MOE_TILE // SUBLANES

    def clear_tile(start):
        first = pl.multiple_of(start // SUBLANES, tile_groups)
        cp = pltpu.make_async_copy(zero_ref, xs_hbm.at[pl.ds(first, tile_groups)], sem)
        cp.start()
        cp.wait()

    @pl.when(i == 0)
    def _():
        zero_ref[...] = jnp.zeros_like(zero_ref)
        for ex in range(N_EXPERTS):
            @pl.when(pad_ref[ex] > 0)
            def _():
                clear_tile(ends_ref[ex] - MOE_TILE)

        def clear_tail(tile, carry):
            clear_tile(tile * MOE_TILE)
            return carry

        lax.fori_loop(used_ref[0], xs_hbm.shape[0] // tile_groups, clear_tail, 0)

    def scatter(slot):
        src_ref[slot] = h_ref[...]

        def issue(g, carry):
            for u in range(SUBLANES):
                src = src_ref.at[slot, g, pl.ds(u, 1), :]
                for pos_ref in (pos0_ref, pos1_ref):
                    pltpu.make_async_copy(src, _row(xs_hbm, pos_ref[g * SUBLANES + u]), sems.at[slot]).start()
            return carry

        lax.fori_loop(0, groups, issue, 0)

    def drain(slot):
        def wait(g, carry):
            for _ in range(SUBLANES * N_ASSIGN):
                pltpu.make_async_copy(src_ref.at[slot, 0, pl.ds(0, 1), :], _row(xs_hbm, 0), sems.at[slot]).wait()
            return carry

        lax.fori_loop(0, groups, wait, 0)

    for slot in range(2):
        @pl.when(i % 2 == slot)
        def _():
            scatter(slot)

            @pl.when(i > 0)
            def _():
                drain(1 - slot)

            @pl.when(i == pl.num_programs(0) - 1)
            def _():
                drain(slot)


def _dispatch(ends, padded, n_used, pos, h2):
    t, d = h2.shape
    tp = DMA_TILE
    rows = N_ASSIGN * t + N_EXPERTS * MOE_TILE
    slot = pl.BlockSpec((tp,), lambda i, *_: (i,), memory_space=pltpu.SMEM)
    xs = pl.pallas_call(
        _dispatch_kernel,
        grid_spec=pltpu.PrefetchScalarGridSpec(
            num_scalar_prefetch=3,
            grid=(t // tp,),
            in_specs=[slot, slot, pl.BlockSpec((tp // SUBLANES, SUBLANES, d), lambda i, *_: (i, 0, 0))],
            out_specs=pl.BlockSpec(memory_space=pl.ANY),
            scratch_shapes=[pltpu.VMEM((MOE_TILE // SUBLANES, SUBLANES, d), F32),
                            pltpu.VMEM((2, tp // SUBLANES, SUBLANES, d), F32),
                            pltpu.SemaphoreType.DMA(()), pltpu.SemaphoreType.DMA((2,))],
        ),
        out_shape=jax.ShapeDtypeStruct((rows // SUBLANES, SUBLANES, d), F32),
        compiler_params=_params("arbitrary"),
        name="moe_dispatch",
    )(ends, padded, n_used, pos[0], pos[1], h2.reshape(t // SUBLANES, SUBLANES, d))
    return xs.reshape(rows, d)


def _experts_kernel(te_ref, nv_ref, xs_ref, wg_ref, wu_ref, wd_ref, ys_ref, wgb_ref, wub_ref, wdb_ref):
    j = pl.program_id(0)
    used = nv_ref[0]
    jc = jnp.minimum(j, used - 1)
    new_expert = jnp.logical_or(j == 0, te_ref[jc] != te_ref[jnp.maximum(jc - 1, 0)])

    @pl.when(jnp.logical_and(j < used, new_expert))
    def _():
        wgb_ref[...] = wg_ref[...].astype(BF16)
        wub_ref[...] = wu_ref[...].astype(BF16)
        wdb_ref[...] = wd_ref[...].astype(BF16)

    @pl.when(j < used)
    def _():
        xb = xs_ref[...].astype(BF16)
        gt = jnp.dot(xb, wgb_ref[...], preferred_element_type=F32)
        up = jnp.dot(xb, wub_ref[...], preferred_element_type=F32)
        a = (gt * _sigmoid(gt) * up).astype(BF16)
        ys_ref[...] = jnp.dot(a, wdb_ref[...], preferred_element_type=F32)

    @pl.when(j >= used)
    def _():
        ys_ref[...] = jnp.zeros_like(ys_ref)


def _experts(tile_expert, n_used, xs, w_gate, w_up, w_down, layer):
    rows, d = xs.shape
    f = D_FF_EXPERT
    tile = lambda j, te, nv: (jnp.minimum(j, nv[0] - 1), 0)
    wsel = lambda j, te, nv: (layer, te[jnp.minimum(j, nv[0] - 1)], 0, 0)
    return pl.pallas_call(
        _experts_kernel,
        grid_spec=pltpu.PrefetchScalarGridSpec(
            num_scalar_prefetch=2,
            grid=(rows // MOE_TILE,),
            in_specs=[pl.BlockSpec((MOE_TILE, d), tile),
                      pl.BlockSpec((None, None, d, f), wsel), pl.BlockSpec((None, None, d, f), wsel),
                      pl.BlockSpec((None, None, f, d), wsel)],
            out_specs=pl.BlockSpec((MOE_TILE, d), lambda j, te, nv: (j, 0)),
            scratch_shapes=[pltpu.VMEM((d, f), BF16), pltpu.VMEM((d, f), BF16), pltpu.VMEM((f, d), BF16)],
        ),
        out_shape=jax.ShapeDtypeStruct((rows, d), F32),
        compiler_params=_params("arbitrary"),
        name="moe_experts",
    )(tile_expert, n_used, xs, w_gate, w_up, w_down)


def _combine_kernel(final, pos0_ref, pos1_ref, nxt0_ref, nxt1_ref, ys_hbm, wcol_ref, x_ref, mod_ref, fn_ref,
                    o_ref, buf_ref, sems):
    i = pl.program_id(0)
    tc, d = x_ref.shape
    groups = tc // SUBLANES

    def gather(into, p0_ref, p1_ref):
        def issue(g, carry):
            for u in range(SUBLANES):
                for k, pos_ref in enumerate((p0_ref, p1_ref)):
                    pltpu.make_async_copy(_row(ys_hbm, pos_ref[g * SUBLANES + u]),
                                          buf_ref.at[into, k, g, pl.ds(u, 1), :], sems.at[into]).start()
            return carry

        lax.fori_loop(0, groups, issue, 0)

    def consume(slot):
        def drain(g, carry):
            for _ in range(SUBLANES * N_ASSIGN):
                pltpu.make_async_copy(_row(ys_hbm, 0), buf_ref.at[slot, 0, 0, pl.ds(0, 1), :], sems.at[slot]).wait()
            return carry

        lax.fori_loop(0, groups, drain, 0)
        w = wcol_ref[...]
        ffn = w[:, 0:1] * buf_ref[slot, 0].reshape(tc, d) + w[:, 1:2] * buf_ref[slot, 1].reshape(tc, d)
        y = x_ref[...] + mod_ref[...][5:6] * ffn
        if final:
            y = _rms(y) * fn_ref[...]
        o_ref[...] = y

    @pl.when(i == 0)
    def _():
        gather(0, pos0_ref, pos1_ref)

    for slot in range(2):
        @pl.when(i % 2 == slot)
        def _():
            @pl.when(i + 1 < pl.num_programs(0))
            def _():
                gather(1 - slot, nxt0_ref, nxt1_ref)

            consume(slot)


def _combine(pos, ys, wcol, x1, mod_l, fnorm, seq, final):
    t, d = x1.shape
    tc = DMA_TILE
    per_batch = seq // tc
    row = lambda i: (i, 0)
    last = t // tc - 1
    slot = pl.BlockSpec((tc,), lambda i: (i,), memory_space=pltpu.SMEM)
    ahead = pl.BlockSpec((tc,), lambda i: (jnp.minimum(i + 1, last),), memory_space=pltpu.SMEM)
    return pl.pallas_call(
        functools.partial(_combine_kernel, final),
        grid=(t // tc,),
        in_specs=[slot, slot, ahead, ahead,
                  pl.BlockSpec(memory_space=pl.ANY),
                  pl.BlockSpec((tc, LANES), row), pl.BlockSpec((tc, d), row),
                  pl.BlockSpec((None, 6, d), lambda i: (i // per_batch, 0, 0)),
                  pl.BlockSpec((1, d), lambda i: (0, 0))],
        out_specs=pl.BlockSpec((tc, d), row),
        out_shape=jax.ShapeDtypeStruct((t, d), F32),
        scratch_shapes=[pltpu.VMEM((2, N_ASSIGN, tc // SUBLANES, SUBLANES, d), F32),
                        pltpu.SemaphoreType.DMA((2,))],
        compiler_params=_params("arbitrary"),
        name="moe_combine",
    )(pos[0], pos[1], pos[0], pos[1], ys.reshape(ys.shape[0] // SUBLANES, SUBLANES, d), wcol, x1, mod_l, fnorm)


def _moe(h2, eidx, wcol, w_gate, w_up, w_down, layer, x1, mod_l, fnorm, seq, final):
    pos, meta = _positions(eidx)
    xs = _dispatch(meta[2], meta[3], meta[1], pos, h2)
    ys = _experts(meta[0], meta[1], xs, w_gate, w_up, w_down, layer)
    return _combine(pos, ys, wcol, x1, mod_l, fnorm, seq, final)


def _head_expand():
    m = np.zeros((LANES, ATTN_WIDTH), np.float32)
    for h in range(N_HEADS):
        m[h, h * HEAD_DIM:(h + 1) * HEAD_DIM] = 1.0
    return jnp.asarray(m, BF16)


def kernel(x, c, positions, w_in, w_out, attn_norm, hgrn_norm, lb_params, ada_w, ada_b,
           w_router, b_router, w_gate, w_up, w_down, final_norm):
    batch, seq, d = x.shape
    t = batch * seq
    c_pad = jnp.pad(c, ((0, 8 - batch), (0, 0)))
    mod = _ada_mod(c_pad, ada_w, ada_b)[:, :batch].reshape(DEPTH, batch, 6, d)
    lbs = _lower_bounds(lb_params)
    cos, sin = _rope_tables(positions)
    cos, sin = cos.reshape(t, LANES), sin.reshape(t, LANES)
    head_id = np.arange(MXU_WIDTH) // HEAD_DIM
    ones_bd = jnp.asarray(head_id[:, None] == head_id[None, :], BF16)
    em = _head_expand()
    wr_t = w_router.T
    br = b_router.reshape(N_EXPERTS, 1)
    fnorm = final_norm.reshape(1, d)

    x2d = x.reshape(t, d)
    for l in range(DEPTH):
        outs = _inproj(x2d, mod[l], w_in[l].astype(BF16), cos, sin, lbs[l:l + 1], seq)
        qs, ks, vs = outs[0:3], outs[3:6], outs[6:9]
        qh, kh, gh, ih, og = outs[9:]
        us, sts = [], []
        for p, (window, dilation) in enumerate(DILATED_PATTERNS):
            u, st = _attn_pattern(qs[p], ks[p], vs[p], window, dilation)
            us.append(u)
            sts.append(st)
        o_h = _hgrn(qh, kh, gh, ih, ones_bd, batch, seq)
        x1, h2, eidx, wcol = _outproj(us, sts, o_h, og, x2d, mod[l], attn_norm[l:l + 1], hgrn_norm[l:l + 1],
                                      w_out[l].astype(BF16), em, wr_t, br, seq)
        x2d = _moe(h2, eidx, wcol, w_gate, w_up, w_down, l, x1, mod[l], fnorm, seq,
                   final=(l == DEPTH - 1))
    return x2d.reshape(batch, seq, d)
```

```python
import functools

import numpy as np
import jax
import jax.numpy as jnp
from jax import lax
from jax.experimental import pallas as pl
from jax.experimental.pallas import tpu as pltpu

D_MODEL = 1024
DEPTH = 2
ATTN_WIDTH = 512
HGRN_WIDTH = 512
HEAD_DIM = 64
N_HEADS = 8
DILATED_PATTERNS = ((128, 1), (512, 4), (2048, 16))
ATTN_BLOCK = 128
ATTN_STEP_BLOCKS = 4
ROPE_THETA = 10000.0
N_EXPERTS = 16
N_GROUPS = 4
EXPERTS_PER_GROUP = 4
D_FF_EXPERT = 512
RMS_EPS = 1e-6
IN_COLS = 3 * ATTN_WIDTH + 4 * HGRN_WIDTH

LANES = 128
SUBLANES = 8
MXU_WIDTH = 256
VMEM_LIMIT_BYTES = 56 * 1024 * 1024

TOKEN_TILE = 512
N_ASSIGN = 2
MOE_TILE = 512
POS_TILE = 1024
DMA_TILE = 512
HGRN_CHUNK = 64
HGRN_STEP_CHUNKS = 4
HGRN_SUB = 8
NEG_BIG = -1e30
LOG2E = 1.4426950408889634

F32 = jnp.float32
BF16 = jnp.bfloat16

_NT = (((1,), (1,)), ((), ()))
_TN = (((0,), (0,)), ((), ()))


def _params(*sem):
    return pltpu.CompilerParams(dimension_semantics=sem, vmem_limit_bytes=VMEM_LIMIT_BYTES)


def _sigmoid(x):
    return 0.5 * jnp.tanh(0.5 * x) + 0.5


def _rms(x):
    return x * lax.rsqrt(jnp.mean(x * x, axis=-1, keepdims=True) + RMS_EPS)


def _split3(a):
    a1 = a.astype(BF16)
    r1 = a - a1.astype(F32)
    a2 = r1.astype(BF16)
    a3 = (r1 - a2.astype(F32)).astype(BF16)
    return a1, a2, a3


def _dot_hi(a, b, dims):
    a1, a2, _ = _split3(a)
    b1, b2, _ = _split3(b)
    d = lambda p, q: lax.dot_general(p, q, dims, preferred_element_type=F32)
    return d(a1, b1) + (d(a2, b1) + d(a1, b2))


def _ada_kernel(c_ref, w_ref, b_ref, o_ref):
    c = c_ref[...]
    ca = c * _sigmoid(c)
    o_ref[...] = _dot_hi(ca, w_ref[...], (((1,), (0,)), ((), ()))) + b_ref[...]


def _ada_mod(c_pad, ada_w, ada_b):
    depth, d, n = ada_w.shape
    tn = 1536
    rows = c_pad.shape[0]
    return pl.pallas_call(
        _ada_kernel,
        grid=(depth, n // tn),
        in_specs=[
            pl.BlockSpec((rows, d), lambda l, j: (0, 0)),
            pl.BlockSpec((None, d, tn), lambda l, j: (l, 0, j)),
            pl.BlockSpec((None, 1, tn), lambda l, j: (l, 0, j)),
        ],
        out_specs=pl.BlockSpec((None, rows, tn), lambda l, j: (l, 0, j)),
        out_shape=jax.ShapeDtypeStruct((depth, rows, n), F32),
        compiler_params=_params("parallel", "parallel"),
        name="ada_mod",
    )(c_pad, ada_w, ada_b.reshape(depth, 1, n))


def _lb_kernel(p_ref, o_ref):
    p = p_ref[...]
    e = jnp.exp(p - jnp.max(p, axis=0, keepdims=True))
    sm = e / jnp.sum(e, axis=0, keepdims=True)
    run = jnp.zeros_like(sm[0:1])
    for l in range(p.shape[0]):
        run = run + sm[l:l + 1]
        o_ref[l:l + 1, :] = run - sm[0:1]


def _lower_bounds(lb_params):
    return pl.pallas_call(
        _lb_kernel,
        out_shape=jax.ShapeDtypeStruct(lb_params.shape, F32),
        name="lower_bounds",
    )(lb_params)


def _rope_kernel(pos_ref, inv_ref, cos_ref, sin_ref):
    ang = pos_ref[...].astype(F32) * inv_ref[...]
    lane = lax.broadcasted_iota(jnp.int32, ang.shape, 1)
    first = (lane % HEAD_DIM) < (HEAD_DIM // 2)
    s = jnp.sin(ang)
    cos_ref[...] = jnp.cos(ang)
    sin_ref[...] = jnp.where(first, -s, s)


def _rope_tables(positions):
    b, s = positions.shape
    ts = 1024
    half = HEAD_DIM // 2
    inv = ROPE_THETA ** (-jnp.arange(half, dtype=F32) / half)
    inv = jnp.tile(inv, LANES // half).reshape(1, LANES)
    out = jax.ShapeDtypeStruct((b, s, LANES), F32)
    return pl.pallas_call(
        _rope_kernel,
        grid=(b, s // ts),
        in_specs=[
            pl.BlockSpec((None, ts, 1), lambda i, j: (i, j, 0)),
            pl.BlockSpec((1, LANES), lambda i, j: (0, 0)),
        ],
        out_specs=[pl.BlockSpec((None, ts, LANES), lambda i, j: (i, j, 0))] * 2,
        out_shape=[out, out],
        compiler_params=_params("parallel", "parallel"),
        name="rope_tables",
    )(positions.reshape(b, s, 1), inv)


def _store_streams(val, refs):
    tm = val.shape[0]
    for (_, d), ref in zip(DILATED_PATTERNS, refs):
        if d == 1:
            ref[0] = val.astype(BF16)
        else:
            sw = jnp.swapaxes(val.reshape(tm // d, d, val.shape[1]), 0, 1)
            ref[...] = sw.astype(BF16)


def _inproj_kernel(x_ref, mod_ref, w_ref, cos_ref, sin_ref, lb_ref,
                   q1_ref, q4_ref, q16_ref, k1_ref, k4_ref, k16_ref, v1_ref, v4_ref, v16_ref,
                   qh_ref, kh_ref, gh_ref, ih_ref, og_ref):
    x = x_ref[...]
    mod = mod_ref[...]
    h = _rms(x) * (1.0 + mod[1:2]) + mod[0:1]
    hb = h.astype(BF16)
    reps = ATTN_WIDTH // LANES
    cos = jnp.concatenate([cos_ref[...]] * reps, axis=1)
    sin = jnp.concatenate([sin_ref[...]] * reps, axis=1)
    lane = lax.broadcasted_iota(jnp.int32, cos.shape, 1)
    first = (lane % HEAD_DIM) < (HEAD_DIM // 2)
    half = HEAD_DIM // 2

    def proj(j):
        return jnp.dot(hb, w_ref[:, j * 512:(j + 1) * 512], preferred_element_type=F32)

    def rot(t):
        swapped = jnp.where(first, pltpu.roll(t, ATTN_WIDTH - half, 1), pltpu.roll(t, half, 1))
        return t * cos + swapped * sin

    ps = [proj(j) for j in range(IN_COLS // 512)]
    _store_streams(rot(ps[0]) * (HEAD_DIM ** -0.5 * LOG2E), (q1_ref, q4_ref, q16_ref))
    _store_streams(rot(ps[1]), (k1_ref, k4_ref, k16_ref))
    _store_streams(ps[2], (v1_ref, v4_ref, v16_ref))
    qh_ref[...] = (ps[3] * _sigmoid(ps[3])).astype(BF16)
    lb = lb_ref[...]
    f = lb + (1.0 - lb) / (1.0 + jnp.exp(-ps[4]))
    kh_ref[...] = (1.0 - f).astype(BF16)
    gh_ref[...] = jnp.log(f)
    ih_ref[...] = ps[5].astype(BF16)
    og_ref[...] = _sigmoid(ps[6]).astype(BF16)


def _stream_spec(tm, dil, per_batch, width):
    return pl.BlockSpec((None, dil, tm // dil, width), lambda i: (i // per_batch, 0, i % per_batch, 0))


def _inproj(x2d, mod_l, w_in_l, cos, sin, lb_l, seq):
    t, d = x2d.shape
    tm = TOKEN_TILE
    per_batch = seq // tm
    batch = t // seq
    row = lambda i: (i, 0)
    half_spec = pl.BlockSpec((tm, 512), row)
    bf = jax.ShapeDtypeStruct((t, 512), BF16)
    stream_specs = [_stream_spec(tm, dil, per_batch, ATTN_WIDTH) for _, dil in DILATED_PATTERNS]
    stream_shapes = [jax.ShapeDtypeStruct((batch, dil, seq // dil, ATTN_WIDTH), BF16)
                     for _, dil in DILATED_PATTERNS]
    return pl.pallas_call(
        _inproj_kernel,
        grid=(t // tm,),
        in_specs=[
            pl.BlockSpec((tm, d), row),
            pl.BlockSpec((None, 6, d), lambda i: (i // per_batch, 0, 0)),
            pl.BlockSpec((d, IN_COLS), lambda i: (0, 0)),
            pl.BlockSpec((tm, LANES), row),
            pl.BlockSpec((tm, LANES), row),
            pl.BlockSpec((1, 512), lambda i: (0, 0)),
        ],
        out_specs=stream_specs * 3 + [half_spec] * 5,
        out_shape=stream_shapes * 3 + [bf, bf, jax.ShapeDtypeStruct((t, 512), F32), bf, bf],
        compiler_params=_params("parallel"),
        name="inproj",
    )(x2d, mod_l, w_in_l, cos, sin, lb_l)


def _attn_kernel(q_ref, kc_ref, kp_ref, vc_ref, vp_ref, bias_ref, u_ref, st_ref):
    blk = ATTN_BLOCK
    n = pl.program_id(2)
    keys = jnp.concatenate([kp_ref[...], kc_ref[...]], axis=0)
    vals = jnp.concatenate([vp_ref[...], vc_ref[...]], axis=0)
    biases = [bias_ref[jnp.minimum(n, 1)]] + [bias_ref[1]] * (ATTN_STEP_BLOCKS - 1)
    lane = lax.broadcasted_iota(jnp.int32, (blk, LANES), 1)
    low = lane < HEAD_DIM
    zero = jnp.zeros((), BF16)
    ones = jnp.ones((2 * blk, LANES), BF16)
    slabs = [slice(pair * LANES, (pair + 1) * LANES) for pair in range(N_HEADS // 2)]
    work = [(j, pair) for j in range(ATTN_STEP_BLOCKS) for pair in range(N_HEADS // 2)]

    scores = []
    for j, pair in work:
        qp = q_ref[j * blk:(j + 1) * blk, slabs[pair]]
        qst = jnp.concatenate([jnp.where(low, qp, zero), jnp.where(low, zero, qp)], axis=0)
        kwin = keys[j * blk:(j + 2) * blk, slabs[pair]]
        scores.append(lax.dot_general(qst, kwin, _NT, preferred_element_type=F32) + biases[j])
    maxes, probs = [], []
    for s in scores:
        m = jnp.max(s, axis=1, keepdims=True).astype(BF16).astype(F32)
        maxes.append(m)
        probs.append(jnp.exp2((s - m).astype(BF16)))
    stats = [jnp.zeros((blk, LANES), F32) for _ in range(ATTN_STEP_BLOCKS)]
    for (j, pair), m, p in zip(work, maxes, probs):
        vwin = vals[j * blk:(j + 2) * blk, slabs[pair]]
        ul = jnp.dot(p, jnp.concatenate([vwin, ones], axis=1), preferred_element_type=F32)
        u, l = ul[:, :LANES], ul[:, LANES:]
        for hh in range(2):
            head = 2 * pair + hh
            stats[j] = jnp.where(lane == head, m[hh * blk:(hh + 1) * blk], stats[j])
            stats[j] = jnp.where(lane == N_HEADS + head, l[hh * blk:(hh + 1) * blk], stats[j])
        u_ref[j * blk:(j + 1) * blk, slabs[pair]] = jnp.where(low, u[:blk], u[blk:]).astype(BF16)
    for j in range(ATTN_STEP_BLOCKS):
        st_ref[j * blk:(j + 1) * blk, :] = stats[j]


def _band_bias(steps):
    blk = ATTN_BLOCK
    qi = np.arange(2 * blk)[:, None] % blk
    kj = np.arange(2 * blk)[None, :]
    dist = qi + blk - kj
    band = (dist >= 0) & (dist <= steps)
    first = band & (kj >= blk)
    return jnp.asarray(np.where(np.stack([first, band]), 0.0, NEG_BIG), F32)


def _attn_pattern(qs, ks, vs, window, dilation):
    batch, _, m, _ = qs.shape
    rows = ATTN_STEP_BLOCKS * ATTN_BLOCK
    nb = m // rows
    bias = _band_bias(window // dilation)
    cur = pl.BlockSpec((None, None, rows, ATTN_WIDTH), lambda b, r, n: (b, r, n, 0))
    prev = pl.BlockSpec((None, None, ATTN_BLOCK, ATTN_WIDTH),
                        lambda b, r, n: (b, r, jnp.maximum(n * ATTN_STEP_BLOCKS - 1, 0), 0))
    return pl.pallas_call(
        _attn_kernel,
        grid=(batch, dilation, nb),
        in_specs=[cur, cur, prev, cur, prev,
                  pl.BlockSpec(bias.shape, lambda b, r, n: (0, 0, 0))],
        out_specs=[cur, pl.BlockSpec((None, None, rows, LANES), lambda b, r, n: (b, r, n, 0))],
        out_shape=[jax.ShapeDtypeStruct((batch, dilation, m, ATTN_WIDTH), BF16),
                   jax.ShapeDtypeStruct((batch, dilation, m, LANES), F32)],
        compiler_params=_params("parallel", "parallel", "arbitrary"),
        name=f"attn_d{dilation}",
    )(qs, ks, ks, vs, vs, bias)


def _hgrn_kernel(q_ref, k_ref, g_ref, v_ref, ones_ref, o_ref, st_ref):
    @pl.when(pl.program_id(1) == 0)
    def _():
        st_ref[...] = jnp.zeros_like(st_ref)

    state = [st_ref[grp] for grp in range(st_ref.shape[0])]
    for chunk in range(HGRN_STEP_CHUNKS):
        state = _hgrn_chunk(chunk * HGRN_CHUNK, q_ref, k_ref, g_ref, v_ref, ones_ref, o_ref, state)
    for grp, st in enumerate(state):
        st_ref[grp] = st


def _hgrn_chunk(off, q_ref, k_ref, g_ref, v_ref, ones_ref, o_ref, state):
    c, sub, w = HGRN_CHUNK, HGRN_SUB, HGRN_WIDTH
    span = slice(off, off + c)
    q = q_ref[span, :].astype(F32)
    k = k_ref[span, :].astype(F32)
    vb = v_ref[span, :]
    v = vb.astype(F32)
    g = g_ref[span, :]

    ri = lax.broadcasted_iota(jnp.int32, (c, c), 0)
    ci = lax.broadcasted_iota(jnp.int32, (c, c), 1)
    tri = (ci <= ri).astype(BF16)
    g1, g2, g3 = _split3(g)
    cs = lambda t: jnp.dot(tri, t, preferred_element_type=F32)
    b = (cs(g1) + (cs(g2) + cs(g3))) * LOG2E
    b_last = b[c - 1:c, :]

    hw = state[0].shape[0]
    qd = (q * jnp.exp2(b)).astype(BF16)
    kt = (k * jnp.exp2(b_last - b)).astype(BF16)
    decay = jnp.exp2(b_last)
    r0i = lax.broadcasted_iota(jnp.int32, (hw, hw), 0) // HEAD_DIM
    c0i = lax.broadcasted_iota(jnp.int32, (hw, hw), 1) // HEAD_DIM
    same_head = r0i == c0i
    o_parts, new_state = [], []
    for grp, st in enumerate(state):
        sl = slice(grp * hw, (grp + 1) * hw)
        o_parts.append(lax.dot_general(qd[:, sl], st.astype(BF16), _NT, preferred_element_type=F32))
        upd = lax.dot_general(vb[:, sl], kt[:, sl], _TN, preferred_element_type=F32)
        new_state.append(st * decay[:, sl] + jnp.where(same_head, upd, 0.0))
    o_inter = jnp.concatenate(o_parts, axis=1)

    gheads = hw // HEAD_DIM
    groups = [slice(grp * hw, (grp + 1) * hw) for grp in range(w // hw)]
    hrow = lax.broadcasted_iota(jnp.int32, (gheads * sub, hw), 0) // sub
    hlane = lax.broadcasted_iota(jnp.int32, (gheads * sub, hw), 1) // HEAD_DIM
    hmask = hrow == hlane
    trow = lax.broadcasted_iota(jnp.int32, (sub, w), 0)
    ones = ones_ref[...]
    ow = ones.shape[0]

    nblk = c // sub
    ws = []
    for blk in range(nblk):
        rows = slice(blk * sub, (blk + 1) * sub)
        bi, qi, ki = b[rows], q[rows], k[rows]
        for s in range(sub):
            e = jnp.exp2(bi - bi[s:s + 1])
            ws.append(jnp.where(trow >= s, qi * (ki[s:s + 1] * e), 0.0))
    wcat = jnp.concatenate(ws, axis=0).astype(BF16)
    sc = jnp.concatenate(
        [jnp.dot(wcat[:, j * ow:(j + 1) * ow], ones, preferred_element_type=F32) for j in range(w // ow)],
        axis=1)

    inter_scores = [None]
    for blk in range(1, nblk):
        r0 = blk * sub
        rows = slice(r0, r0 + sub)
        bref = b[r0:r0 + 1]
        qs = q[rows] * jnp.exp2(b[rows] - bref)
        kp = (k[0:r0] * jnp.exp2(bref - b[0:r0])).astype(BF16)
        per_group = []
        for sl in groups:
            qexp = jnp.where(hmask, jnp.concatenate([qs[:, sl]] * gheads, axis=0), 0.0).astype(BF16)
            per_group.append(lax.dot_general(qexp, kp[:, sl], _NT, preferred_element_type=F32))
        inter_scores.append(per_group)

    for blk in range(nblk):
        r0 = blk * sub
        rows = slice(r0, r0 + sub)
        acc = o_inter[rows]
        if blk > 0:
            parts = []
            for sl, a in zip(groups, inter_scores[blk]):
                oexp = jnp.dot(a.astype(BF16), vb[0:r0, sl], preferred_element_type=F32)
                oexp = jnp.where(hmask, oexp, 0.0)
                part = oexp[0:sub]
                for hd in range(1, gheads):
                    part = part + oexp[hd * sub:(hd + 1) * sub]
                parts.append(part)
            acc = acc + jnp.concatenate(parts, axis=1)
        vi = v[rows]
        for s in range(sub):
            at = (blk * sub + s) * sub
            acc = acc + sc[at:at + sub] * vi[s:s + 1]
        o_ref[off + r0:off + r0 + sub, :] = acc
    return new_state


def _hgrn(qh, kh, gh, ih, ones_bd, batch, seq):
    c, w = HGRN_CHUNK, HGRN_WIDTH
    view = lambda a: a.reshape(batch, seq, w)
    rows = HGRN_STEP_CHUNKS * c
    blk = pl.BlockSpec((None, rows, w), lambda b, n: (b, n, 0))
    o = pl.pallas_call(
        _hgrn_kernel,
        grid=(batch, seq // rows),
        in_specs=[blk, blk, blk, blk, pl.BlockSpec(ones_bd.shape, lambda b, n: (0, 0))],
        out_specs=blk,
        out_shape=jax.ShapeDtypeStruct((batch, seq, w), F32),
        scratch_shapes=[pltpu.VMEM((w // MXU_WIDTH, MXU_WIDTH, MXU_WIDTH), F32)],
        compiler_params=_params("parallel", "arbitrary"),
        name="hgrn2",
    )(view(qh), view(kh), view(gh), view(ih), ones_bd)
    return o.reshape(batch * seq, w)


def _route(probs):
    rows = [probs[i:i + 1, :] for i in range(N_EXPERTS)]
    gsum = []
    for gidx in range(N_GROUPS):
        a, b_, c_, d_ = rows[4 * gidx:4 * gidx + 4]
        hi1, lo1 = jnp.maximum(a, b_), jnp.minimum(a, b_)
        hi2, lo2 = jnp.maximum(c_, d_), jnp.minimum(c_, d_)
        top1 = jnp.maximum(hi1, hi2)
        second = jnp.maximum(jnp.minimum(hi1, hi2), jnp.maximum(lo1, lo2))
        gsum.append(top1 + second)
    best, gi = gsum[0], jnp.zeros(gsum[0].shape, jnp.int32)
    for gidx in range(1, N_GROUPS):
        upd = gsum[gidx] > best
        best = jnp.where(upd, gsum[gidx], best)
        gi = jnp.where(upd, gidx, gi)
    vals = []
    for j in range(EXPERTS_PER_GROUP):
        vj = rows[j]
        for gidx in range(1, N_GROUPS):
            vj = jnp.where(gi == gidx, rows[4 * gidx + j], vj)
        vals.append(vj)
    v1, i1 = vals[0], jnp.zeros(gi.shape, jnp.int32)
    for j in range(1, EXPERTS_PER_GROUP):
        upd = vals[j] > v1
        v1 = jnp.where(upd, vals[j], v1)
        i1 = jnp.where(upd, j, i1)
    v2, i2 = jnp.full(v1.shape, -1.0, F32), jnp.zeros(gi.shape, jnp.int32)
    for j in range(EXPERTS_PER_GROUP):
        upd = (i1 != j) & (vals[j] > v2)
        v2 = jnp.where(upd, vals[j], v2)
        i2 = jnp.where(upd, j, i2)
    tot = v1 + v2
    base = gi * EXPERTS_PER_GROUP
    return base + i1, base + i2, v1 / tot, v2 / tot


def _outproj_kernel(u1_ref, u2_ref, u3_ref, s1_ref, s2_ref, s3_ref, o_ref, og_ref, x_ref, mod_ref,
                    an_ref, hn_ref, w_ref, em_ref, wr_ref, br_ref,
                    x1_ref, h2_ref, eidx_ref, wcol_ref, cnt_ref, uscr_ref, sscr_ref):
    us = (u1_ref, u2_ref, u3_ref)
    ss = (s1_ref, s2_ref, s3_ref)
    em = em_ref[...]

    def natural(ref):
        dil, per, width = ref.shape
        if dil == 1:
            return ref[0].astype(F32)
        scr = uscr_ref if width == ATTN_WIDTH else sscr_ref
        slabs = scr.shape[0]
        for r in range(dil):
            val = ref[r].astype(F32)
            for c in range(slabs):
                scr[c, pl.ds(r, per, stride=dil), :] = val[:, c * LANES:(c + 1) * LANES]
        return jnp.concatenate([scr[c] for c in range(slabs)], axis=1)

    stats = [natural(s_ref) for s_ref in ss]
    mmax = jnp.maximum(jnp.maximum(stats[0], stats[1]), stats[2])
    wts = [jnp.exp2(s - mmax) for s in stats]
    den = jnp.zeros_like(mmax)
    for s, wgt in zip(stats, wts):
        den = den + pltpu.roll(wgt, N_HEADS, 1) * s
    inv = 1.0 / pltpu.roll(den, LANES - N_HEADS, 1)
    head_lane = lax.broadcasted_iota(jnp.int32, mmax.shape, 1) < N_HEADS
    attn = jnp.zeros((mmax.shape[0], ATTN_WIDTH), F32)
    for p, wgt in enumerate(wts):
        share = jnp.where(head_lane, wgt * inv, 0.0).astype(BF16)
        attn = attn + jnp.dot(share, em, preferred_element_type=F32) * natural(us[p])
    mod = mod_ref[...]
    a_n = _rms(attn) * an_ref[...]
    rec = _rms(o_ref[...]) * hn_ref[...] * og_ref[...].astype(F32)
    merged = jnp.concatenate([a_n, rec], axis=1).astype(BF16)
    mix = jnp.dot(merged, w_ref[...], preferred_element_type=F32)
    x1 = x_ref[...] + mod[2:3] * mix
    x1_ref[...] = x1
    h2 = _rms(x1) * (1.0 + mod[4:5]) + mod[3:4]
    h2_ref[...] = h2

    logits = _dot_hi(wr_ref[...], h2, _NT) + br_ref[...]
    e = jnp.exp(logits - jnp.max(logits, axis=0, keepdims=True))
    probs = e / jnp.sum(e, axis=0, keepdims=True)
    e1, e2, w1, w2 = _route(probs)
    eidx_ref[...] = jnp.concatenate([e1, e2], axis=0)
    erow = lax.broadcasted_iota(jnp.int32, probs.shape, 0)
    chosen = (erow == e1).astype(F32) + (erow == e2).astype(F32)
    cnt_ref[...] = _spread(jnp.sum(chosen, axis=1, keepdims=True))
    tm = probs.shape[1]
    srow = lax.broadcasted_iota(jnp.int32, (LANES, tm), 0)
    w_t = jnp.where(srow == 0, w1, jnp.where(srow == 1, w2, 0.0))
    wcol_ref[...] = w_t.T


def _outproj(us, sts, o_h, og, x2d, mod_l, an_l, hn_l, w_out_l, em, wr_t, br, seq):
    t, d = x2d.shape
    tm = TOKEN_TILE
    per_batch = seq // tm
    row = lambda i: (i, 0)
    const = lambda i: (0, 0)
    half = pl.BlockSpec((tm, 512), row)
    stat = pl.BlockSpec((tm, LANES), row)
    full = pl.BlockSpec((tm, d), row)
    u_specs = [_stream_spec(tm, dil, per_batch, ATTN_WIDTH) for _, dil in DILATED_PATTERNS]
    s_specs = [_stream_spec(tm, dil, per_batch, LANES) for _, dil in DILATED_PATTERNS]
    return pl.pallas_call(
        _outproj_kernel,
        grid=(t // tm,),
        in_specs=u_specs + s_specs + [half, half, full,
                  pl.BlockSpec((None, 6, d), lambda i: (i // per_batch, 0, 0)),
                  pl.BlockSpec((1, 512), const), pl.BlockSpec((1, 512), const),
                  pl.BlockSpec((d, d), const),
                  pl.BlockSpec((LANES, 512), const),
                  pl.BlockSpec((N_EXPERTS, d), const), pl.BlockSpec((N_EXPERTS, 1), const)],
        out_specs=[full, full, pl.BlockSpec((N_ASSIGN, tm), lambda i: (0, i)), stat,
                   pl.BlockSpec((None, N_EXPERTS, LANES), lambda i: (i, 0, 0))],
        out_shape=[jax.ShapeDtypeStruct((t, d), F32), jax.ShapeDtypeStruct((t, d), F32),
                   jax.ShapeDtypeStruct((N_ASSIGN, t), jnp.int32), jax.ShapeDtypeStruct((t, LANES), F32),
                   jax.ShapeDtypeStruct((t // tm, N_EXPERTS, LANES), F32)],
        scratch_shapes=[pltpu.VMEM((ATTN_WIDTH // LANES, tm, LANES), F32), pltpu.VMEM((1, tm, LANES), F32)],
        compiler_params=_params("parallel"),
        name="outproj_route",
    )(*us, *sts, o_h, og, x2d, mod_l, an_l, hn_l, w_out_l, em, wr_t, br)


def _one_hots(e_ref):
    e = e_ref[...]
    erow = lax.broadcasted_iota(jnp.int32, (N_EXPERTS, e.shape[1]), 0)
    return (erow == e[0:1]).astype(F32), (erow == e[1:2]).astype(F32)


def _spread(col):
    return jnp.broadcast_to(col, (N_EXPERTS, LANES))


def _positions_kernel(e_ref, cnt_ref, pos_ref, meta_ref, carry_ref, offs_ref):
    i = pl.program_id(0)
    tp = e_ref.shape[1]
    oh0, oh1 = _one_hots(e_ref)

    @pl.when(i == 0)
    def _():
        cnt = jnp.sum(cnt_ref[...], axis=0)
        padded = jnp.floor((cnt + (MOE_TILE - 1)) * (1.0 / MOE_TILE)) * MOE_TILE
        run = jnp.zeros((1, LANES), F32)
        starts = []
        for ex in range(N_EXPERTS):
            starts.append(run)
            run = run + padded[ex:ex + 1]
        offs = jnp.concatenate(starts, axis=0)
        offs_ref[...] = offs
        carry_ref[...] = jnp.zeros_like(carry_ref)
        ends = offs + padded
        lane = lax.broadcasted_iota(jnp.int32, (N_EXPERTS, LANES), 1)
        srow = lax.broadcasted_iota(jnp.int32, (N_EXPERTS, LANES), 0)
        tile_start = (lane * MOE_TILE).astype(F32)
        tile_expert = jnp.sum((ends <= tile_start).astype(F32), axis=0, keepdims=True)
        tile_expert = jnp.minimum(tile_expert, N_EXPERTS - 1.0)
        on_diag = srow == lane
        ends_lane = jnp.sum(jnp.where(on_diag, ends, 0.0), axis=0, keepdims=True)
        pad_lane = jnp.sum(jnp.where(on_diag, padded, 0.0), axis=0, keepdims=True)
        meta = jnp.concatenate([tile_expert, run * (1.0 / MOE_TILE), ends_lane, pad_lane,
                                jnp.zeros((4, LANES), F32)], axis=0)
        meta_ref[...] = meta.astype(jnp.int32)

    r = lax.broadcasted_iota(jnp.int32, (tp, tp), 0)
    c = lax.broadcasted_iota(jnp.int32, (tp, tp), 1)
    upper = (r <= c).astype(BF16)
    oh = jnp.concatenate([oh0, oh1], axis=0).astype(BF16)
    pre = jnp.dot(oh, upper, preferred_element_type=F32)
    pre0, pre1 = pre[:N_EXPERTS], pre[N_EXPERTS:]
    tot0, tot1 = pre0[:, tp - 1:tp], pre1[:, tp - 1:tp]
    base = offs_ref[...][:, 0:1] + carry_ref[...][:, 0:1]
    p0 = jnp.sum(oh0 * (pre0 - 1.0 + base), axis=0, keepdims=True)
    p1 = jnp.sum(oh1 * (pre1 - 1.0 + (base + tot0)), axis=0, keepdims=True)
    pos_ref[...] = jnp.concatenate([p0, p1], axis=0).astype(jnp.int32)
    carry_ref[...] += _spread(tot0 + tot1)


def _positions(eidx, cnt):
    t = eidx.shape[1]
    tp = POS_TILE
    blk = pl.BlockSpec((N_ASSIGN, tp), lambda i: (0, i))
    whole = pl.BlockSpec(cnt.shape, lambda i: (0, 0, 0))
    stat = pltpu.VMEM((N_EXPERTS, LANES), F32)
    return pl.pallas_call(
        _positions_kernel,
        grid=(t // tp,),
        in_specs=[blk, whole],
        out_specs=[blk, pl.BlockSpec((8, LANES), lambda i: (0, 0))],
        out_shape=[jax.ShapeDtypeStruct((N_ASSIGN, t), jnp.int32), jax.ShapeDtypeStruct((8, LANES), jnp.int32)],
        scratch_shapes=[stat, stat],
        compiler_params=_params("arbitrary"),
        name="moe_positions",
    )(eidx, cnt)


def _row(ref, r):
    return ref.at[r >> 3, pl.ds(r & (SUBLANES - 1), 1), :]


def _dispatch_kernel(ends_ref, pad_ref, used_ref, pos0_ref, pos1_ref, h_ref, xs_hbm, zero_ref, src_ref, sem, sems):
    i = pl.program_id(0)
    groups = h_ref.shape[0]
    tile_groups = MOE_TILE // SUBLANES

    def clear_copy(start):
        first = pl.multiple_of(start // SUBLANES, tile_groups)
        return pltpu.make_async_copy(zero_ref, xs_hbm.at[pl.ds(first, tile_groups)], sem)

    @pl.when(i == 0)
    def _():
        zero_ref[...] = jnp.zeros_like(zero_ref)
        n_tiles = xs_hbm.shape[0] // tile_groups
        for act in ("start", "wait"):
            for ex in range(N_EXPERTS):
                @pl.when(pad_ref[ex] > 0)
                def _():
                    getattr(clear_copy(ends_ref[ex] - MOE_TILE), act)()

            def clear_tail(tile, carry):
                getattr(clear_copy(tile * MOE_TILE), act)()
                return carry

            lax.fori_loop(used_ref[0], n_tiles, clear_tail, 0)

    def scatter(slot):
        src_ref[slot] = h_ref[...]

        def issue(g, carry):
            for u in range(SUBLANES):
                src = src_ref.at[slot, g, pl.ds(u, 1), :]
                for pos_ref in (pos0_ref, pos1_ref):
                    pltpu.make_async_copy(src, _row(xs_hbm, pos_ref[g * SUBLANES + u]), sems.at[slot]).start()
            return carry

        lax.fori_loop(0, groups, issue, 0)

    def drain(slot):
        def wait(g, carry):
            for _ in range(SUBLANES * N_ASSIGN):
                pltpu.make_async_copy(src_ref.at[slot, 0, pl.ds(0, 1), :], _row(xs_hbm, 0), sems.at[slot]).wait()
            return carry

        lax.fori_loop(0, groups, wait, 0)

    for slot in range(2):
        @pl.when(i % 2 == slot)
        def _():
            scatter(slot)

            @pl.when(i > 0)
            def _():
                drain(1 - slot)

            @pl.when(i == pl.num_programs(0) - 1)
            def _():
                drain(slot)


def _dispatch(ends, padded, n_used, pos, h2):
    t, d = h2.shape
    tp = DMA_TILE
    rows = N_ASSIGN * t + N_EXPERTS * MOE_TILE
    slot = pl.BlockSpec((tp,), lambda i, *_: (i,), memory_space=pltpu.SMEM)
    xs = pl.pallas_call(
        _dispatch_kernel,
        grid_spec=pltpu.PrefetchScalarGridSpec(
            num_scalar_prefetch=3,
            grid=(t // tp,),
            in_specs=[slot, slot, pl.BlockSpec((tp // SUBLANES, SUBLANES, d), lambda i, *_: (i, 0, 0))],
            out_specs=pl.BlockSpec(memory_space=pl.ANY),
            scratch_shapes=[pltpu.VMEM((MOE_TILE // SUBLANES, SUBLANES, d), F32),
                            pltpu.VMEM((2, tp // SUBLANES, SUBLANES, d), F32),
                            pltpu.SemaphoreType.DMA(()), pltpu.SemaphoreType.DMA((2,))],
        ),
        out_shape=jax.ShapeDtypeStruct((rows // SUBLANES, SUBLANES, d), F32),
        compiler_params=_params("arbitrary"),
        name="moe_dispatch",
    )(ends, padded, n_used, pos[0], pos[1], h2.reshape(t // SUBLANES, SUBLANES, d))
    return xs.reshape(rows, d)


def _experts_kernel(te_ref, nv_ref, xs_ref, wg_ref, wu_ref, wd_ref, ys_ref, wgb_ref, wub_ref, wdb_ref):
    j = pl.program_id(0)
    used = nv_ref[0]
    jc = jnp.minimum(j, used - 1)
    new_expert = jnp.logical_or(j == 0, te_ref[jc] != te_ref[jnp.maximum(jc - 1, 0)])

    @pl.when(jnp.logical_and(j < used, new_expert))
    def _():
        wgb_ref[...] = wg_ref[...].astype(BF16)
        wub_ref[...] = wu_ref[...].astype(BF16)
        wdb_ref[...] = wd_ref[...].astype(BF16)

    @pl.when(j < used)
    def _():
        xb = xs_ref[...].astype(BF16)
        gt = jnp.dot(xb, wgb_ref[...], preferred_element_type=F32)
        up = jnp.dot(xb, wub_ref[...], preferred_element_type=F32)
        a = (gt * _sigmoid(gt) * up).astype(BF16)
        ys_ref[...] = jnp.dot(a, wdb_ref[...], preferred_element_type=F32)

    @pl.when(j >= used)
    def _():
        ys_ref[...] = jnp.zeros_like(ys_ref)


def _experts(tile_expert, n_used, xs, w_gate, w_up, w_down, layer):
    rows, d = xs.shape
    f = D_FF_EXPERT
    tile = lambda j, te, nv: (jnp.minimum(j, nv[0] - 1), 0)
    wsel = lambda j, te, nv: (layer, te[jnp.minimum(j, nv[0] - 1)], 0, 0)
    return pl.pallas_call(
        _experts_kernel,
        grid_spec=pltpu.PrefetchScalarGridSpec(
            num_scalar_prefetch=2,
            grid=(rows // MOE_TILE,),
            in_specs=[pl.BlockSpec((MOE_TILE, d), tile),
                      pl.BlockSpec((None, None, d, f), wsel), pl.BlockSpec((None, None, d, f), wsel),
                      pl.BlockSpec((None, None, f, d), wsel)],
            out_specs=pl.BlockSpec((MOE_TILE, d), lambda j, te, nv: (j, 0)),
            scratch_shapes=[pltpu.VMEM((d, f), BF16), pltpu.VMEM((d, f), BF16), pltpu.VMEM((f, d), BF16)],
        ),
        out_shape=jax.ShapeDtypeStruct((rows, d), F32),
        compiler_params=_params("arbitrary"),
        name="moe_experts",
    )(tile_expert, n_used, xs, w_gate, w_up, w_down)


def _combine_kernel(final, pos0_ref, pos1_ref, nxt0_ref, nxt1_ref, ys_hbm, wcol_ref, x_ref, mod_ref, fn_ref,
                    o_ref, buf_ref, sems):
    i = pl.program_id(0)
    tc, d = x_ref.shape
    groups = tc // SUBLANES

    def gather(into, p0_ref, p1_ref):
        def issue(g, carry):
            for u in range(SUBLANES):
                for k, pos_ref in enumerate((p0_ref, p1_ref)):
                    pltpu.make_async_copy(_row(ys_hbm, pos_ref[g * SUBLANES + u]),
                                          buf_ref.at[into, k, g, pl.ds(u, 1), :], sems.at[into]).start()
            return carry

        lax.fori_loop(0, groups, issue, 0)

    def consume(slot):
        def drain(g, carry):
            for _ in range(SUBLANES * N_ASSIGN):
                pltpu.make_async_copy(_row(ys_hbm, 0), buf_ref.at[slot, 0, 0, pl.ds(0, 1), :], sems.at[slot]).wait()
            return carry

        lax.fori_loop(0, groups, drain, 0)
        w = wcol_ref[...]
        ffn = w[:, 0:1] * buf_ref[slot, 0].reshape(tc, d) + w[:, 1:2] * buf_ref[slot, 1].reshape(tc, d)
        y = x_ref[...] + mod_ref[...][5:6] * ffn
        if final:
            y = _rms(y) * fn_ref[...]
        o_ref[...] = y

    @pl.when(i == 0)
    def _():
        gather(0, pos0_ref, pos1_ref)

    for slot in range(2):
        @pl.when(i % 2 == slot)
        def _():
            @pl.when(i + 1 < pl.num_programs(0))
            def _():
                gather(1 - slot, nxt0_ref, nxt1_ref)

            consume(slot)


def _combine(pos, ys, wcol, x1, mod_l, fnorm, seq, final):
    t, d = x1.shape
    tc = DMA_TILE
    per_batch = seq // tc
    row = lambda i: (i, 0)
    last = t // tc - 1
    slot = pl.BlockSpec((tc,), lambda i: (i,), memory_space=pltpu.SMEM)
    ahead = pl.BlockSpec((tc,), lambda i: (jnp.minimum(i + 1, last),), memory_space=pltpu.SMEM)
    return pl.pallas_call(
        functools.partial(_combine_kernel, final),
        grid=(t // tc,),
        in_specs=[slot, slot, ahead, ahead,
                  pl.BlockSpec(memory_space=pl.ANY),
                  pl.BlockSpec((tc, LANES), row), pl.BlockSpec((tc, d), row),
                  pl.BlockSpec((None, 6, d), lambda i: (i // per_batch, 0, 0)),
                  pl.BlockSpec((1, d), lambda i: (0, 0))],
        out_specs=pl.BlockSpec((tc, d), row),
        out_shape=jax.ShapeDtypeStruct((t, d), F32),
        scratch_shapes=[pltpu.VMEM((2, N_ASSIGN, tc // SUBLANES, SUBLANES, d), F32),
                        pltpu.SemaphoreType.DMA((2,))],
        compiler_params=_params("arbitrary"),
        name="moe_combine",
    )(pos[0], pos[1], pos[0], pos[1], ys.reshape(ys.shape[0] // SUBLANES, SUBLANES, d), wcol, x1, mod_l, fnorm)


def _moe(h2, eidx, wcol, cnt, w_gate, w_up, w_down, layer, x1, mod_l, fnorm, seq, final):
    pos, meta = _positions(eidx, cnt)
    xs = _dispatch(meta[2], meta[3], meta[1], pos, h2)
    ys = _experts(meta[0], meta[1], xs, w_gate, w_up, w_down, layer)
    return _combine(pos, ys, wcol, x1, mod_l, fnorm, seq, final)


def _head_expand():
    m = np.zeros((LANES, ATTN_WIDTH), np.float32)
    for h in range(N_HEADS):
        m[h, h * HEAD_DIM:(h + 1) * HEAD_DIM] = 1.0
    return jnp.asarray(m, BF16)


def kernel(x, c, positions, w_in, w_out, attn_norm, hgrn_norm, lb_params, ada_w, ada_b,
           w_router, b_router, w_gate, w_up, w_down, final_norm):
    batch, seq, d = x.shape
    t = batch * seq
    c_pad = jnp.pad(c, ((0, 8 - batch), (0, 0)))
    mod = _ada_mod(c_pad, ada_w, ada_b)[:, :batch].reshape(DEPTH, batch, 6, d)
    lbs = _lower_bounds(lb_params)
    cos, sin = _rope_tables(positions)
    cos, sin = cos.reshape(t, LANES), sin.reshape(t, LANES)
    head_id = np.arange(MXU_WIDTH) // HEAD_DIM
    ones_bd = jnp.asarray(head_id[:, None] == head_id[None, :], BF16)
    em = _head_expand()
    wr_t = w_router.T
    br = b_router.reshape(N_EXPERTS, 1)
    fnorm = final_norm.reshape(1, d)

    x2d = x.reshape(t, d)
    for l in range(DEPTH):
        outs = _inproj(x2d, mod[l], w_in[l].astype(BF16), cos, sin, lbs[l:l + 1], seq)
        qs, ks, vs = outs[0:3], outs[3:6], outs[6:9]
        qh, kh, gh, ih, og = outs[9:]
        us, sts = [], []
        for p, (window, dilation) in enumerate(DILATED_PATTERNS):
            u, st = _attn_pattern(qs[p], ks[p], vs[p], window, dilation)
            us.append(u)
            sts.append(st)
        o_h = _hgrn(qh, kh, gh, ih, ones_bd, batch, seq)
        x1, h2, eidx, wcol, cnt = _outproj(us, sts, o_h, og, x2d, mod[l], attn_norm[l:l + 1], hgrn_norm[l:l + 1],
                                      w_out[l].astype(BF16), em, wr_t, br, seq)
        x2d = _moe(h2, eidx, wcol, cnt, w_gate, w_up, w_down, l, x1, mod[l], fnorm, seq,
                   final=(l == DEPTH - 1))
    return x2d.reshape(batch, seq, d)
```

```python
import functools

import numpy as np
import jax
import jax.numpy as jnp
from jax import lax
from jax.experimental import pallas as pl
from jax.experimental.pallas import tpu as pltpu

D_MODEL = 1024
DEPTH = 2
ATTN_WIDTH = 512
HGRN_WIDTH = 512
HEAD_DIM = 64
N_HEADS = 8
DILATED_PATTERNS = ((128, 1), (512, 4), (2048, 16))
ATTN_BLOCK = 128
ATTN_STEP_BLOCKS = 4
ROPE_THETA = 10000.0
N_EXPERTS = 16
N_GROUPS = 4
EXPERTS_PER_GROUP = 4
D_FF_EXPERT = 512
RMS_EPS = 1e-6
IN_COLS = 3 * ATTN_WIDTH + 4 * HGRN_WIDTH

LANES = 128
SUBLANES = 8
MXU_WIDTH = 256
VMEM_LIMIT_BYTES = 56 * 1024 * 1024

TOKEN_TILE = 512
N_ASSIGN = 2
MOE_TILE = 512
POS_TILE = 1024
DMA_TILE = 1024
HGRN_CHUNK = 64
HGRN_STEP_CHUNKS = 8
HGRN_SUB = 8
NEG_BIG = -1e30
LOG2E = 1.4426950408889634

F32 = jnp.float32
BF16 = jnp.bfloat16

_NT = (((1,), (1,)), ((), ()))
_TN = (((0,), (0,)), ((), ()))


def _params(*sem):
    return pltpu.CompilerParams(dimension_semantics=sem, vmem_limit_bytes=VMEM_LIMIT_BYTES)


def _sigmoid(x):
    return 0.5 * jnp.tanh(0.5 * x) + 0.5


def _rms(x):
    return x * lax.rsqrt(jnp.mean(x * x, axis=-1, keepdims=True) + RMS_EPS)


def _split3(a):
    a1 = a.astype(BF16)
    r1 = a - a1.astype(F32)
    a2 = r1.astype(BF16)
    a3 = (r1 - a2.astype(F32)).astype(BF16)
    return a1, a2, a3


def _dot_hi(a, b, dims):
    a1, a2, _ = _split3(a)
    b1, b2, _ = _split3(b)
    d = lambda p, q: lax.dot_general(p, q, dims, preferred_element_type=F32)
    return d(a1, b1) + (d(a2, b1) + d(a1, b2))


def _ada_kernel(c_ref, w_ref, b_ref, o_ref):
    c = c_ref[...]
    ca = c * _sigmoid(c)
    o_ref[...] = _dot_hi(ca, w_ref[...], (((1,), (0,)), ((), ()))) + b_ref[...]


def _ada_mod(c_pad, ada_w, ada_b):
    depth, d, n = ada_w.shape
    tn = 1536
    rows = c_pad.shape[0]
    return pl.pallas_call(
        _ada_kernel,
        grid=(depth, n // tn),
        in_specs=[
            pl.BlockSpec((rows, d), lambda l, j: (0, 0)),
            pl.BlockSpec((None, d, tn), lambda l, j: (l, 0, j)),
            pl.BlockSpec((None, 1, tn), lambda l, j: (l, 0, j)),
        ],
        out_specs=pl.BlockSpec((None, rows, tn), lambda l, j: (l, 0, j)),
        out_shape=jax.ShapeDtypeStruct((depth, rows, n), F32),
        compiler_params=_params("parallel", "parallel"),
        name="ada_mod",
    )(c_pad, ada_w, ada_b.reshape(depth, 1, n))


def _lb_kernel(p_ref, o_ref):
    p = p_ref[...]
    e = jnp.exp(p - jnp.max(p, axis=0, keepdims=True))
    sm = e / jnp.sum(e, axis=0, keepdims=True)
    run = jnp.zeros_like(sm[0:1])
    for l in range(p.shape[0]):
        run = run + sm[l:l + 1]
        o_ref[l:l + 1, :] = run - sm[0:1]


def _lower_bounds(lb_params):
    return pl.pallas_call(
        _lb_kernel,
        out_shape=jax.ShapeDtypeStruct(lb_params.shape, F32),
        name="lower_bounds",
    )(lb_params)


def _rope_kernel(pos_ref, inv_ref, cos_ref, sin_ref):
    ang = inv_ref[...] * pos_ref[...].astype(F32)
    reps = LANES // ang.shape[0]
    c = jnp.concatenate([jnp.cos(ang)] * reps, axis=0)
    s = jnp.concatenate([jnp.sin(ang)] * reps, axis=0)
    row = lax.broadcasted_iota(jnp.int32, s.shape, 0)
    first = (row % HEAD_DIM) < (HEAD_DIM // 2)
    cos_ref[...] = c.T
    sin_ref[...] = jnp.where(first, -s, s).T


def _rope_tables(positions):
    b, s = positions.shape
    ts = 1024
    half = HEAD_DIM // 2
    inv = (ROPE_THETA ** (-jnp.arange(half, dtype=F32) / half)).reshape(half, 1)
    out = jax.ShapeDtypeStruct((b, s, LANES), F32)
    return pl.pallas_call(
        _rope_kernel,
        grid=(b, s // ts),
        in_specs=[
            pl.BlockSpec((None, 1, ts), lambda i, j: (i, 0, j)),
            pl.BlockSpec((half, 1), lambda i, j: (0, 0)),
        ],
        out_specs=[pl.BlockSpec((None, ts, LANES), lambda i, j: (i, j, 0))] * 2,
        out_shape=[out, out],
        compiler_params=_params("parallel", "parallel"),
        name="rope_tables",
    )(positions.reshape(b, 1, s), inv)


def _store_streams(val, refs):
    tm = val.shape[0]
    for (_, d), ref in zip(DILATED_PATTERNS, refs):
        if d == 1:
            ref[0] = val.astype(BF16)
        else:
            sw = jnp.swapaxes(val.reshape(tm // d, d, val.shape[1]), 0, 1)
            ref[...] = sw.astype(BF16)


def _inproj_kernel(x_ref, mod_ref, w_ref, cos_ref, sin_ref, lb_ref,
                   q1_ref, q4_ref, q16_ref, k1_ref, k4_ref, k16_ref, v1_ref, v4_ref, v16_ref,
                   qh_ref, kh_ref, gh_ref, ih_ref, og_ref):
    x = x_ref[...]
    mod = mod_ref[...]
    h = _rms(x) * (1.0 + mod[1:2]) + mod[0:1]
    hb = h.astype(BF16)
    reps = ATTN_WIDTH // LANES
    cos = jnp.concatenate([cos_ref[...]] * reps, axis=1)
    sin = jnp.concatenate([sin_ref[...]] * reps, axis=1)
    lane = lax.broadcasted_iota(jnp.int32, cos.shape, 1)
    first = (lane % HEAD_DIM) < (HEAD_DIM // 2)
    half = HEAD_DIM // 2

    def proj(j):
        return jnp.dot(hb, w_ref[:, j * 512:(j + 1) * 512], preferred_element_type=F32)

    def rot(t):
        swapped = jnp.where(first, pltpu.roll(t, ATTN_WIDTH - half, 1), pltpu.roll(t, half, 1))
        return t * cos + swapped * sin

    ps = [proj(j) for j in range(IN_COLS // 512)]
    _store_streams(rot(ps[0]) * (HEAD_DIM ** -0.5 * LOG2E), (q1_ref, q4_ref, q16_ref))
    _store_streams(rot(ps[1]), (k1_ref, k4_ref, k16_ref))
    _store_streams(ps[2], (v1_ref, v4_ref, v16_ref))
    qh_ref[...] = (ps[3] * _sigmoid(ps[3])).astype(BF16)
    lb = lb_ref[...]
    f = lb + (1.0 - lb) / (1.0 + jnp.exp(-ps[4]))
    kh_ref[...] = (1.0 - f).astype(BF16)
    gh_ref[...] = jnp.log(f)
    ih_ref[...] = ps[5].astype(BF16)
    og_ref[...] = _sigmoid(ps[6]).astype(BF16)


def _stream_spec(tm, dil, per_batch, width):
    return pl.BlockSpec((None, dil, tm // dil, width), lambda i: (i // per_batch, 0, i % per_batch, 0))


def _inproj(x2d, mod_l, w_in_l, cos, sin, lb_l, seq):
    t, d = x2d.shape
    tm = TOKEN_TILE
    per_batch = seq // tm
    batch = t // seq
    row = lambda i: (i, 0)
    half_spec = pl.BlockSpec((tm, 512), row)
    bf = jax.ShapeDtypeStruct((t, 512), BF16)
    stream_specs = [_stream_spec(tm, dil, per_batch, ATTN_WIDTH) for _, dil in DILATED_PATTERNS]
    stream_shapes = [jax.ShapeDtypeStruct((batch, dil, seq // dil, ATTN_WIDTH), BF16)
                     for _, dil in DILATED_PATTERNS]
    return pl.pallas_call(
        _inproj_kernel,
        grid=(t // tm,),
        in_specs=[
            pl.BlockSpec((tm, d), row),
            pl.BlockSpec((None, 6, d), lambda i: (i // per_batch, 0, 0)),
            pl.BlockSpec((d, IN_COLS), lambda i: (0, 0)),
            pl.BlockSpec((tm, LANES), row),
            pl.BlockSpec((tm, LANES), row),
            pl.BlockSpec((1, 512), lambda i: (0, 0)),
        ],
        out_specs=stream_specs * 3 + [half_spec] * 5,
        out_shape=stream_shapes * 3 + [bf, bf, jax.ShapeDtypeStruct((t, 512), F32), bf, bf],
        compiler_params=_params("parallel"),
        name="inproj",
    )(x2d, mod_l, w_in_l, cos, sin, lb_l)


def _attn_kernel(q_ref, kc_ref, kp_ref, vc_ref, vp_ref, bias_ref, u_ref, st_ref):
    blk = ATTN_BLOCK
    n = pl.program_id(2)
    keys = jnp.concatenate([kp_ref[...], kc_ref[...]], axis=0)
    vals = jnp.concatenate([vp_ref[...], vc_ref[...]], axis=0)
    biases = [bias_ref[jnp.minimum(n, 1)]] + [bias_ref[1]] * (ATTN_STEP_BLOCKS - 1)
    lane = lax.broadcasted_iota(jnp.int32, (blk, LANES), 1)
    low = lane < HEAD_DIM
    zero = jnp.zeros((), BF16)
    ones = jnp.ones((2 * blk, LANES), BF16)
    slabs = [slice(pair * LANES, (pair + 1) * LANES) for pair in range(N_HEADS // 2)]
    work = [(j, pair) for j in range(ATTN_STEP_BLOCKS) for pair in range(N_HEADS // 2)]

    scores = []
    for j, pair in work:
        qp = q_ref[j * blk:(j + 1) * blk, slabs[pair]]
        qst = jnp.concatenate([jnp.where(low, qp, zero), jnp.where(low, zero, qp)], axis=0)
        kwin = keys[j * blk:(j + 2) * blk, slabs[pair]]
        scores.append(lax.dot_general(qst, kwin, _NT, preferred_element_type=F32) + biases[j])
    maxes, probs = [], []
    for s in scores:
        m = jnp.max(s, axis=1, keepdims=True).astype(BF16).astype(F32)
        maxes.append(m)
        probs.append(jnp.exp2((s - m).astype(BF16)))
    stats = [jnp.zeros((blk, LANES), F32) for _ in range(ATTN_STEP_BLOCKS)]
    for (j, pair), m, p in zip(work, maxes, probs):
        vwin = vals[j * blk:(j + 2) * blk, slabs[pair]]
        ul = jnp.dot(p, jnp.concatenate([vwin, ones], axis=1), preferred_element_type=F32)
        u, l = ul[:, :LANES], ul[:, LANES:]
        for hh in range(2):
            head = 2 * pair + hh
            stats[j] = jnp.where(lane == head, m[hh * blk:(hh + 1) * blk], stats[j])
            stats[j] = jnp.where(lane == N_HEADS + head, l[hh * blk:(hh + 1) * blk], stats[j])
        u_ref[j * blk:(j + 1) * blk, slabs[pair]] = jnp.where(low, u[:blk], u[blk:]).astype(BF16)
    for j in range(ATTN_STEP_BLOCKS):
        st_ref[j * blk:(j + 1) * blk, :] = stats[j]


def _band_bias(steps):
    blk = ATTN_BLOCK
    qi = np.arange(2 * blk)[:, None] % blk
    kj = np.arange(2 * blk)[None, :]
    dist = qi + blk - kj
    band = (dist >= 0) & (dist <= steps)
    first = band & (kj >= blk)
    return jnp.asarray(np.where(np.stack([first, band]), 0.0, NEG_BIG), F32)


def _attn_pattern(qs, ks, vs, window, dilation):
    batch, _, m, _ = qs.shape
    rows = ATTN_STEP_BLOCKS * ATTN_BLOCK
    nb = m // rows
    bias = _band_bias(window // dilation)
    cur = pl.BlockSpec((None, None, rows, ATTN_WIDTH), lambda b, r, n: (b, r, n, 0))
    prev = pl.BlockSpec((None, None, ATTN_BLOCK, ATTN_WIDTH),
                        lambda b, r, n: (b, r, jnp.maximum(n * ATTN_STEP_BLOCKS - 1, 0), 0))
    return pl.pallas_call(
        _attn_kernel,
        grid=(batch, dilation, nb),
        in_specs=[cur, cur, prev, cur, prev,
                  pl.BlockSpec(bias.shape, lambda b, r, n: (0, 0, 0))],
        out_specs=[cur, pl.BlockSpec((None, None, rows, LANES), lambda b, r, n: (b, r, n, 0))],
        out_shape=[jax.ShapeDtypeStruct((batch, dilation, m, ATTN_WIDTH), BF16),
                   jax.ShapeDtypeStruct((batch, dilation, m, LANES), F32)],
        compiler_params=_params("parallel", "parallel", "arbitrary"),
        name=f"attn_d{dilation}",
    )(qs, ks, ks, vs, vs, bias)


def _hgrn_kernel(q_ref, k_ref, g_ref, v_ref, ones_ref, o_ref, st_ref):
    @pl.when(pl.program_id(1) == 0)
    def _():
        st_ref[...] = jnp.zeros_like(st_ref)

    state = [st_ref[grp] for grp in range(st_ref.shape[0])]
    for chunk in range(HGRN_STEP_CHUNKS):
        state = _hgrn_chunk(chunk * HGRN_CHUNK, q_ref, k_ref, g_ref, v_ref, ones_ref, o_ref, state)
    for grp, st in enumerate(state):
        st_ref[grp] = st


def _hgrn_chunk(off, q_ref, k_ref, g_ref, v_ref, ones_ref, o_ref, state):
    c, sub, w = HGRN_CHUNK, HGRN_SUB, HGRN_WIDTH
    span = slice(off, off + c)
    q = q_ref[span, :].astype(F32)
    k = k_ref[span, :].astype(F32)
    vb = v_ref[span, :]
    v = vb.astype(F32)
    g = g_ref[span, :]

    ri = lax.broadcasted_iota(jnp.int32, (c, c), 0)
    ci = lax.broadcasted_iota(jnp.int32, (c, c), 1)
    tri = (ci <= ri).astype(BF16)
    g1, g2, g3 = _split3(g)
    cs = lambda t: jnp.dot(tri, t, preferred_element_type=F32)
    b = (cs(g1) + (cs(g2) + cs(g3))) * LOG2E
    b_last = b[c - 1:c, :]

    hw = state[0].shape[0]
    qd = (q * jnp.exp2(b)).astype(BF16)
    kt = (k * jnp.exp2(b_last - b)).astype(BF16)
    decay = jnp.exp2(b_last)
    r0i = lax.broadcasted_iota(jnp.int32, (hw, hw), 0) // HEAD_DIM
    c0i = lax.broadcasted_iota(jnp.int32, (hw, hw), 1) // HEAD_DIM
    same_head = r0i == c0i
    o_parts, new_state = [], []
    for grp, st in enumerate(state):
        sl = slice(grp * hw, (grp + 1) * hw)
        o_parts.append(lax.dot_general(qd[:, sl], st.astype(BF16), _NT, preferred_element_type=F32))
        upd = lax.dot_general(vb[:, sl], kt[:, sl], _TN, preferred_element_type=F32)
        new_state.append(st * decay[:, sl] + jnp.where(same_head, upd, 0.0))
    o_inter = jnp.concatenate(o_parts, axis=1)

    gheads = hw // HEAD_DIM
    groups = [slice(grp * hw, (grp + 1) * hw) for grp in range(w // hw)]
    hrow = lax.broadcasted_iota(jnp.int32, (gheads * sub, hw), 0) // sub
    hlane = lax.broadcasted_iota(jnp.int32, (gheads * sub, hw), 1) // HEAD_DIM
    hmask = hrow == hlane
    trow = lax.broadcasted_iota(jnp.int32, (sub, w), 0)
    ones = ones_ref[...]
    ow = ones.shape[0]

    nblk = c // sub
    ws = []
    for blk in range(nblk):
        rows = slice(blk * sub, (blk + 1) * sub)
        bi, qi, ki = b[rows], q[rows], k[rows]
        for s in range(sub):
            e = jnp.exp2(bi - bi[s:s + 1])
            ws.append(jnp.where(trow >= s, qi * (ki[s:s + 1] * e), 0.0))
    wcat = jnp.concatenate(ws, axis=0).astype(BF16)
    sc = jnp.concatenate(
        [jnp.dot(wcat[:, j * ow:(j + 1) * ow], ones, preferred_element_type=F32) for j in range(w // ow)],
        axis=1)

    inter_scores = [None]
    for blk in range(1, nblk):
        r0 = blk * sub
        rows = slice(r0, r0 + sub)
        bref = b[r0:r0 + 1]
        qs = q[rows] * jnp.exp2(b[rows] - bref)
        kp = (k[0:r0] * jnp.exp2(bref - b[0:r0])).astype(BF16)
        per_group = []
        for sl in groups:
            qexp = jnp.where(hmask, jnp.concatenate([qs[:, sl]] * gheads, axis=0), 0.0).astype(BF16)
            per_group.append(lax.dot_general(qexp, kp[:, sl], _NT, preferred_element_type=F32))
        inter_scores.append(per_group)

    for blk in range(nblk):
        r0 = blk * sub
        rows = slice(r0, r0 + sub)
        acc = o_inter[rows]
        if blk > 0:
            parts = []
            for sl, a in zip(groups, inter_scores[blk]):
                oexp = jnp.dot(a.astype(BF16), vb[0:r0, sl], preferred_element_type=F32)
                oexp = jnp.where(hmask, oexp, 0.0)
                part = oexp[0:sub]
                for hd in range(1, gheads):
                    part = part + oexp[hd * sub:(hd + 1) * sub]
                parts.append(part)
            acc = acc + jnp.concatenate(parts, axis=1)
        vi = v[rows]
        for s in range(sub):
            at = (blk * sub + s) * sub
            acc = acc + sc[at:at + sub] * vi[s:s + 1]
        o_ref[off + r0:off + r0 + sub, :] = acc
    return new_state


def _hgrn(qh, kh, gh, ih, ones_bd, batch, seq):
    c, w = HGRN_CHUNK, HGRN_WIDTH
    view = lambda a: a.reshape(batch, seq, w)
    rows = HGRN_STEP_CHUNKS * c
    blk = pl.BlockSpec((None, rows, w), lambda b, n: (b, n, 0))
    o = pl.pallas_call(
        _hgrn_kernel,
        grid=(batch, seq // rows),
        in_specs=[blk, blk, blk, blk, pl.BlockSpec(ones_bd.shape, lambda b, n: (0, 0))],
        out_specs=blk,
        out_shape=jax.ShapeDtypeStruct((batch, seq, w), F32),
        scratch_shapes=[pltpu.VMEM((w // MXU_WIDTH, MXU_WIDTH, MXU_WIDTH), F32)],
        compiler_params=_params("parallel", "arbitrary"),
        name="hgrn2",
    )(view(qh), view(kh), view(gh), view(ih), ones_bd)
    return o.reshape(batch * seq, w)


def _route(probs):
    rows = [probs[i:i + 1, :] for i in range(N_EXPERTS)]
    gsum = []
    for gidx in range(N_GROUPS):
        a, b_, c_, d_ = rows[4 * gidx:4 * gidx + 4]
        hi1, lo1 = jnp.maximum(a, b_), jnp.minimum(a, b_)
        hi2, lo2 = jnp.maximum(c_, d_), jnp.minimum(c_, d_)
        top1 = jnp.maximum(hi1, hi2)
        second = jnp.maximum(jnp.minimum(hi1, hi2), jnp.maximum(lo1, lo2))
        gsum.append(top1 + second)
    best, gi = gsum[0], jnp.zeros(gsum[0].shape, jnp.int32)
    for gidx in range(1, N_GROUPS):
        upd = gsum[gidx] > best
        best = jnp.where(upd, gsum[gidx], best)
        gi = jnp.where(upd, gidx, gi)
    vals = []
    for j in range(EXPERTS_PER_GROUP):
        vj = rows[j]
        for gidx in range(1, N_GROUPS):
            vj = jnp.where(gi == gidx, rows[4 * gidx + j], vj)
        vals.append(vj)
    v1, i1 = vals[0], jnp.zeros(gi.shape, jnp.int32)
    for j in range(1, EXPERTS_PER_GROUP):
        upd = vals[j] > v1
        v1 = jnp.where(upd, vals[j], v1)
        i1 = jnp.where(upd, j, i1)
    v2, i2 = jnp.full(v1.shape, -1.0, F32), jnp.zeros(gi.shape, jnp.int32)
    for j in range(EXPERTS_PER_GROUP):
        upd = (i1 != j) & (vals[j] > v2)
        v2 = jnp.where(upd, vals[j], v2)
        i2 = jnp.where(upd, j, i2)
    tot = v1 + v2
    base = gi * EXPERTS_PER_GROUP
    return base + i1, base + i2, v1 / tot, v2 / tot


def _outproj_kernel(u1_ref, u2_ref, u3_ref, s1_ref, s2_ref, s3_ref, o_ref, og_ref, x_ref, mod_ref,
                    an_ref, hn_ref, w_ref, em_ref, wr_ref, br_ref,
                    x1_ref, h2_ref, eidx_ref, wcol_ref, cnt_ref, uscr_ref, sscr_ref):
    us = (u1_ref, u2_ref, u3_ref)
    ss = (s1_ref, s2_ref, s3_ref)
    em = em_ref[...]

    def natural(ref):
        dil, per, width = ref.shape
        if dil == 1:
            return ref[0].astype(F32)
        scr = uscr_ref if width == ATTN_WIDTH else sscr_ref
        slabs = scr.shape[0]
        for r in range(dil):
            val = ref[r].astype(F32)
            for c in range(slabs):
                scr[c, pl.ds(r, per, stride=dil), :] = val[:, c * LANES:(c + 1) * LANES]
        return jnp.concatenate([scr[c] for c in range(slabs)], axis=1)

    stats = [natural(s_ref) for s_ref in ss]
    mmax = jnp.maximum(jnp.maximum(stats[0], stats[1]), stats[2])
    wts = [jnp.exp2(s - mmax) for s in stats]
    den = jnp.zeros_like(mmax)
    for s, wgt in zip(stats, wts):
        den = den + pltpu.roll(wgt, N_HEADS, 1) * s
    inv = 1.0 / pltpu.roll(den, LANES - N_HEADS, 1)
    head_lane = lax.broadcasted_iota(jnp.int32, mmax.shape, 1) < N_HEADS
    attn = jnp.zeros((mmax.shape[0], ATTN_WIDTH), F32)
    for p, wgt in enumerate(wts):
        share = jnp.where(head_lane, wgt * inv, 0.0).astype(BF16)
        attn = attn + jnp.dot(share, em, preferred_element_type=F32) * natural(us[p])
    mod = mod_ref[...]
    a_n = _rms(attn) * an_ref[...]
    rec = _rms(o_ref[...]) * hn_ref[...] * og_ref[...].astype(F32)
    merged = jnp.concatenate([a_n, rec], axis=1).astype(BF16)
    mix = jnp.dot(merged, w_ref[...], preferred_element_type=F32)
    x1 = x_ref[...] + mod[2:3] * mix
    x1_ref[...] = x1
    h2 = _rms(x1) * (1.0 + mod[4:5]) + mod[3:4]
    h2_ref[...] = h2

    logits = _dot_hi(wr_ref[...], h2, _NT) + br_ref[...]
    e = jnp.exp(logits - jnp.max(logits, axis=0, keepdims=True))
    probs = e / jnp.sum(e, axis=0, keepdims=True)
    e1, e2, w1, w2 = _route(probs)
    eidx_ref[...] = jnp.concatenate([e1, e2], axis=0)
    erow = lax.broadcasted_iota(jnp.int32, probs.shape, 0)
    chosen = (erow == e1).astype(F32) + (erow == e2).astype(F32)
    cnt_ref[...] = _spread(jnp.sum(chosen, axis=1, keepdims=True))
    tm = probs.shape[1]
    srow = lax.broadcasted_iota(jnp.int32, (LANES, tm), 0)
    w_t = jnp.where(srow == 0, w1, jnp.where(srow == 1, w2, 0.0))
    wcol_ref[...] = w_t.T


def _outproj(us, sts, o_h, og, x2d, mod_l, an_l, hn_l, w_out_l, em, wr_t, br, seq):
    t, d = x2d.shape
    tm = TOKEN_TILE
    per_batch = seq // tm
    row = lambda i: (i, 0)
    const = lambda i: (0, 0)
    half = pl.BlockSpec((tm, 512), row)
    stat = pl.BlockSpec((tm, LANES), row)
    full = pl.BlockSpec((tm, d), row)
    u_specs = [_stream_spec(tm, dil, per_batch, ATTN_WIDTH) for _, dil in DILATED_PATTERNS]
    s_specs = [_stream_spec(tm, dil, per_batch, LANES) for _, dil in DILATED_PATTERNS]
    return pl.pallas_call(
        _outproj_kernel,
        grid=(t // tm,),
        in_specs=u_specs + s_specs + [half, half, full,
                  pl.BlockSpec((None, 6, d), lambda i: (i // per_batch, 0, 0)),
                  pl.BlockSpec((1, 512), const), pl.BlockSpec((1, 512), const),
                  pl.BlockSpec((d, d), const),
                  pl.BlockSpec((LANES, 512), const),
                  pl.BlockSpec((N_EXPERTS, d), const), pl.BlockSpec((N_EXPERTS, 1), const)],
        out_specs=[full, full, pl.BlockSpec((N_ASSIGN, tm), lambda i: (0, i)), stat,
                   pl.BlockSpec((None, N_EXPERTS, LANES), lambda i: (i, 0, 0))],
        out_shape=[jax.ShapeDtypeStruct((t, d), F32), jax.ShapeDtypeStruct((t, d), F32),
                   jax.ShapeDtypeStruct((N_ASSIGN, t), jnp.int32), jax.ShapeDtypeStruct((t, LANES), F32),
                   jax.ShapeDtypeStruct((t // tm, N_EXPERTS, LANES), F32)],
        scratch_shapes=[pltpu.VMEM((ATTN_WIDTH // LANES, tm, LANES), F32), pltpu.VMEM((1, tm, LANES), F32)],
        compiler_params=_params("parallel"),
        name="outproj_route",
    )(*us, *sts, o_h, og, x2d, mod_l, an_l, hn_l, w_out_l, em, wr_t, br)


def _one_hots(e_ref):
    e = e_ref[...]
    erow = lax.broadcasted_iota(jnp.int32, (N_EXPERTS, e.shape[1]), 0)
    return (erow == e[0:1]).astype(F32), (erow == e[1:2]).astype(F32)


def _spread(col):
    return jnp.broadcast_to(col, (N_EXPERTS, LANES))


def _positions_kernel(e_ref, cnt_ref, pos_ref, meta_ref, carry_ref, offs_ref):
    i = pl.program_id(0)
    tp = e_ref.shape[1]
    oh0, oh1 = _one_hots(e_ref)

    @pl.when(i == 0)
    def _():
        cnt = jnp.sum(cnt_ref[...], axis=0)
        padded = jnp.floor((cnt + (MOE_TILE - 1)) * (1.0 / MOE_TILE)) * MOE_TILE
        run = jnp.zeros((1, LANES), F32)
        starts = []
        for ex in range(N_EXPERTS):
            starts.append(run)
            run = run + padded[ex:ex + 1]
        offs = jnp.concatenate(starts, axis=0)
        offs_ref[...] = offs
        carry_ref[...] = jnp.zeros_like(carry_ref)
        ends = offs + padded
        lane = lax.broadcasted_iota(jnp.int32, (N_EXPERTS, LANES), 1)
        srow = lax.broadcasted_iota(jnp.int32, (N_EXPERTS, LANES), 0)
        tile_start = (lane * MOE_TILE).astype(F32)
        tile_expert = jnp.sum((ends <= tile_start).astype(F32), axis=0, keepdims=True)
        tile_expert = jnp.minimum(tile_expert, N_EXPERTS - 1.0)
        on_diag = srow == lane
        ends_lane = jnp.sum(jnp.where(on_diag, ends, 0.0), axis=0, keepdims=True)
        pad_lane = jnp.sum(jnp.where(on_diag, padded, 0.0), axis=0, keepdims=True)
        meta = jnp.concatenate([tile_expert, run * (1.0 / MOE_TILE), ends_lane, pad_lane,
                                jnp.zeros((4, LANES), F32)], axis=0)
        meta_ref[...] = meta.astype(jnp.int32)

    r = lax.broadcasted_iota(jnp.int32, (tp, tp), 0)
    c = lax.broadcasted_iota(jnp.int32, (tp, tp), 1)
    upper = (r <= c).astype(BF16)
    oh = jnp.concatenate([oh0, oh1], axis=0).astype(BF16)
    pre = jnp.dot(oh, upper, preferred_element_type=F32)
    pre0, pre1 = pre[:N_EXPERTS], pre[N_EXPERTS:]
    tot0, tot1 = pre0[:, tp - 1:tp], pre1[:, tp - 1:tp]
    base = offs_ref[...][:, 0:1] + carry_ref[...][:, 0:1]
    p0 = jnp.sum(oh0 * (pre0 - 1.0 + base), axis=0, keepdims=True)
    p1 = jnp.sum(oh1 * (pre1 - 1.0 + (base + tot0)), axis=0, keepdims=True)
    pos_ref[...] = jnp.concatenate([p0, p1], axis=0).astype(jnp.int32)
    carry_ref[...] += _spread(tot0 + tot1)


def _positions(eidx, cnt):
    t = eidx.shape[1]
    tp = POS_TILE
    blk = pl.BlockSpec((N_ASSIGN, tp), lambda i: (0, i))
    whole = pl.BlockSpec(cnt.shape, lambda i: (0, 0, 0))
    stat = pltpu.VMEM((N_EXPERTS, LANES), F32)
    return pl.pallas_call(
        _positions_kernel,
        grid=(t // tp,),
        in_specs=[blk, whole],
        out_specs=[blk, pl.BlockSpec((8, LANES), lambda i: (0, 0))],
        out_shape=[jax.ShapeDtypeStruct((N_ASSIGN, t), jnp.int32), jax.ShapeDtypeStruct((8, LANES), jnp.int32)],
        scratch_shapes=[stat, stat],
        compiler_params=_params("arbitrary"),
        name="moe_positions",
    )(eidx, cnt)


def _row(ref, r):
    return ref.at[r >> 3, pl.ds(r & (SUBLANES - 1), 1), :]


def _dispatch_kernel(ends_ref, pad_ref, used_ref, pos0_ref, pos1_ref, h_ref, xs_hbm, zero_ref, src_ref, sem, sems):
    i = pl.program_id(0)
    groups = h_ref.shape[0]
    tile_groups = MOE_TILE // SUBLANES

    def clear_copy(start):
        first = pl.multiple_of(start // SUBLANES, tile_groups)
        return pltpu.make_async_copy(zero_ref, xs_hbm.at[pl.ds(first, tile_groups)], sem)

    @pl.when(i == 0)
    def _():
        zero_ref[...] = jnp.zeros_like(zero_ref)
        n_tiles = xs_hbm.shape[0] // tile_groups
        for act in ("start", "wait"):
            for ex in range(N_EXPERTS):
                @pl.when(pad_ref[ex] > 0)
                def _():
                    getattr(clear_copy(ends_ref[ex] - MOE_TILE), act)()

            def clear_tail(tile, carry):
                getattr(clear_copy(tile * MOE_TILE), act)()
                return carry

            lax.fori_loop(used_ref[0], n_tiles, clear_tail, 0)

    def scatter(slot):
        src_ref[slot] = h_ref[...]

        def issue(g, carry):
            for u in range(SUBLANES):
                src = src_ref.at[slot, g, pl.ds(u, 1), :]
                for pos_ref in (pos0_ref, pos1_ref):
                    pltpu.make_async_copy(src, _row(xs_hbm, pos_ref[g * SUBLANES + u]), sems.at[slot]).start()
            return carry

        lax.fori_loop(0, groups, issue, 0)

    def drain(slot):
        def wait(g, carry):
            for _ in range(SUBLANES * N_ASSIGN):
                pltpu.make_async_copy(src_ref.at[slot, 0, pl.ds(0, 1), :], _row(xs_hbm, 0), sems.at[slot]).wait()
            return carry

        lax.fori_loop(0, groups, wait, 0)

    for slot in range(2):
        @pl.when(i % 2 == slot)
        def _():
            scatter(slot)

            @pl.when(i > 0)
            def _():
                drain(1 - slot)

            @pl.when(i == pl.num_programs(0) - 1)
            def _():
                drain(slot)


def _dispatch(ends, padded, n_used, pos, h2):
    t, d = h2.shape
    tp = DMA_TILE
    rows = N_ASSIGN * t + N_EXPERTS * MOE_TILE
    slot = pl.BlockSpec((tp,), lambda i, *_: (i,), memory_space=pltpu.SMEM)
    xs = pl.pallas_call(
        _dispatch_kernel,
        grid_spec=pltpu.PrefetchScalarGridSpec(
            num_scalar_prefetch=3,
            grid=(t // tp,),
            in_specs=[slot, slot, pl.BlockSpec((tp // SUBLANES, SUBLANES, d), lambda i, *_: (i, 0, 0))],
            out_specs=pl.BlockSpec(memory_space=pl.ANY),
            scratch_shapes=[pltpu.VMEM((MOE_TILE // SUBLANES, SUBLANES, d), F32),
                            pltpu.VMEM((2, tp // SUBLANES, SUBLANES, d), F32),
                            pltpu.SemaphoreType.DMA(()), pltpu.SemaphoreType.DMA((2,))],
        ),
        out_shape=jax.ShapeDtypeStruct((rows // SUBLANES, SUBLANES, d), F32),
        compiler_params=_params("arbitrary"),
        name="moe_dispatch",
    )(ends, padded, n_used, pos[0], pos[1], h2.reshape(t // SUBLANES, SUBLANES, d))
    return xs.reshape(rows, d)


def _experts_kernel(te_ref, nv_ref, xs_ref, wg_ref, wu_ref, wd_ref, ys_ref, wgb_ref, wub_ref, wdb_ref):
    j = pl.program_id(0)
    used = nv_ref[0]
    jc = jnp.minimum(j, used - 1)
    new_expert = jnp.logical_or(j == 0, te_ref[jc] != te_ref[jnp.maximum(jc - 1, 0)])

    @pl.when(jnp.logical_and(j < used, new_expert))
    def _():
        wgb_ref[...] = wg_ref[...].astype(BF16)
        wub_ref[...] = wu_ref[...].astype(BF16)
        wdb_ref[...] = wd_ref[...].astype(BF16)

    @pl.when(j < used)
    def _():
        xb = xs_ref[...].astype(BF16)
        gt = jnp.dot(xb, wgb_ref[...], preferred_element_type=F32)
        up = jnp.dot(xb, wub_ref[...], preferred_element_type=F32)
        a = (gt * _sigmoid(gt) * up).astype(BF16)
        ys_ref[...] = jnp.dot(a, wdb_ref[...], preferred_element_type=F32)

    @pl.when(j >= used)
    def _():
        ys_ref[...] = jnp.zeros_like(ys_ref)


def _experts(tile_expert, n_used, xs, w_gate, w_up, w_down, layer):
    rows, d = xs.shape
    f = D_FF_EXPERT
    tile = lambda j, te, nv: (jnp.minimum(j, nv[0] - 1), 0)
    wsel = lambda j, te, nv: (layer, te[jnp.minimum(j, nv[0] - 1)], 0, 0)
    return pl.pallas_call(
        _experts_kernel,
        grid_spec=pltpu.PrefetchScalarGridSpec(
            num_scalar_prefetch=2,
            grid=(rows // MOE_TILE,),
            in_specs=[pl.BlockSpec((MOE_TILE, d), tile),
                      pl.BlockSpec((None, None, d, f), wsel), pl.BlockSpec((None, None, d, f), wsel),
                      pl.BlockSpec((None, None, f, d), wsel)],
            out_specs=pl.BlockSpec((MOE_TILE, d), lambda j, te, nv: (j, 0)),
            scratch_shapes=[pltpu.VMEM((d, f), BF16), pltpu.VMEM((d, f), BF16), pltpu.VMEM((f, d), BF16)],
        ),
        out_shape=jax.ShapeDtypeStruct((rows, d), F32),
        compiler_params=_params("arbitrary"),
        name="moe_experts",
    )(tile_expert, n_used, xs, w_gate, w_up, w_down)


def _combine_kernel(final, pos0_ref, pos1_ref, nxt0_ref, nxt1_ref, ys_hbm, wcol_ref, x_ref, mod_ref, fn_ref,
                    o_ref, buf_ref, sems):
    i = pl.program_id(0)
    tc, d = x_ref.shape
    groups = tc // SUBLANES

    def gather(into, p0_ref, p1_ref):
        def issue(g, carry):
            for u in range(SUBLANES):
                for k, pos_ref in enumerate((p0_ref, p1_ref)):
                    pltpu.make_async_copy(_row(ys_hbm, pos_ref[g * SUBLANES + u]),
                                          buf_ref.at[into, k, g, pl.ds(u, 1), :], sems.at[into]).start()
            return carry

        lax.fori_loop(0, groups, issue, 0)

    def consume(slot):
        def drain(g, carry):
            for _ in range(SUBLANES * N_ASSIGN):
                pltpu.make_async_copy(_row(ys_hbm, 0), buf_ref.at[slot, 0, 0, pl.ds(0, 1), :], sems.at[slot]).wait()
            return carry

        lax.fori_loop(0, groups, drain, 0)
        w = wcol_ref[...]
        ffn = w[:, 0:1] * buf_ref[slot, 0].reshape(tc, d) + w[:, 1:2] * buf_ref[slot, 1].reshape(tc, d)
        y = x_ref[...] + mod_ref[...][5:6] * ffn
        if final:
            y = _rms(y) * fn_ref[...]
        o_ref[...] = y

    @pl.when(i == 0)
    def _():
        gather(0, pos0_ref, pos1_ref)

    for slot in range(2):
        @pl.when(i % 2 == slot)
        def _():
            @pl.when(i + 1 < pl.num_programs(0))
            def _():
                gather(1 - slot, nxt0_ref, nxt1_ref)

            consume(slot)


def _combine(pos, ys, wcol, x1, mod_l, fnorm, seq, final):
    t, d = x1.shape
    tc = DMA_TILE
    per_batch = seq // tc
    row = lambda i: (i, 0)
    last = t // tc - 1
    slot = pl.BlockSpec((tc,), lambda i: (i,), memory_space=pltpu.SMEM)
    ahead = pl.BlockSpec((tc,), lambda i: (jnp.minimum(i + 1, last),), memory_space=pltpu.SMEM)
    return pl.pallas_call(
        functools.partial(_combine_kernel, final),
        grid=(t // tc,),
        in_specs=[slot, slot, ahead, ahead,
                  pl.BlockSpec(memory_space=pl.ANY),
                  pl.BlockSpec((tc, LANES), row), pl.BlockSpec((tc, d), row),
                  pl.BlockSpec((None, 6, d), lambda i: (i // per_batch, 0, 0)),
                  pl.BlockSpec((1, d), lambda i: (0, 0))],
        out_specs=pl.BlockSpec((tc, d), row),
        out_shape=jax.ShapeDtypeStruct((t, d), F32),
        scratch_shapes=[pltpu.VMEM((2, N_ASSIGN, tc // SUBLANES, SUBLANES, d), F32),
                        pltpu.SemaphoreType.DMA((2,))],
        compiler_params=_params("arbitrary"),
        name="moe_combine",
    )(pos[0], pos[1], pos[0], pos[1], ys.reshape(ys.shape[0] // SUBLANES, SUBLANES, d), wcol, x1, mod_l, fnorm)


def _moe(h2, eidx, wcol, cnt, w_gate, w_up, w_down, layer, x1, mod_l, fnorm, seq, final):
    pos, meta = _positions(eidx, cnt)
    xs = _dispatch(meta[2], meta[3], meta[1], pos, h2)
    ys = _experts(meta[0], meta[1], xs, w_gate, w_up, w_down, layer)
    return _combine(pos, ys, wcol, x1, mod_l, fnorm, seq, final)


def _head_expand():
    m = np.zeros((LANES, ATTN_WIDTH), np.float32)
    for h in range(N_HEADS):
        m[h, h * HEAD_DIM:(h + 1) * HEAD_DIM] = 1.0
    return jnp.asarray(m, BF16)


def kernel(x, c, positions, w_in, w_out, attn_norm, hgrn_norm, lb_params, ada_w, ada_b,
           w_router, b_router, w_gate, w_up, w_down, final_norm):
    batch, seq, d = x.shape
    t = batch * seq
    c_pad = jnp.pad(c, ((0, 8 - batch), (0, 0)))
    mod = _ada_mod(c_pad, ada_w, ada_b)[:, :batch].reshape(DEPTH, batch, 6, d)
    lbs = _lower_bounds(lb_params)
    cos, sin = _rope_tables(positions)
    cos, sin = cos.reshape(t, LANES), sin.reshape(t, LANES)
    head_id = np.arange(MXU_WIDTH) // HEAD_DIM
    ones_bd = jnp.asarray(head_id[:, None] == head_id[None, :], BF16)
    em = _head_expand()
    wr_t = w_router.T
    br = b_router.reshape(N_EXPERTS, 1)
    fnorm = final_norm.reshape(1, d)

    x2d = x.reshape(t, d)
    for l in range(DEPTH):
        outs = _inproj(x2d, mod[l], w_in[l].astype(BF16), cos, sin, lbs[l:l + 1], seq)
        qs, ks, vs = outs[0:3], outs[3:6], outs[6:9]
        qh, kh, gh, ih, og = outs[9:]
        us, sts = [], []
        for p, (window, dilation) in enumerate(DILATED_PATTERNS):
            u, st = _attn_pattern(qs[p], ks[p], vs[p], window, dilation)
            us.append(u)
            sts.append(st)
        o_h = _hgrn(qh, kh, gh, ih, ones_bd, batch, seq)
        x1, h2, eidx, wcol, cnt = _outproj(us, sts, o_h, og, x2d, mod[l], attn_norm[l:l + 1], hgrn_norm[l:l + 1],
                                      w_out[l].astype(BF16), em, wr_t, br, seq)
        x2d = _moe(h2, eidx, wcol, cnt, w_gate, w_up, w_down, l, x1, mod[l], fnorm, seq,
                   final=(l == DEPTH - 1))
    return x2d.reshape(batch, seq, d)
```

```python
import functools

import numpy as np
import jax
import jax.numpy as jnp
from jax import lax
from jax.experimental import pallas as pl
from jax.experimental.pallas import tpu as pltpu

D_MODEL = 1024
DEPTH = 2
ATTN_WIDTH = 512
HGRN_WIDTH = 512
HEAD_DIM = 64
N_HEADS = 8
DILATED_PATTERNS = ((128, 1), (512, 4), (2048, 16))
ATTN_BLOCK = 128
ATTN_STEP_BLOCKS = 4
ROPE_THETA = 10000.0
N_EXPERTS = 16
N_GROUPS = 4
EXPERTS_PER_GROUP = 4
D_FF_EXPERT = 512
RMS_EPS = 1e-6
IN_COLS = 3 * ATTN_WIDTH + 4 * HGRN_WIDTH
PROJ_CHUNK = ATTN_WIDTH
assert HGRN_WIDTH == PROJ_CHUNK

LANES = 128
SUBLANES = 8
MXU_WIDTH = 256
VMEM_LIMIT_BYTES = 56 * 1024 * 1024

TOKEN_TILE = 512
N_ASSIGN = 2
MOE_TILE = 512
POS_TILE = 1024
DMA_TILE = 1024
HGRN_CHUNK = 64
HGRN_STEP_CHUNKS = 8
HGRN_SUB = 8
NEG_BIG = -1e30
LOG2E = 1.4426950408889634

F32 = jnp.float32
BF16 = jnp.bfloat16

_NT = (((1,), (1,)), ((), ()))
_TN = (((0,), (0,)), ((), ()))


def _params(*sem):
    return pltpu.CompilerParams(dimension_semantics=sem, vmem_limit_bytes=VMEM_LIMIT_BYTES)


def _sigmoid(x):
    return 0.5 * jnp.tanh(0.5 * x) + 0.5


def _rms(x):
    return x * lax.rsqrt(jnp.mean(x * x, axis=-1, keepdims=True) + RMS_EPS)


def _split3(a):
    a1 = a.astype(BF16)
    r1 = a - a1.astype(F32)
    a2 = r1.astype(BF16)
    a3 = (r1 - a2.astype(F32)).astype(BF16)
    return a1, a2, a3


def _dot_hi(a, b, dims):
    a1, a2, _ = _split3(a)
    b1, b2, _ = _split3(b)
    d = lambda p, q: lax.dot_general(p, q, dims, preferred_element_type=F32)
    return d(a1, b1) + (d(a2, b1) + d(a1, b2))


def _ada_kernel(c_ref, w_ref, b_ref, o_ref):
    c = c_ref[...]
    ca = c * _sigmoid(c)
    o_ref[...] = _dot_hi(ca, w_ref[...], (((1,), (0,)), ((), ()))) + b_ref[...]


def _ada_mod(c_pad, ada_w, ada_b):
    depth, d, n = ada_w.shape
    tn = 1536
    rows = c_pad.shape[0]
    return pl.pallas_call(
        _ada_kernel,
        grid=(depth, n // tn),
        in_specs=[
            pl.BlockSpec((rows, d), lambda l, j: (0, 0)),
            pl.BlockSpec((None, d, tn), lambda l, j: (l, 0, j)),
            pl.BlockSpec((None, 1, tn), lambda l, j: (l, 0, j)),
        ],
        out_specs=pl.BlockSpec((None, rows, tn), lambda l, j: (l, 0, j)),
        out_shape=jax.ShapeDtypeStruct((depth, rows, n), F32),
        compiler_params=_params("parallel", "parallel"),
        name="ada_mod",
    )(c_pad, ada_w, ada_b.reshape(depth, 1, n))


def _lb_kernel(p_ref, o_ref):
    p = p_ref[...]
    e = jnp.exp(p - jnp.max(p, axis=0, keepdims=True))
    sm = e / jnp.sum(e, axis=0, keepdims=True)
    run = jnp.zeros_like(sm[0:1])
    for l in range(p.shape[0]):
        run = run + sm[l:l + 1]
        o_ref[l:l + 1, :] = run - sm[0:1]


def _lower_bounds(lb_params):
    return pl.pallas_call(
        _lb_kernel,
        out_shape=jax.ShapeDtypeStruct(lb_params.shape, F32),
        name="lower_bounds",
    )(lb_params)


def _rope_kernel(pos_ref, inv_ref, cos_ref, sin_ref):
    ang = inv_ref[...] * pos_ref[...].astype(F32)
    reps = LANES // ang.shape[0]
    c = jnp.concatenate([jnp.cos(ang)] * reps, axis=0)
    s = jnp.concatenate([jnp.sin(ang)] * reps, axis=0)
    row = lax.broadcasted_iota(jnp.int32, s.shape, 0)
    first = (row % HEAD_DIM) < (HEAD_DIM // 2)
    cos_ref[...] = c.T
    sin_ref[...] = jnp.where(first, -s, s).T


def _rope_tables(positions):
    b, s = positions.shape
    ts = 1024
    half = HEAD_DIM // 2
    inv = (ROPE_THETA ** (-jnp.arange(half, dtype=F32) / half)).reshape(half, 1)
    out = jax.ShapeDtypeStruct((b, s, LANES), F32)
    return pl.pallas_call(
        _rope_kernel,
        grid=(b, s // ts),
        in_specs=[
            pl.BlockSpec((None, 1, ts), lambda i, j: (i, 0, j)),
            pl.BlockSpec((half, 1), lambda i, j: (0, 0)),
        ],
        out_specs=[pl.BlockSpec((None, ts, LANES), lambda i, j: (i, j, 0))] * 2,
        out_shape=[out, out],
        compiler_params=_params("parallel", "parallel"),
        name="rope_tables",
    )(positions.reshape(b, 1, s), inv)


def _store_streams(val, refs):
    tm = val.shape[0]
    for (_, d), ref in zip(DILATED_PATTERNS, refs):
        if d == 1:
            ref[0] = val.astype(BF16)
        else:
            sw = jnp.swapaxes(val.reshape(tm // d, d, val.shape[1]), 0, 1)
            ref[...] = sw.astype(BF16)


def _inproj_kernel(x_ref, mod_ref, w_ref, cos_ref, sin_ref, lb_ref,
                   q1_ref, q4_ref, q16_ref, k1_ref, k4_ref, k16_ref, v1_ref, v4_ref, v16_ref,
                   qh_ref, kh_ref, gh_ref, ih_ref, og_ref):
    x = x_ref[...]
    mod = mod_ref[...]
    h = _rms(x) * (1.0 + mod[1:2]) + mod[0:1]
    hb = h.astype(BF16)
    reps = ATTN_WIDTH // LANES
    cos = jnp.concatenate([cos_ref[...]] * reps, axis=1)
    sin = jnp.concatenate([sin_ref[...]] * reps, axis=1)
    lane = lax.broadcasted_iota(jnp.int32, cos.shape, 1)
    first = (lane % HEAD_DIM) < (HEAD_DIM // 2)
    half = HEAD_DIM // 2

    def proj(j):
        return jnp.dot(hb, w_ref[:, j * PROJ_CHUNK:(j + 1) * PROJ_CHUNK], preferred_element_type=F32)

    def rot(t):
        swapped = jnp.where(first, pltpu.roll(t, ATTN_WIDTH - half, 1), pltpu.roll(t, half, 1))
        return t * cos + swapped * sin

    ps = [proj(j) for j in range(IN_COLS // PROJ_CHUNK)]
    _store_streams(rot(ps[0]) * (HEAD_DIM ** -0.5 * LOG2E), (q1_ref, q4_ref, q16_ref))
    _store_streams(rot(ps[1]), (k1_ref, k4_ref, k16_ref))
    _store_streams(ps[2], (v1_ref, v4_ref, v16_ref))
    qh_ref[...] = (ps[3] * _sigmoid(ps[3])).astype(BF16)
    lb = lb_ref[...]
    f = lb + (1.0 - lb) / (1.0 + jnp.exp(-ps[4]))
    kh_ref[...] = (1.0 - f).astype(BF16)
    gh_ref[...] = jnp.log(f)
    ih_ref[...] = ps[5].astype(BF16)
    og_ref[...] = _sigmoid(ps[6]).astype(BF16)


def _stream_spec(tm, dil, per_batch, width):
    return pl.BlockSpec((None, dil, tm // dil, width), lambda i: (i // per_batch, 0, i % per_batch, 0))


def _inproj(x2d, mod_l, w_in_l, cos, sin, lb_l, seq):
    t, d = x2d.shape
    tm = TOKEN_TILE
    per_batch = seq // tm
    batch = t // seq
    row = lambda i: (i, 0)
    half_spec = pl.BlockSpec((tm, PROJ_CHUNK), row)
    bf = jax.ShapeDtypeStruct((t, PROJ_CHUNK), BF16)
    stream_specs = [_stream_spec(tm, dil, per_batch, ATTN_WIDTH) for _, dil in DILATED_PATTERNS]
    stream_shapes = [jax.ShapeDtypeStruct((batch, dil, seq // dil, ATTN_WIDTH), BF16)
                     for _, dil in DILATED_PATTERNS]
    return pl.pallas_call(
        _inproj_kernel,
        grid=(t // tm,),
        in_specs=[
            pl.BlockSpec((tm, d), row),
            pl.BlockSpec((None, 6, d), lambda i: (i // per_batch, 0, 0)),
            pl.BlockSpec((d, IN_COLS), lambda i: (0, 0)),
            pl.BlockSpec((tm, LANES), row),
            pl.BlockSpec((tm, LANES), row),
            pl.BlockSpec((1, HGRN_WIDTH), lambda i: (0, 0)),
        ],
        out_specs=stream_specs * 3 + [half_spec] * 5,
        out_shape=stream_shapes * 3 + [bf, bf, jax.ShapeDtypeStruct((t, PROJ_CHUNK), F32), bf, bf],
        compiler_params=_params("parallel"),
        name="inproj",
    )(x2d, mod_l, w_in_l, cos, sin, lb_l)


def _attn_kernel(q_ref, kc_ref, kp_ref, vc_ref, vp_ref, bias_ref, u_ref, st_ref):
    blk = ATTN_BLOCK
    n = pl.program_id(2)
    keys = jnp.concatenate([kp_ref[...], kc_ref[...]], axis=0)
    vals = jnp.concatenate([vp_ref[...], vc_ref[...]], axis=0)
    biases = [bias_ref[jnp.minimum(n, 1)]] + [bias_ref[1]] * (ATTN_STEP_BLOCKS - 1)
    lane = lax.broadcasted_iota(jnp.int32, (blk, LANES), 1)
    low = lane < HEAD_DIM
    zero = jnp.zeros((), BF16)
    ones = jnp.ones((2 * blk, LANES), BF16)
    slabs = [slice(pair * LANES, (pair + 1) * LANES) for pair in range(N_HEADS // 2)]
    work = [(j, pair) for j in range(ATTN_STEP_BLOCKS) for pair in range(N_HEADS // 2)]

    scores = []
    for j, pair in work:
        qp = q_ref[j * blk:(j + 1) * blk, slabs[pair]]
        qst = jnp.concatenate([jnp.where(low, qp, zero), jnp.where(low, zero, qp)], axis=0)
        kwin = keys[j * blk:(j + 2) * blk, slabs[pair]]
        scores.append(lax.dot_general(qst, kwin, _NT, preferred_element_type=F32) + biases[j])
    maxes, probs = [], []
    for s in scores:
        m = jnp.max(s, axis=1, keepdims=True).astype(BF16).astype(F32)
        maxes.append(m)
        probs.append(jnp.exp2((s - m).astype(BF16)))
    stats = [jnp.zeros((blk, LANES), F32) for _ in range(ATTN_STEP_BLOCKS)]
    for (j, pair), m, p in zip(work, maxes, probs):
        vwin = vals[j * blk:(j + 2) * blk, slabs[pair]]
        ul = jnp.dot(p, jnp.concatenate([vwin, ones], axis=1), preferred_element_type=F32)
        u, l = ul[:, :LANES], ul[:, LANES:]
        for hh in range(2):
            head = 2 * pair + hh
            stats[j] = jnp.where(lane == head, m[hh * blk:(hh + 1) * blk], stats[j])
            stats[j] = jnp.where(lane == N_HEADS + head, l[hh * blk:(hh + 1) * blk], stats[j])
        u_ref[j * blk:(j + 1) * blk, slabs[pair]] = jnp.where(low, u[:blk], u[blk:]).astype(BF16)
    for j in range(ATTN_STEP_BLOCKS):
        st_ref[j * blk:(j + 1) * blk, :] = stats[j]


def _band_bias(steps):
    blk = ATTN_BLOCK
    qi = np.arange(2 * blk)[:, None] % blk
    kj = np.arange(2 * blk)[None, :]
    dist = qi + blk - kj
    band = (dist >= 0) & (dist <= steps)
    first = band & (kj >= blk)
    return jnp.asarray(np.where(np.stack([first, band]), 0.0, NEG_BIG), F32)


def _attn_pattern(qs, ks, vs, window, dilation):
    batch, _, m, _ = qs.shape
    rows = ATTN_STEP_BLOCKS * ATTN_BLOCK
    nb = m // rows
    bias = _band_bias(window // dilation)
    cur = pl.BlockSpec((None, None, rows, ATTN_WIDTH), lambda b, r, n: (b, r, n, 0))
    prev = pl.BlockSpec((None, None, ATTN_BLOCK, ATTN_WIDTH),
                        lambda b, r, n: (b, r, jnp.maximum(n * ATTN_STEP_BLOCKS - 1, 0), 0))
    return pl.pallas_call(
        _attn_kernel,
        grid=(batch, dilation, nb),
        in_specs=[cur, cur, prev, cur, prev,
                  pl.BlockSpec(bias.shape, lambda b, r, n: (0, 0, 0))],
        out_specs=[cur, pl.BlockSpec((None, None, rows, LANES), lambda b, r, n: (b, r, n, 0))],
        out_shape=[jax.ShapeDtypeStruct((batch, dilation, m, ATTN_WIDTH), BF16),
                   jax.ShapeDtypeStruct((batch, dilation, m, LANES), F32)],
        compiler_params=_params("parallel", "parallel", "arbitrary"),
        name=f"attn_d{dilation}",
    )(qs, ks, ks, vs, vs, bias)


def _hgrn_kernel(q_ref, k_ref, g_ref, v_ref, ones_ref, o_ref, st_ref):
    @pl.when(pl.program_id(1) == 0)
    def _():
        st_ref[...] = jnp.zeros_like(st_ref)

    state = [st_ref[grp] for grp in range(st_ref.shape[0])]
    for chunk in range(HGRN_STEP_CHUNKS):
        state = _hgrn_chunk(chunk * HGRN_CHUNK, q_ref, k_ref, g_ref, v_ref, ones_ref, o_ref, state)
    for grp, st in enumerate(state):
        st_ref[grp] = st


def _hgrn_chunk(off, q_ref, k_ref, g_ref, v_ref, ones_ref, o_ref, state):
    c, sub, w = HGRN_CHUNK, HGRN_SUB, HGRN_WIDTH
    span = slice(off, off + c)
    q = q_ref[span, :].astype(F32)
    k = k_ref[span, :].astype(F32)
    vb = v_ref[span, :]
    v = vb.astype(F32)
    g = g_ref[span, :]

    ri = lax.broadcasted_iota(jnp.int32, (c, c), 0)
    ci = lax.broadcasted_iota(jnp.int32, (c, c), 1)
    tri = (ci <= ri).astype(BF16)
    g1, g2, g3 = _split3(g)
    cs = lambda t: jnp.dot(tri, t, preferred_element_type=F32)
    b = (cs(g1) + (cs(g2) + cs(g3))) * LOG2E
    b_last = b[c - 1:c, :]

    hw = state[0].shape[0]
    qd = (q * jnp.exp2(b)).astype(BF16)
    kt = (k * jnp.exp2(b_last - b)).astype(BF16)
    decay = jnp.exp2(b_last)
    r0i = lax.broadcasted_iota(jnp.int32, (hw, hw), 0) // HEAD_DIM
    c0i = lax.broadcasted_iota(jnp.int32, (hw, hw), 1) // HEAD_DIM
    same_head = r0i == c0i
    o_parts, new_state = [], []
    for grp, st in enumerate(state):
        sl = slice(grp * hw, (grp + 1) * hw)
        o_parts.append(lax.dot_general(qd[:, sl], st.astype(BF16), _NT, preferred_element_type=F32))
        upd = lax.dot_general(vb[:, sl], kt[:, sl], _TN, preferred_element_type=F32)
        new_state.append(st * decay[:, sl] + jnp.where(same_head, upd, 0.0))
    o_inter = jnp.concatenate(o_parts, axis=1)

    gheads = hw // HEAD_DIM
    groups = [slice(grp * hw, (grp + 1) * hw) for grp in range(w // hw)]
    hrow = lax.broadcasted_iota(jnp.int32, (gheads * sub, hw), 0) // sub
    hlane = lax.broadcasted_iota(jnp.int32, (gheads * sub, hw), 1) // HEAD_DIM
    hmask = hrow == hlane
    trow = lax.broadcasted_iota(jnp.int32, (sub, w), 0)
    ones = ones_ref[...]
    ow = ones.shape[0]

    nblk = c // sub
    ws = []
    for blk in range(nblk):
        rows = slice(blk * sub, (blk + 1) * sub)
        bi, qi, ki = b[rows], q[rows], k[rows]
        for s in range(sub):
            e = jnp.exp2(bi - bi[s:s + 1])
            ws.append(jnp.where(trow >= s, qi * (ki[s:s + 1] * e), 0.0))
    wcat = jnp.concatenate(ws, axis=0).astype(BF16)
    sc = jnp.concatenate(
        [jnp.dot(wcat[:, j * ow:(j + 1) * ow], ones, preferred_element_type=F32) for j in range(w // ow)],
        axis=1)

    inter_scores = [None]
    for blk in range(1, nblk):
        r0 = blk * sub
        rows = slice(r0, r0 + sub)
        bref = b[r0:r0 + 1]
        qs = q[rows] * jnp.exp2(b[rows] - bref)
        kp = (k[0:r0] * jnp.exp2(bref - b[0:r0])).astype(BF16)
        per_group = []
        for sl in groups:
            qexp = jnp.where(hmask, jnp.concatenate([qs[:, sl]] * gheads, axis=0), 0.0).astype(BF16)
            per_group.append(lax.dot_general(qexp, kp[:, sl], _NT, preferred_element_type=F32))
        inter_scores.append(per_group)

    for blk in range(nblk):
        r0 = blk * sub
        rows = slice(r0, r0 + sub)
        acc = o_inter[rows]
        if blk > 0:
            parts = []
            for sl, a in zip(groups, inter_scores[blk]):
                oexp = jnp.dot(a.astype(BF16), vb[0:r0, sl], preferred_element_type=F32)
                oexp = jnp.where(hmask, oexp, 0.0)
                part = oexp[0:sub]
                for hd in range(1, gheads):
                    part = part + oexp[hd * sub:(hd + 1) * sub]
                parts.append(part)
            acc = acc + jnp.concatenate(parts, axis=1)
        vi = v[rows]
        for s in range(sub):
            at = (blk * sub + s) * sub
            acc = acc + sc[at:at + sub] * vi[s:s + 1]
        o_ref[off + r0:off + r0 + sub, :] = acc
    return new_state


def _hgrn(qh, kh, gh, ih, ones_bd, batch, seq):
    c, w = HGRN_CHUNK, HGRN_WIDTH
    view = lambda a: a.reshape(batch, seq, w)
    rows = HGRN_STEP_CHUNKS * c
    blk = pl.BlockSpec((None, rows, w), lambda b, n: (b, n, 0))
    o = pl.pallas_call(
        _hgrn_kernel,
        grid=(batch, seq // rows),
        in_specs=[blk, blk, blk, blk, pl.BlockSpec(ones_bd.shape, lambda b, n: (0, 0))],
        out_specs=blk,
        out_shape=jax.ShapeDtypeStruct((batch, seq, w), F32),
        scratch_shapes=[pltpu.VMEM((w // MXU_WIDTH, MXU_WIDTH, MXU_WIDTH), F32)],
        compiler_params=_params("parallel", "arbitrary"),
        name="hgrn2",
    )(view(qh), view(kh), view(gh), view(ih), ones_bd)
    return o.reshape(batch * seq, w)


def _route(probs):
    rows = [probs[i:i + 1, :] for i in range(N_EXPERTS)]
    gsum = []
    for gidx in range(N_GROUPS):
        a, b_, c_, d_ = rows[4 * gidx:4 * gidx + 4]
        hi1, lo1 = jnp.maximum(a, b_), jnp.minimum(a, b_)
        hi2, lo2 = jnp.maximum(c_, d_), jnp.minimum(c_, d_)
        top1 = jnp.maximum(hi1, hi2)
        second = jnp.maximum(jnp.minimum(hi1, hi2), jnp.maximum(lo1, lo2))
        gsum.append(top1 + second)
    best, gi = gsum[0], jnp.zeros(gsum[0].shape, jnp.int32)
    for gidx in range(1, N_GROUPS):
        upd = gsum[gidx] > best
        best = jnp.where(upd, gsum[gidx], best)
        gi = jnp.where(upd, gidx, gi)
    vals = []
    for j in range(EXPERTS_PER_GROUP):
        vj = rows[j]
        for gidx in range(1, N_GROUPS):
            vj = jnp.where(gi == gidx, rows[4 * gidx + j], vj)
        vals.append(vj)
    v1, i1 = vals[0], jnp.zeros(gi.shape, jnp.int32)
    for j in range(1, EXPERTS_PER_GROUP):
        upd = vals[j] > v1
        v1 = jnp.where(upd, vals[j], v1)
        i1 = jnp.where(upd, j, i1)
    v2, i2 = jnp.full(v1.shape, -1.0, F32), jnp.zeros(gi.shape, jnp.int32)
    for j in range(EXPERTS_PER_GROUP):
        upd = (i1 != j) & (vals[j] > v2)
        v2 = jnp.where(upd, vals[j], v2)
        i2 = jnp.where(upd, j, i2)
    tot = v1 + v2
    base = gi * EXPERTS_PER_GROUP
    return base + i1, base + i2, v1 / tot, v2 / tot


def _outproj_kernel(u1_ref, u2_ref, u3_ref, s1_ref, s2_ref, s3_ref, o_ref, og_ref, x_ref, mod_ref,
                    an_ref, hn_ref, w_ref, em_ref, wr_ref, br_ref,
                    x1_ref, h2_ref, eidx_ref, wcol_ref, cnt_ref, uscr_ref, sscr_ref):
    us = (u1_ref, u2_ref, u3_ref)
    ss = (s1_ref, s2_ref, s3_ref)
    em = em_ref[...]

    def natural(ref):
        dil, per, width = ref.shape
        if dil == 1:
            return ref[0].astype(F32)
        scr = uscr_ref if width == ATTN_WIDTH else sscr_ref
        slabs = scr.shape[0]
        for r in range(dil):
            val = ref[r].astype(F32)
            for c in range(slabs):
                scr[c, pl.ds(r, per, stride=dil), :] = val[:, c * LANES:(c + 1) * LANES]
        return jnp.concatenate([scr[c] for c in range(slabs)], axis=1)

    stats = [natural(s_ref) for s_ref in ss]
    mmax = jnp.maximum(jnp.maximum(stats[0], stats[1]), stats[2])
    wts = [jnp.exp2(s - mmax) for s in stats]
    den = jnp.zeros_like(mmax)
    for s, wgt in zip(stats, wts):
        den = den + pltpu.roll(wgt, N_HEADS, 1) * s
    inv = 1.0 / pltpu.roll(den, LANES - N_HEADS, 1)
    head_lane = lax.broadcasted_iota(jnp.int32, mmax.shape, 1) < N_HEADS
    attn = jnp.zeros((mmax.shape[0], ATTN_WIDTH), F32)
    for p, wgt in enumerate(wts):
        share = jnp.where(head_lane, wgt * inv, 0.0).astype(BF16)
        attn = attn + jnp.dot(share, em, preferred_element_type=F32) * natural(us[p])
    mod = mod_ref[...]
    a_n = _rms(attn) * an_ref[...]
    rec = _rms(o_ref[...]) * hn_ref[...] * og_ref[...].astype(F32)
    merged = jnp.concatenate([a_n, rec], axis=1).astype(BF16)
    mix = jnp.dot(merged, w_ref[...], preferred_element_type=F32)
    x1 = x_ref[...] + mod[2:3] * mix
    x1_ref[...] = x1
    h2 = _rms(x1) * (1.0 + mod[4:5]) + mod[3:4]
    h2_ref[...] = h2

    logits = _dot_hi(wr_ref[...], h2, _NT) + br_ref[...]
    e = jnp.exp(logits - jnp.max(logits, axis=0, keepdims=True))
    probs = e / jnp.sum(e, axis=0, keepdims=True)
    e1, e2, w1, w2 = _route(probs)
    eidx_ref[...] = jnp.concatenate([e1, e2], axis=0)
    erow = lax.broadcasted_iota(jnp.int32, probs.shape, 0)
    chosen = (erow == e1).astype(F32) + (erow == e2).astype(F32)
    cnt_ref[...] = _spread(jnp.sum(chosen, axis=1, keepdims=True))
    tm = probs.shape[1]
    srow = lax.broadcasted_iota(jnp.int32, (LANES, tm), 0)
    w_t = jnp.where(srow == 0, w1, jnp.where(srow == 1, w2, 0.0))
    wcol_ref[...] = w_t.T


def _outproj(us, sts, o_h, og, x2d, mod_l, an_l, hn_l, w_out_l, em, wr_t, br, seq):
    t, d = x2d.shape
    tm = TOKEN_TILE
    per_batch = seq // tm
    row = lambda i: (i, 0)
    const = lambda i: (0, 0)
    half = pl.BlockSpec((tm, HGRN_WIDTH), row)
    stat = pl.BlockSpec((tm, LANES), row)
    full = pl.BlockSpec((tm, d), row)
    u_specs = [_stream_spec(tm, dil, per_batch, ATTN_WIDTH) for _, dil in DILATED_PATTERNS]
    s_specs = [_stream_spec(tm, dil, per_batch, LANES) for _, dil in DILATED_PATTERNS]
    return pl.pallas_call(
        _outproj_kernel,
        grid=(t // tm,),
        in_specs=u_specs + s_specs + [half, half, full,
                  pl.BlockSpec((None, 6, d), lambda i: (i // per_batch, 0, 0)),
                  pl.BlockSpec((1, ATTN_WIDTH), const), pl.BlockSpec((1, HGRN_WIDTH), const),
                  pl.BlockSpec((d, d), const),
                  pl.BlockSpec((LANES, ATTN_WIDTH), const),
                  pl.BlockSpec((N_EXPERTS, d), const), pl.BlockSpec((N_EXPERTS, 1), const)],
        out_specs=[full, full, pl.BlockSpec((N_ASSIGN, tm), lambda i: (0, i)), stat,
                   pl.BlockSpec((None, N_EXPERTS, LANES), lambda i: (i, 0, 0))],
        out_shape=[jax.ShapeDtypeStruct((t, d), F32), jax.ShapeDtypeStruct((t, d), F32),
                   jax.ShapeDtypeStruct((N_ASSIGN, t), jnp.int32), jax.ShapeDtypeStruct((t, LANES), F32),
                   jax.ShapeDtypeStruct((t // tm, N_EXPERTS, LANES), F32)],
        scratch_shapes=[pltpu.VMEM((ATTN_WIDTH // LANES, tm, LANES), F32), pltpu.VMEM((1, tm, LANES), F32)],
        compiler_params=_params("parallel"),
        name="outproj_route",
    )(*us, *sts, o_h, og, x2d, mod_l, an_l, hn_l, w_out_l, em, wr_t, br)


def _one_hots(e_ref):
    e = e_ref[...]
    erow = lax.broadcasted_iota(jnp.int32, (N_EXPERTS, e.shape[1]), 0)
    return (erow == e[0:1]).astype(F32), (erow == e[1:2]).astype(F32)


def _spread(col):
    return jnp.broadcast_to(col, (N_EXPERTS, LANES))


def _positions_kernel(e_ref, cnt_ref, pos_ref, meta_ref, carry_ref, offs_ref):
    i = pl.program_id(0)
    tp = e_ref.shape[1]
    oh0, oh1 = _one_hots(e_ref)

    @pl.when(i == 0)
    def _():
        cnt = jnp.sum(cnt_ref[...], axis=0)
        padded = jnp.floor((cnt + (MOE_TILE - 1)) * (1.0 / MOE_TILE)) * MOE_TILE
        run = jnp.zeros((1, LANES), F32)
        starts = []
        for ex in range(N_EXPERTS):
            starts.append(run)
            run = run + padded[ex:ex + 1]
        offs = jnp.concatenate(starts, axis=0)
        offs_ref[...] = offs
        carry_ref[...] = jnp.zeros_like(carry_ref)
        ends = offs + padded
        lane = lax.broadcasted_iota(jnp.int32, (N_EXPERTS, LANES), 1)
        srow = lax.broadcasted_iota(jnp.int32, (N_EXPERTS, LANES), 0)
        tile_start = (lane * MOE_TILE).astype(F32)
        tile_expert = jnp.sum((ends <= tile_start).astype(F32), axis=0, keepdims=True)
        tile_expert = jnp.minimum(tile_expert, N_EXPERTS - 1.0)
        on_diag = srow == lane
        ends_lane = jnp.sum(jnp.where(on_diag, ends, 0.0), axis=0, keepdims=True)
        pad_lane = jnp.sum(jnp.where(on_diag, padded, 0.0), axis=0, keepdims=True)
        meta = jnp.concatenate([tile_expert, run * (1.0 / MOE_TILE), ends_lane, pad_lane,
                                jnp.zeros((4, LANES), F32)], axis=0)
        meta_ref[...] = meta.astype(jnp.int32)

    r = lax.broadcasted_iota(jnp.int32, (tp, tp), 0)
    c = lax.broadcasted_iota(jnp.int32, (tp, tp), 1)
    upper = (r <= c).astype(BF16)
    oh = jnp.concatenate([oh0, oh1], axis=0).astype(BF16)
    pre = jnp.dot(oh, upper, preferred_element_type=F32)
    pre0, pre1 = pre[:N_EXPERTS], pre[N_EXPERTS:]
    tot0, tot1 = pre0[:, tp - 1:tp], pre1[:, tp - 1:tp]
    base = offs_ref[...][:, 0:1] + carry_ref[...][:, 0:1]
    p0 = jnp.sum(oh0 * (pre0 - 1.0 + base), axis=0, keepdims=True)
    p1 = jnp.sum(oh1 * (pre1 - 1.0 + (base + tot0)), axis=0, keepdims=True)
    pos_ref[...] = jnp.concatenate([p0, p1], axis=0).astype(jnp.int32)
    carry_ref[...] += _spread(tot0 + tot1)


def _positions(eidx, cnt):
    t = eidx.shape[1]
    tp = POS_TILE
    blk = pl.BlockSpec((N_ASSIGN, tp), lambda i: (0, i))
    whole = pl.BlockSpec(cnt.shape, lambda i: (0, 0, 0))
    stat = pltpu.VMEM((N_EXPERTS, LANES), F32)
    return pl.pallas_call(
        _positions_kernel,
        grid=(t // tp,),
        in_specs=[blk, whole],
        out_specs=[blk, pl.BlockSpec((8, LANES), lambda i: (0, 0))],
        out_shape=[jax.ShapeDtypeStruct((N_ASSIGN, t), jnp.int32), jax.ShapeDtypeStruct((8, LANES), jnp.int32)],
        scratch_shapes=[stat, stat],
        compiler_params=_params("arbitrary"),
        name="moe_positions",
    )(eidx, cnt)


def _row(ref, r):
    return ref.at[r >> 3, pl.ds(r & (SUBLANES - 1), 1), :]


def _dispatch_kernel(ends_ref, pad_ref, used_ref, pos0_ref, pos1_ref, h_ref, xs_hbm, zero_ref, src_ref, sem, sems):
    i = pl.program_id(0)
    groups = h_ref.shape[0]
    tile_groups = MOE_TILE // SUBLANES

    def clear_copy(start):
        first = pl.multiple_of(start // SUBLANES, tile_groups)
        return pltpu.make_async_copy(zero_ref, xs_hbm.at[pl.ds(first, tile_groups)], sem)

    @pl.when(i == 0)
    def _():
        zero_ref[...] = jnp.zeros_like(zero_ref)
        n_tiles = xs_hbm.shape[0] // tile_groups
        for act in ("start", "wait"):
            for ex in range(N_EXPERTS):
                @pl.when(pad_ref[ex] > 0)
                def _():
                    getattr(clear_copy(ends_ref[ex] - MOE_TILE), act)()

            def clear_tail(tile, carry):
                getattr(clear_copy(tile * MOE_TILE), act)()
                return carry

            lax.fori_loop(used_ref[0], n_tiles, clear_tail, 0)

    def scatter(slot):
        src_ref[slot] = h_ref[...]

        def issue(g, carry):
            for u in range(SUBLANES):
                src = src_ref.at[slot, g, pl.ds(u, 1), :]
                for pos_ref in (pos0_ref, pos1_ref):
                    pltpu.make_async_copy(src, _row(xs_hbm, pos_ref[g * SUBLANES + u]), sems.at[slot]).start()
            return carry

        lax.fori_loop(0, groups, issue, 0)

    def drain(slot):
        def wait(g, carry):
            for _ in range(SUBLANES * N_ASSIGN):
                pltpu.make_async_copy(src_ref.at[slot, 0, pl.ds(0, 1), :], _row(xs_hbm, 0), sems.at[slot]).wait()
            return carry

        lax.fori_loop(0, groups, wait, 0)

    for slot in range(2):
        @pl.when(i % 2 == slot)
        def _():
            scatter(slot)

            @pl.when(i > 0)
            def _():
                drain(1 - slot)

            @pl.when(i == pl.num_programs(0) - 1)
            def _():
                drain(slot)


def _dispatch(ends, padded, n_used, pos, h2):
    t, d = h2.shape
    tp = DMA_TILE
    rows = N_ASSIGN * t + N_EXPERTS * MOE_TILE
    slot = pl.BlockSpec((tp,), lambda i, *_: (i,), memory_space=pltpu.SMEM)
    xs = pl.pallas_call(
        _dispatch_kernel,
        grid_spec=pltpu.PrefetchScalarGridSpec(
            num_scalar_prefetch=3,
            grid=(t // tp,),
            in_specs=[slot, slot, pl.BlockSpec((tp // SUBLANES, SUBLANES, d), lambda i, *_: (i, 0, 0))],
            out_specs=pl.BlockSpec(memory_space=pl.ANY),
            scratch_shapes=[pltpu.VMEM((MOE_TILE // SUBLANES, SUBLANES, d), F32),
                            pltpu.VMEM((2, tp // SUBLANES, SUBLANES, d), F32),
                            pltpu.SemaphoreType.DMA(()), pltpu.SemaphoreType.DMA((2,))],
        ),
        out_shape=jax.ShapeDtypeStruct((rows // SUBLANES, SUBLANES, d), F32),
        compiler_params=_params("arbitrary"),
        name="moe_dispatch",
    )(ends, padded, n_used, pos[0], pos[1], h2.reshape(t // SUBLANES, SUBLANES, d))
    return xs.reshape(rows, d)


def _experts_kernel(te_ref, nv_ref, xs_ref, wg_ref, wu_ref, wd_ref, ys_ref, wgb_ref, wub_ref, wdb_ref):
    j = pl.program_id(0)
    used = nv_ref[0]
    jc = jnp.minimum(j, used - 1)
    new_expert = jnp.logical_or(j == 0, te_ref[jc] != te_ref[jnp.maximum(jc - 1, 0)])

    @pl.when(jnp.logical_and(j < used, new_expert))
    def _():
        wgb_ref[...] = wg_ref[...].astype(BF16)
        wub_ref[...] = wu_ref[...].astype(BF16)
        wdb_ref[...] = wd_ref[...].astype(BF16)

    @pl.when(j < used)
    def _():
        xb = xs_ref[...].astype(BF16)
        gt = jnp.dot(xb, wgb_ref[...], preferred_element_type=F32)
        up = jnp.dot(xb, wub_ref[...], preferred_element_type=F32)
        a = (gt * _sigmoid(gt) * up).astype(BF16)
        ys_ref[...] = jnp.dot(a, wdb_ref[...], preferred_element_type=F32)

    @pl.when(j >= used)
    def _():
        ys_ref[...] = jnp.zeros_like(ys_ref)


def _experts(tile_expert, n_used, xs, w_gate, w_up, w_down, layer):
    rows, d = xs.shape
    f = D_FF_EXPERT
    tile = lambda j, te, nv: (jnp.minimum(j, nv[0] - 1), 0)
    wsel = lambda j, te, nv: (layer, te[jnp.minimum(j, nv[0] - 1)], 0, 0)
    return pl.pallas_call(
        _experts_kernel,
        grid_spec=pltpu.PrefetchScalarGridSpec(
            num_scalar_prefetch=2,
            grid=(rows // MOE_TILE,),
            in_specs=[pl.BlockSpec((MOE_TILE, d), tile),
                      pl.BlockSpec((None, None, d, f), wsel), pl.BlockSpec((None, None, d, f), wsel),
                      pl.BlockSpec((None, None, f, d), wsel)],
            out_specs=pl.BlockSpec((MOE_TILE, d), lambda j, te, nv: (j, 0)),
            scratch_shapes=[pltpu.VMEM((d, f), BF16), pltpu.VMEM((d, f), BF16), pltpu.VMEM((f, d), BF16)],
        ),
        out_shape=jax.ShapeDtypeStruct((rows, d), F32),
        compiler_params=_params("arbitrary"),
        name="moe_experts",
    )(tile_expert, n_used, xs, w_gate, w_up, w_down)


def _combine_kernel(final, pos0_ref, pos1_ref, nxt0_ref, nxt1_ref, ys_hbm, wcol_ref, x_ref, mod_ref, fn_ref,
                    o_ref, buf_ref, sems):
    i = pl.program_id(0)
    tc, d = x_ref.shape
    groups = tc // SUBLANES

    def gather(into, p0_ref, p1_ref):
        def issue(g, carry):
            for u in range(SUBLANES):
                for k, pos_ref in enumerate((p0_ref, p1_ref)):
                    pltpu.make_async_copy(_row(ys_hbm, pos_ref[g * SUBLANES + u]),
                                          buf_ref.at[into, k, g, pl.ds(u, 1), :], sems.at[into]).start()
            return carry

        lax.fori_loop(0, groups, issue, 0)

    def consume(slot):
        def drain(g, carry):
            for _ in range(SUBLANES * N_ASSIGN):
                pltpu.make_async_copy(_row(ys_hbm, 0), buf_ref.at[slot, 0, 0, pl.ds(0, 1), :], sems.at[slot]).wait()
            return carry

        lax.fori_loop(0, groups, drain, 0)
        w = wcol_ref[...]
        ffn = w[:, 0:1] * buf_ref[slot, 0].reshape(tc, d) + w[:, 1:2] * buf_ref[slot, 1].reshape(tc, d)
        y = x_ref[...] + mod_ref[...][5:6] * ffn
        if final:
            y = _rms(y) * fn_ref[...]
        o_ref[...] = y

    @pl.when(i == 0)
    def _():
        gather(0, pos0_ref, pos1_ref)

    for slot in range(2):
        @pl.when(i % 2 == slot)
        def _():
            @pl.when(i + 1 < pl.num_programs(0))
            def _():
                gather(1 - slot, nxt0_ref, nxt1_ref)

            consume(slot)


def _combine(pos, ys, wcol, x1, mod_l, fnorm, seq, final):
    t, d = x1.shape
    tc = DMA_TILE
    per_batch = seq // tc
    row = lambda i: (i, 0)
    last = t // tc - 1
    slot = pl.BlockSpec((tc,), lambda i: (i,), memory_space=pltpu.SMEM)
    ahead = pl.BlockSpec((tc,), lambda i: (jnp.minimum(i + 1, last),), memory_space=pltpu.SMEM)
    return pl.pallas_call(
        functools.partial(_combine_kernel, final),
        grid=(t // tc,),
        in_specs=[slot, slot, ahead, ahead,
                  pl.BlockSpec(memory_space=pl.ANY),
                  pl.BlockSpec((tc, LANES), row), pl.BlockSpec((tc, d), row),
                  pl.BlockSpec((None, 6, d), lambda i: (i // per_batch, 0, 0)),
                  pl.BlockSpec((1, d), lambda i: (0, 0))],
        out_specs=pl.BlockSpec((tc, d), row),
        out_shape=jax.ShapeDtypeStruct((t, d), F32),
        scratch_shapes=[pltpu.VMEM((2, N_ASSIGN, tc // SUBLANES, SUBLANES, d), F32),
                        pltpu.SemaphoreType.DMA((2,))],
        compiler_params=_params("arbitrary"),
        name="moe_combine",
    )(pos[0], pos[1], pos[0], pos[1], ys.reshape(ys.shape[0] // SUBLANES, SUBLANES, d), wcol, x1, mod_l, fnorm)


def _moe(h2, eidx, wcol, cnt, w_gate, w_up, w_down, layer, x1, mod_l, fnorm, seq, final):
    pos, meta = _positions(eidx, cnt)
    xs = _dispatch(meta[2], meta[3], meta[1], pos, h2)
    ys = _experts(meta[0], meta[1], xs, w_gate, w_up, w_down, layer)
    return _combine(pos, ys, wcol, x1, mod_l, fnorm, seq, final)


def _head_expand():
    m = np.zeros((LANES, ATTN_WIDTH), np.float32)
    for h in range(N_HEADS):
        m[h, h * HEAD_DIM:(h + 1) * HEAD_DIM] = 1.0
    return jnp.asarray(m, BF16)


def kernel(x, c, positions, w_in, w_out, attn_norm, hgrn_norm, lb_params, ada_w, ada_b,
           w_router, b_router, w_gate, w_up, w_down, final_norm):
    batch, seq, d = x.shape
    t = batch * seq
    c_pad = jnp.pad(c, ((0, 8 - batch), (0, 0)))
    mod = _ada_mod(c_pad, ada_w, ada_b)[:, :batch].reshape(DEPTH, batch, 6, d)
    lbs = _lower_bounds(lb_params)
    cos, sin = _rope_tables(positions)
    cos, sin = cos.reshape(t, LANES), sin.reshape(t, LANES)
    head_id = np.arange(MXU_WIDTH) // HEAD_DIM
    ones_bd = jnp.asarray(head_id[:, None] == head_id[None, :], BF16)
    em = _head_expand()
    wr_t = w_router.T
    br = b_router.reshape(N_EXPERTS, 1)
    fnorm = final_norm.reshape(1, d)

    x2d = x.reshape(t, d)
    for l in range(DEPTH):
        outs = _inproj(x2d, mod[l], w_in[l].astype(BF16), cos, sin, lbs[l:l + 1], seq)
        qs, ks, vs = outs[0:3], outs[3:6], outs[6:9]
        qh, kh, gh, ih, og = outs[9:]
        us, sts = [], []
        for p, (window, dilation) in enumerate(DILATED_PATTERNS):
            u, st = _attn_pattern(qs[p], ks[p], vs[p], window, dilation)
            us.append(u)
            sts.append(st)
        o_h = _hgrn(qh, kh, gh, ih, ones_bd, batch, seq)
        x1, h2, eidx, wcol, cnt = _outproj(us, sts, o_h, og, x2d, mod[l], attn_norm[l:l + 1], hgrn_norm[l:l + 1],
                                      w_out[l].astype(BF16), em, wr_t, br, seq)
        x2d = _moe(h2, eidx, wcol, cnt, w_gate, w_up, w_down, l, x1, mod[l], fnorm, seq,
                   final=(l == DEPTH - 1))
    return x2d.reshape(batch, seq, d)
```

```python
import functools

import numpy as np
import jax
import jax.numpy as jnp
from jax import lax
from jax.experimental import pallas as pl
from jax.experimental.pallas import tpu as pltpu

D_MODEL = 1024
DEPTH = 2
ATTN_WIDTH = 512
HGRN_WIDTH = 512
HEAD_DIM = 64
N_HEADS = 8
DILATED_PATTERNS = ((128, 1), (512, 4), (2048, 16))
ATTN_BLOCK = 128
ATTN_STEP_BLOCKS = 4
ROPE_THETA = 10000.0
N_EXPERTS = 16
N_GROUPS = 4
EXPERTS_PER_GROUP = 4
D_FF_EXPERT = 512
RMS_EPS = 1e-6
IN_COLS = 3 * ATTN_WIDTH + 4 * HGRN_WIDTH
PROJ_CHUNK = ATTN_WIDTH
assert HGRN_WIDTH == PROJ_CHUNK

LANES = 128
SUBLANES = 8
MXU_WIDTH = 256
VMEM_LIMIT_BYTES = 56 * 1024 * 1024

TOKEN_TILE = 512
N_ASSIGN = 2
MOE_TILE = 512
POS_TILE = 1024
DMA_TILE = 1024
HGRN_CHUNK = 64
HGRN_STEP_CHUNKS = 8
HGRN_SUB = 8
NEG_BIG = -1e30
LOG2E = 1.4426950408889634

F32 = jnp.float32
BF16 = jnp.bfloat16

_NT = (((1,), (1,)), ((), ()))
_TN = (((0,), (0,)), ((), ()))


def _params(*sem):
    return pltpu.CompilerParams(dimension_semantics=sem, vmem_limit_bytes=VMEM_LIMIT_BYTES)


def _sigmoid(x):
    return 0.5 * jnp.tanh(0.5 * x) + 0.5


def _rms(x):
    return x * lax.rsqrt(jnp.mean(x * x, axis=-1, keepdims=True) + RMS_EPS)


def _split3(a):
    a1 = a.astype(BF16)
    r1 = a - a1.astype(F32)
    a2 = r1.astype(BF16)
    a3 = (r1 - a2.astype(F32)).astype(BF16)
    return a1, a2, a3


def _dot_hi(a, b, dims):
    a1, a2, _ = _split3(a)
    b1, b2, _ = _split3(b)
    d = lambda p, q: lax.dot_general(p, q, dims, preferred_element_type=F32)
    return d(a1, b1) + (d(a2, b1) + d(a1, b2))


def _ada_kernel(c_ref, w_ref, b_ref, o_ref):
    c = c_ref[...]
    ca = c * _sigmoid(c)
    o_ref[...] = _dot_hi(ca, w_ref[...], (((1,), (0,)), ((), ()))) + b_ref[...]


def _ada_mod(c_pad, ada_w, ada_b):
    depth, d, n = ada_w.shape
    tn = 1536
    rows = c_pad.shape[0]
    return pl.pallas_call(
        _ada_kernel,
        grid=(depth, n // tn),
        in_specs=[
            pl.BlockSpec((rows, d), lambda l, j: (0, 0)),
            pl.BlockSpec((None, d, tn), lambda l, j: (l, 0, j)),
            pl.BlockSpec((None, 1, tn), lambda l, j: (l, 0, j)),
        ],
        out_specs=pl.BlockSpec((None, rows, tn), lambda l, j: (l, 0, j)),
        out_shape=jax.ShapeDtypeStruct((depth, rows, n), F32),
        compiler_params=_params("parallel", "parallel"),
        name="ada_mod",
    )(c_pad, ada_w, ada_b.reshape(depth, 1, n))


def _lb_kernel(p_ref, o_ref):
    p = p_ref[...]
    e = jnp.exp(p - jnp.max(p, axis=0, keepdims=True))
    sm = e / jnp.sum(e, axis=0, keepdims=True)
    run = jnp.zeros_like(sm[0:1])
    for l in range(p.shape[0]):
        run = run + sm[l:l + 1]
        o_ref[l:l + 1, :] = run - sm[0:1]


def _lower_bounds(lb_params):
    return pl.pallas_call(
        _lb_kernel,
        out_shape=jax.ShapeDtypeStruct(lb_params.shape, F32),
        name="lower_bounds",
    )(lb_params)


def _rope_kernel(pos_ref, inv_ref, cos_ref, sin_ref):
    ang = inv_ref[...] * pos_ref[...].astype(F32)
    reps = LANES // ang.shape[0]
    c = jnp.concatenate([jnp.cos(ang)] * reps, axis=0)
    s = jnp.concatenate([jnp.sin(ang)] * reps, axis=0)
    row = lax.broadcasted_iota(jnp.int32, s.shape, 0)
    first = (row % HEAD_DIM) < (HEAD_DIM // 2)
    cos_ref[...] = c.T
    sin_ref[...] = jnp.where(first, -s, s).T


def _rope_tables(positions):
    b, s = positions.shape
    ts = 1024
    half = HEAD_DIM // 2
    inv = (ROPE_THETA ** (-jnp.arange(half, dtype=F32) / half)).reshape(half, 1)
    out = jax.ShapeDtypeStruct((b, s, LANES), F32)
    return pl.pallas_call(
        _rope_kernel,
        grid=(b, s // ts),
        in_specs=[
            pl.BlockSpec((None, 1, ts), lambda i, j: (i, 0, j)),
            pl.BlockSpec((half, 1), lambda i, j: (0, 0)),
        ],
        out_specs=[pl.BlockSpec((None, ts, LANES), lambda i, j: (i, j, 0))] * 2,
        out_shape=[out, out],
        compiler_params=_params("parallel", "parallel"),
        name="rope_tables",
    )(positions.reshape(b, 1, s), inv)


def _store_streams(val, refs):
    tm = val.shape[0]
    for (_, d), ref in zip(DILATED_PATTERNS, refs):
        if d == 1:
            ref[0] = val.astype(BF16)
        else:
            sw = jnp.swapaxes(val.reshape(tm // d, d, val.shape[1]), 0, 1)
            ref[...] = sw.astype(BF16)


def _inproj_kernel(x_ref, mod_ref, w_ref, cos_ref, sin_ref, lb_ref,
                   q1_ref, q4_ref, q16_ref, k1_ref, k4_ref, k16_ref, v1_ref, v4_ref, v16_ref,
                   qh_ref, kh_ref, gh_ref, ih_ref, og_ref):
    x = x_ref[...]
    mod = mod_ref[...]
    h = _rms(x) * (1.0 + mod[1:2]) + mod[0:1]
    hb = h.astype(BF16)
    reps = ATTN_WIDTH // LANES
    cos = jnp.concatenate([cos_ref[...]] * reps, axis=1)
    sin = jnp.concatenate([sin_ref[...]] * reps, axis=1)
    lane = lax.broadcasted_iota(jnp.int32, cos.shape, 1)
    first = (lane % HEAD_DIM) < (HEAD_DIM // 2)
    half = HEAD_DIM // 2

    def proj(j):
        return jnp.dot(hb, w_ref[:, j * PROJ_CHUNK:(j + 1) * PROJ_CHUNK], preferred_element_type=F32)

    def rot(t):
        swapped = jnp.where(first, pltpu.roll(t, ATTN_WIDTH - half, 1), pltpu.roll(t, half, 1))
        return t * cos + swapped * sin

    ps = [proj(j) for j in range(IN_COLS // PROJ_CHUNK)]
    _store_streams(rot(ps[0]) * (HEAD_DIM ** -0.5 * LOG2E), (q1_ref, q4_ref, q16_ref))
    _store_streams(rot(ps[1]), (k1_ref, k4_ref, k16_ref))
    _store_streams(ps[2], (v1_ref, v4_ref, v16_ref))
    qh_ref[...] = (ps[3] * _sigmoid(ps[3])).astype(BF16)
    lb = lb_ref[...]
    f = lb + (1.0 - lb) / (1.0 + jnp.exp(-ps[4]))
    kh_ref[...] = (1.0 - f).astype(BF16)
    gh_ref[...] = jnp.log(f)
    ih_ref[...] = ps[5].astype(BF16)
    og_ref[...] = _sigmoid(ps[6]).astype(BF16)


def _stream_spec(tm, dil, per_batch, width):
    return pl.BlockSpec((None, dil, tm // dil, width), lambda i: (i // per_batch, 0, i % per_batch, 0))


def _inproj(x2d, mod_l, w_in_l, cos, sin, lb_l, seq):
    t, d = x2d.shape
    tm = TOKEN_TILE
    per_batch = seq // tm
    batch = t // seq
    row = lambda i: (i, 0)
    half_spec = pl.BlockSpec((tm, PROJ_CHUNK), row)
    bf = jax.ShapeDtypeStruct((t, PROJ_CHUNK), BF16)
    stream_specs = [_stream_spec(tm, dil, per_batch, ATTN_WIDTH) for _, dil in DILATED_PATTERNS]
    stream_shapes = [jax.ShapeDtypeStruct((batch, dil, seq // dil, ATTN_WIDTH), BF16)
                     for _, dil in DILATED_PATTERNS]
    return pl.pallas_call(
        _inproj_kernel,
        grid=(t // tm,),
        in_specs=[
            pl.BlockSpec((tm, d), row),
            pl.BlockSpec((None, 6, d), lambda i: (i // per_batch, 0, 0)),
            pl.BlockSpec((d, IN_COLS), lambda i: (0, 0)),
            pl.BlockSpec((tm, LANES), row),
            pl.BlockSpec((tm, LANES), row),
            pl.BlockSpec((1, HGRN_WIDTH), lambda i: (0, 0)),
        ],
        out_specs=stream_specs * 3 + [half_spec] * 5,
        out_shape=stream_shapes * 3 + [bf, bf, jax.ShapeDtypeStruct((t, PROJ_CHUNK), F32), bf, bf],
        compiler_params=_params("parallel"),
        name="inproj",
    )(x2d, mod_l, w_in_l, cos, sin, lb_l)


def _attn_kernel(q_ref, kc_ref, kp_ref, vc_ref, vp_ref, bias_ref, u_ref, st_ref):
    blk = ATTN_BLOCK
    n = pl.program_id(2)
    keys = jnp.concatenate([kp_ref[...], kc_ref[...]], axis=0)
    vals = jnp.concatenate([vp_ref[...], vc_ref[...]], axis=0)
    biases = [bias_ref[jnp.minimum(n, 1)]] + [bias_ref[1]] * (ATTN_STEP_BLOCKS - 1)
    lane = lax.broadcasted_iota(jnp.int32, (blk, LANES), 1)
    low = lane < HEAD_DIM
    zero = jnp.zeros((), BF16)
    ones = jnp.ones((2 * blk, LANES), BF16)
    slabs = [slice(pair * LANES, (pair + 1) * LANES) for pair in range(N_HEADS // 2)]
    work = [(j, pair) for j in range(ATTN_STEP_BLOCKS) for pair in range(N_HEADS // 2)]

    scores = []
    for j, pair in work:
        qp = q_ref[j * blk:(j + 1) * blk, slabs[pair]]
        qst = jnp.concatenate([jnp.where(low, qp, zero), jnp.where(low, zero, qp)], axis=0)
        kwin = keys[j * blk:(j + 2) * blk, slabs[pair]]
        scores.append(lax.dot_general(qst, kwin, _NT, preferred_element_type=F32) + biases[j])
    maxes, probs = [], []
    for s in scores:
        m = jnp.max(s, axis=1, keepdims=True).astype(BF16).astype(F32)
        maxes.append(m)
        probs.append(jnp.exp2((s - m).astype(BF16)))
    stats = [jnp.zeros((blk, LANES), F32) for _ in range(ATTN_STEP_BLOCKS)]
    for (j, pair), m, p in zip(work, maxes, probs):
        vwin = vals[j * blk:(j + 2) * blk, slabs[pair]]
        ul = jnp.dot(p, jnp.concatenate([vwin, ones], axis=1), preferred_element_type=F32)
        u, l = ul[:, :LANES], ul[:, LANES:]
        for hh in range(2):
            head = 2 * pair + hh
            stats[j] = jnp.where(lane == head, m[hh * blk:(hh + 1) * blk], stats[j])
            stats[j] = jnp.where(lane == N_HEADS + head, l[hh * blk:(hh + 1) * blk], stats[j])
        u_ref[j * blk:(j + 1) * blk, slabs[pair]] = jnp.where(low, u[:blk], u[blk:]).astype(BF16)
    for j in range(ATTN_STEP_BLOCKS):
        st_ref[j * blk:(j + 1) * blk, :] = stats[j]


def _band_bias(steps):
    blk = ATTN_BLOCK
    qi = np.arange(2 * blk)[:, None] % blk
    kj = np.arange(2 * blk)[None, :]
    dist = qi + blk - kj
    band = (dist >= 0) & (dist <= steps)
    first = band & (kj >= blk)
    return jnp.asarray(np.where(np.stack([first, band]), 0.0, NEG_BIG), F32)


def _attn_pattern(qs, ks, vs, window, dilation):
    batch, _, m, _ = qs.shape
    rows = ATTN_STEP_BLOCKS * ATTN_BLOCK
    nb = m // rows
    bias = _band_bias(window // dilation)
    cur = pl.BlockSpec((None, None, rows, ATTN_WIDTH), lambda b, r, n: (b, r, n, 0))
    prev = pl.BlockSpec((None, None, ATTN_BLOCK, ATTN_WIDTH),
                        lambda b, r, n: (b, r, jnp.maximum(n * ATTN_STEP_BLOCKS - 1, 0), 0))
    return pl.pallas_call(
        _attn_kernel,
        grid=(batch, dilation, nb),
        in_specs=[cur, cur, prev, cur, prev,
                  pl.BlockSpec(bias.shape, lambda b, r, n: (0, 0, 0))],
        out_specs=[cur, pl.BlockSpec((None, None, rows, LANES), lambda b, r, n: (b, r, n, 0))],
        out_shape=[jax.ShapeDtypeStruct((batch, dilation, m, ATTN_WIDTH), BF16),
                   jax.ShapeDtypeStruct((batch, dilation, m, LANES), F32)],
        compiler_params=_params("parallel", "parallel", "arbitrary"),
        name=f"attn_d{dilation}",
    )(qs, ks, ks, vs, vs, bias)


def _hgrn_kernel(q_ref, k_ref, g_ref, v_ref, ones_ref, o_ref, st_ref):
    @pl.when(pl.program_id(1) == 0)
    def _():
        st_ref[...] = jnp.zeros_like(st_ref)

    state = [st_ref[grp] for grp in range(st_ref.shape[0])]
    for chunk in range(HGRN_STEP_CHUNKS):
        state = _hgrn_chunk(chunk * HGRN_CHUNK, q_ref, k_ref, g_ref, v_ref, ones_ref, o_ref, state)
    for grp, st in enumerate(state):
        st_ref[grp] = st


def _hgrn_chunk(off, q_ref, k_ref, g_ref, v_ref, ones_ref, o_ref, state):
    c, sub, w = HGRN_CHUNK, HGRN_SUB, HGRN_WIDTH
    span = slice(off, off + c)
    q = q_ref[span, :].astype(F32)
    k = k_ref[span, :].astype(F32)
    vb = v_ref[span, :]
    v = vb.astype(F32)
    g = g_ref[span, :]

    ri = lax.broadcasted_iota(jnp.int32, (c, c), 0)
    ci = lax.broadcasted_iota(jnp.int32, (c, c), 1)
    tri = (ci <= ri).astype(BF16)
    g1, g2, g3 = _split3(g)
    cs = lambda t: jnp.dot(tri, t, preferred_element_type=F32)
    b = (cs(g1) + (cs(g2) + cs(g3))) * LOG2E
    b_last = b[c - 1:c, :]

    hw = state[0].shape[0]
    qd = (q * jnp.exp2(b)).astype(BF16)
    kt = (k * jnp.exp2(b_last - b)).astype(BF16)
    decay = jnp.exp2(b_last)
    r0i = lax.broadcasted_iota(jnp.int32, (hw, hw), 0) // HEAD_DIM
    c0i = lax.broadcasted_iota(jnp.int32, (hw, hw), 1) // HEAD_DIM
    same_head = r0i == c0i
    o_parts, new_state = [], []
    for grp, st in enumerate(state):
        sl = slice(grp * hw, (grp + 1) * hw)
        o_parts.append(lax.dot_general(qd[:, sl], st.astype(BF16), _NT, preferred_element_type=F32))
        upd = lax.dot_general(vb[:, sl], kt[:, sl], _TN, preferred_element_type=F32)
        new_state.append(st * decay[:, sl] + jnp.where(same_head, upd, 0.0))
    o_inter = jnp.concatenate(o_parts, axis=1)

    gheads = hw // HEAD_DIM
    groups = [slice(grp * hw, (grp + 1) * hw) for grp in range(w // hw)]
    hrow = lax.broadcasted_iota(jnp.int32, (gheads * sub, hw), 0) // sub
    hlane = lax.broadcasted_iota(jnp.int32, (gheads * sub, hw), 1) // HEAD_DIM
    hmask = hrow == hlane
    trow = lax.broadcasted_iota(jnp.int32, (sub, w), 0)
    ones = ones_ref[...]
    ow = ones.shape[0]

    nblk = c // sub
    ws = []
    for blk in range(nblk):
        rows = slice(blk * sub, (blk + 1) * sub)
        bi, qi, ki = b[rows], q[rows], k[rows]
        for s in range(sub):
            e = jnp.exp2(bi - bi[s:s + 1])
            ws.append(jnp.where(trow >= s, qi * (ki[s:s + 1] * e), 0.0))
    wcat = jnp.concatenate(ws, axis=0).astype(BF16)
    sc = jnp.concatenate(
        [jnp.dot(wcat[:, j * ow:(j + 1) * ow], ones, preferred_element_type=F32) for j in range(w // ow)],
        axis=1)

    inter_scores = [None]
    for blk in range(1, nblk):
        r0 = blk * sub
        rows = slice(r0, r0 + sub)
        bref = b[r0:r0 + 1]
        qs = q[rows] * jnp.exp2(b[rows] - bref)
        kp = (k[0:r0] * jnp.exp2(bref - b[0:r0])).astype(BF16)
        per_group = []
        for sl in groups:
            qexp = jnp.where(hmask, jnp.concatenate([qs[:, sl]] * gheads, axis=0), 0.0).astype(BF16)
            per_group.append(lax.dot_general(qexp, kp[:, sl], _NT, preferred_element_type=F32))
        inter_scores.append(per_group)

    for blk in range(nblk):
        r0 = blk * sub
        rows = slice(r0, r0 + sub)
        acc = o_inter[rows]
        if blk > 0:
            parts = []
            for sl, a in zip(groups, inter_scores[blk]):
                oexp = jnp.dot(a.astype(BF16), vb[0:r0, sl], preferred_element_type=F32)
                oexp = jnp.where(hmask, oexp, 0.0)
                part = oexp[0:sub]
                for hd in range(1, gheads):
                    part = part + oexp[hd * sub:(hd + 1) * sub]
                parts.append(part)
            acc = acc + jnp.concatenate(parts, axis=1)
        vi = v[rows]
        for s in range(sub):
            at = (blk * sub + s) * sub
            acc = acc + sc[at:at + sub] * vi[s:s + 1]
        o_ref[off + r0:off + r0 + sub, :] = acc
    return new_state


def _hgrn(qh, kh, gh, ih, ones_bd, batch, seq):
    c, w = HGRN_CHUNK, HGRN_WIDTH
    view = lambda a: a.reshape(batch, seq, w)
    rows = HGRN_STEP_CHUNKS * c
    blk = pl.BlockSpec((None, rows, w), lambda b, n: (b, n, 0))
    o = pl.pallas_call(
        _hgrn_kernel,
        grid=(batch, seq // rows),
        in_specs=[blk, blk, blk, blk, pl.BlockSpec(ones_bd.shape, lambda b, n: (0, 0))],
        out_specs=blk,
        out_shape=jax.ShapeDtypeStruct((batch, seq, w), F32),
        scratch_shapes=[pltpu.VMEM((w // MXU_WIDTH, MXU_WIDTH, MXU_WIDTH), F32)],
        compiler_params=_params("parallel", "arbitrary"),
        name="hgrn2",
    )(view(qh), view(kh), view(gh), view(ih), ones_bd)
    return o.reshape(batch * seq, w)


def _route(probs):
    rows = [probs[i:i + 1, :] for i in range(N_EXPERTS)]
    gsum = []
    for gidx in range(N_GROUPS):
        a, b_, c_, d_ = rows[4 * gidx:4 * gidx + 4]
        hi1, lo1 = jnp.maximum(a, b_), jnp.minimum(a, b_)
        hi2, lo2 = jnp.maximum(c_, d_), jnp.minimum(c_, d_)
        top1 = jnp.maximum(hi1, hi2)
        second = jnp.maximum(jnp.minimum(hi1, hi2), jnp.maximum(lo1, lo2))
        gsum.append(top1 + second)
    best, gi = gsum[0], jnp.zeros(gsum[0].shape, jnp.int32)
    for gidx in range(1, N_GROUPS):
        upd = gsum[gidx] > best
        best = jnp.where(upd, gsum[gidx], best)
        gi = jnp.where(upd, gidx, gi)
    vals = []
    for j in range(EXPERTS_PER_GROUP):
        vj = rows[j]
        for gidx in range(1, N_GROUPS):
            vj = jnp.where(gi == gidx, rows[4 * gidx + j], vj)
        vals.append(vj)
    v1, i1 = vals[0], jnp.zeros(gi.shape, jnp.int32)
    for j in range(1, EXPERTS_PER_GROUP):
        upd = vals[j] > v1
        v1 = jnp.where(upd, vals[j], v1)
        i1 = jnp.where(upd, j, i1)
    v2, i2 = jnp.full(v1.shape, -1.0, F32), jnp.zeros(gi.shape, jnp.int32)
    for j in range(EXPERTS_PER_GROUP):
        upd = (i1 != j) & (vals[j] > v2)
        v2 = jnp.where(upd, vals[j], v2)
        i2 = jnp.where(upd, j, i2)
    tot = v1 + v2
    base = gi * EXPERTS_PER_GROUP
    return base + i1, base + i2, v1 / tot, v2 / tot


def _outproj_kernel(u1_ref, u2_ref, u3_ref, s1_ref, s2_ref, s3_ref, o_ref, og_ref, x_ref, mod_ref,
                    an_ref, hn_ref, w_ref, em_ref, wr_ref, br_ref,
                    x1_ref, h2_ref, eidx_ref, wcol_ref, cnt_ref, uscr_ref, sscr_ref):
    us = (u1_ref, u2_ref, u3_ref)
    ss = (s1_ref, s2_ref, s3_ref)
    em = em_ref[...]

    def natural(ref):
        dil, per, width = ref.shape
        if dil == 1:
            return ref[0].astype(F32)
        scr = uscr_ref if width == ATTN_WIDTH else sscr_ref
        slabs = scr.shape[0]
        for r in range(dil):
            val = ref[r].astype(F32)
            for c in range(slabs):
                scr[c, pl.ds(r, per, stride=dil), :] = val[:, c * LANES:(c + 1) * LANES]
        return jnp.concatenate([scr[c] for c in range(slabs)], axis=1)

    stats = [natural(s_ref) for s_ref in ss]
    mmax = jnp.maximum(jnp.maximum(stats[0], stats[1]), stats[2])
    wts = [jnp.exp2(s - mmax) for s in stats]
    den = jnp.zeros_like(mmax)
    for s, wgt in zip(stats, wts):
        den = den + pltpu.roll(wgt, N_HEADS, 1) * s
    inv = 1.0 / pltpu.roll(den, LANES - N_HEADS, 1)
    head_lane = lax.broadcasted_iota(jnp.int32, mmax.shape, 1) < N_HEADS
    attn = jnp.zeros((mmax.shape[0], ATTN_WIDTH), F32)
    for p, wgt in enumerate(wts):
        share = jnp.where(head_lane, wgt * inv, 0.0).astype(BF16)
        attn = attn + jnp.dot(share, em, preferred_element_type=F32) * natural(us[p])
    mod = mod_ref[...]
    a_n = _rms(attn) * an_ref[...]
    rec = _rms(o_ref[...]) * hn_ref[...] * og_ref[...].astype(F32)
    merged = jnp.concatenate([a_n, rec], axis=1).astype(BF16)
    mix = jnp.dot(merged, w_ref[...], preferred_element_type=F32)
    x1 = x_ref[...] + mod[2:3] * mix
    x1_ref[...] = x1
    h2 = _rms(x1) * (1.0 + mod[4:5]) + mod[3:4]
    h2_ref[...] = h2

    logits = _dot_hi(wr_ref[...], h2, _NT) + br_ref[...]
    e = jnp.exp(logits - jnp.max(logits, axis=0, keepdims=True))
    probs = e / jnp.sum(e, axis=0, keepdims=True)
    e1, e2, w1, w2 = _route(probs)
    eidx_ref[...] = jnp.concatenate([e1, e2], axis=0)
    erow = lax.broadcasted_iota(jnp.int32, probs.shape, 0)
    chosen = (erow == e1).astype(F32) + (erow == e2).astype(F32)
    cnt_ref[...] = _spread(jnp.sum(chosen, axis=1, keepdims=True))
    tm = probs.shape[1]
    srow = lax.broadcasted_iota(jnp.int32, (LANES, tm), 0)
    w_t = jnp.where(srow == 0, w1, jnp.where(srow == 1, w2, 0.0))
    wcol_ref[...] = w_t.T


def _outproj(us, sts, o_h, og, x2d, mod_l, an_l, hn_l, w_out_l, em, wr_t, br, seq):
    t, d = x2d.shape
    tm = TOKEN_TILE
    per_batch = seq // tm
    row = lambda i: (i, 0)
    const = lambda i: (0, 0)
    half = pl.BlockSpec((tm, HGRN_WIDTH), row)
    stat = pl.BlockSpec((tm, LANES), row)
    full = pl.BlockSpec((tm, d), row)
    u_specs = [_stream_spec(tm, dil, per_batch, ATTN_WIDTH) for _, dil in DILATED_PATTERNS]
    s_specs = [_stream_spec(tm, dil, per_batch, LANES) for _, dil in DILATED_PATTERNS]
    return pl.pallas_call(
        _outproj_kernel,
        grid=(t // tm,),
        in_specs=u_specs + s_specs + [half, half, full,
                  pl.BlockSpec((None, 6, d), lambda i: (i // per_batch, 0, 0)),
                  pl.BlockSpec((1, ATTN_WIDTH), const), pl.BlockSpec((1, HGRN_WIDTH), const),
                  pl.BlockSpec((d, d), const),
                  pl.BlockSpec((LANES, ATTN_WIDTH), const),
                  pl.BlockSpec((N_EXPERTS, d), const), pl.BlockSpec((N_EXPERTS, 1), const)],
        out_specs=[full, full, pl.BlockSpec((N_ASSIGN, tm), lambda i: (0, i)), stat,
                   pl.BlockSpec((None, N_EXPERTS, LANES), lambda i: (i, 0, 0))],
        out_shape=[jax.ShapeDtypeStruct((t, d), F32), jax.ShapeDtypeStruct((t, d), F32),
                   jax.ShapeDtypeStruct((N_ASSIGN, t), jnp.int32), jax.ShapeDtypeStruct((t, LANES), F32),
                   jax.ShapeDtypeStruct((t // tm, N_EXPERTS, LANES), F32)],
        scratch_shapes=[pltpu.VMEM((ATTN_WIDTH // LANES, tm, LANES), F32), pltpu.VMEM((1, tm, LANES), F32)],
        compiler_params=_params("parallel"),
        name="outproj_route",
    )(*us, *sts, o_h, og, x2d, mod_l, an_l, hn_l, w_out_l, em, wr_t, br)


def _one_hots(e_ref):
    e = e_ref[...]
    erow = lax.broadcasted_iota(jnp.int32, (N_EXPERTS, e.shape[1]), 0)
    return (erow == e[0:1]).astype(F32), (erow == e[1:2]).astype(F32)


def _spread(col):
    return jnp.broadcast_to(col, (N_EXPERTS, LANES))


def _positions_kernel(e_ref, cnt_ref, pos_ref, meta_ref, carry_ref, offs_ref):
    i = pl.program_id(0)
    tp = e_ref.shape[1]
    oh0, oh1 = _one_hots(e_ref)

    @pl.when(i == 0)
    def _():
        cnt = jnp.sum(cnt_ref[...], axis=0)
        padded = jnp.floor((cnt + (MOE_TILE - 1)) * (1.0 / MOE_TILE)) * MOE_TILE
        run = jnp.zeros((1, LANES), F32)
        starts = []
        for ex in range(N_EXPERTS):
            starts.append(run)
            run = run + padded[ex:ex + 1]
        offs = jnp.concatenate(starts, axis=0)
        offs_ref[...] = offs
        carry_ref[...] = jnp.zeros_like(carry_ref)
        ends = offs + padded
        lane = lax.broadcasted_iota(jnp.int32, (N_EXPERTS, LANES), 1)
        srow = lax.broadcasted_iota(jnp.int32, (N_EXPERTS, LANES), 0)
        tile_start = (lane * MOE_TILE).astype(F32)
        tile_expert = jnp.sum((ends <= tile_start).astype(F32), axis=0, keepdims=True)
        tile_expert = jnp.minimum(tile_expert, N_EXPERTS - 1.0)
        on_diag = srow == lane
        ends_lane = jnp.sum(jnp.where(on_diag, ends, 0.0), axis=0, keepdims=True)
        pad_lane = jnp.sum(jnp.where(on_diag, padded, 0.0), axis=0, keepdims=True)
        meta = jnp.concatenate([tile_expert, run * (1.0 / MOE_TILE), ends_lane, pad_lane,
                                jnp.zeros((4, LANES), F32)], axis=0)
        meta_ref[...] = meta.astype(jnp.int32)

    r = lax.broadcasted_iota(jnp.int32, (tp, tp), 0)
    c = lax.broadcasted_iota(jnp.int32, (tp, tp), 1)
    upper = (r <= c).astype(BF16)
    oh = jnp.concatenate([oh0, oh1], axis=0).astype(BF16)
    pre = jnp.dot(oh, upper, preferred_element_type=F32)
    pre0, pre1 = pre[:N_EXPERTS], pre[N_EXPERTS:]
    tot0, tot1 = pre0[:, tp - 1:tp], pre1[:, tp - 1:tp]
    base = offs_ref[...][:, 0:1] + carry_ref[...][:, 0:1]
    p0 = jnp.sum(oh0 * (pre0 - 1.0 + base), axis=0, keepdims=True)
    p1 = jnp.sum(oh1 * (pre1 - 1.0 + (base + tot0)), axis=0, keepdims=True)
    pos_ref[...] = jnp.concatenate([p0, p1], axis=0).astype(jnp.int32)
    carry_ref[...] += _spread(tot0 + tot1)


def _positions(eidx, cnt):
    t = eidx.shape[1]
    tp = POS_TILE
    blk = pl.BlockSpec((N_ASSIGN, tp), lambda i: (0, i))
    whole = pl.BlockSpec(cnt.shape, lambda i: (0, 0, 0))
    stat = pltpu.VMEM((N_EXPERTS, LANES), F32)
    return pl.pallas_call(
        _positions_kernel,
        grid=(t // tp,),
        in_specs=[blk, whole],
        out_specs=[blk, pl.BlockSpec((8, LANES), lambda i: (0, 0))],
        out_shape=[jax.ShapeDtypeStruct((N_ASSIGN, t), jnp.int32), jax.ShapeDtypeStruct((8, LANES), jnp.int32)],
        scratch_shapes=[stat, stat],
        compiler_params=_params("arbitrary"),
        name="moe_positions",
    )(eidx, cnt)


def _row(ref, r):
    return ref.at[r >> 3, pl.ds(r & (SUBLANES - 1), 1), :]


def _dispatch_kernel(ends_ref, pad_ref, used_ref, pos0_ref, pos1_ref, h_ref, xs_hbm, zero_ref, src_ref, sem, sems):
    i = pl.program_id(0)
    groups = h_ref.shape[0]
    tile_groups = MOE_TILE // SUBLANES

    def clear_copy(start):
        first = pl.multiple_of(start // SUBLANES, tile_groups)
        return pltpu.make_async_copy(zero_ref, xs_hbm.at[pl.ds(first, tile_groups)], sem)

    @pl.when(i == 0)
    def _():
        zero_ref[...] = jnp.zeros_like(zero_ref)
        n_tiles = xs_hbm.shape[0] // tile_groups
        for act in ("start", "wait"):
            for ex in range(N_EXPERTS):
                @pl.when(pad_ref[ex] > 0)
                def _():
                    getattr(clear_copy(ends_ref[ex] - MOE_TILE), act)()

            def clear_tail(tile, carry):
                getattr(clear_copy(tile * MOE_TILE), act)()
                return carry

            lax.fori_loop(used_ref[0], n_tiles, clear_tail, 0)

    def scatter(slot):
        src_ref[slot] = h_ref[...]

        def issue(g, carry):
            for u in range(SUBLANES):
                src = src_ref.at[slot, g, pl.ds(u, 1), :]
                for k, pos_ref in enumerate((pos0_ref, pos1_ref)):
                    pltpu.make_async_copy(src, _row(xs_hbm, pos_ref[g * SUBLANES + u]),
                                          sems.at[slot]).start(priority=k)
            return carry

        lax.fori_loop(0, groups, issue, 0)

    def drain(slot):
        def wait(g, carry):
            for _ in range(SUBLANES * N_ASSIGN):
                pltpu.make_async_copy(src_ref.at[slot, 0, pl.ds(0, 1), :], _row(xs_hbm, 0), sems.at[slot]).wait()
            return carry

        lax.fori_loop(0, groups, wait, 0)

    for slot in range(2):
        @pl.when(i % 2 == slot)
        def _():
            scatter(slot)

            @pl.when(i > 0)
            def _():
                drain(1 - slot)

            @pl.when(i == pl.num_programs(0) - 1)
            def _():
                drain(slot)


def _dispatch(ends, padded, n_used, pos, h2):
    t, d = h2.shape
    tp = DMA_TILE
    rows = N_ASSIGN * t + N_EXPERTS * MOE_TILE
    slot = pl.BlockSpec((tp,), lambda i, *_: (i,), memory_space=pltpu.SMEM)
    xs = pl.pallas_call(
        _dispatch_kernel,
        grid_spec=pltpu.PrefetchScalarGridSpec(
            num_scalar_prefetch=3,
            grid=(t // tp,),
            in_specs=[slot, slot, pl.BlockSpec((tp // SUBLANES, SUBLANES, d), lambda i, *_: (i, 0, 0))],
            out_specs=pl.BlockSpec(memory_space=pl.ANY),
            scratch_shapes=[pltpu.VMEM((MOE_TILE // SUBLANES, SUBLANES, d), F32),
                            pltpu.VMEM((2, tp // SUBLANES, SUBLANES, d), F32),
                            pltpu.SemaphoreType.DMA(()), pltpu.SemaphoreType.DMA((2,))],
        ),
        out_shape=jax.ShapeDtypeStruct((rows // SUBLANES, SUBLANES, d), F32),
        compiler_params=_params("arbitrary"),
        name="moe_dispatch",
    )(ends, padded, n_used, pos[0], pos[1], h2.reshape(t // SUBLANES, SUBLANES, d))
    return xs.reshape(rows, d)


def _experts_kernel(te_ref, nv_ref, xs_ref, wg_ref, wu_ref, wd_ref, ys_ref, wgb_ref, wub_ref, wdb_ref):
    j = pl.program_id(0)
    used = nv_ref[0]
    jc = jnp.minimum(j, used - 1)
    new_expert = jnp.logical_or(j == 0, te_ref[jc] != te_ref[jnp.maximum(jc - 1, 0)])

    @pl.when(jnp.logical_and(j < used, new_expert))
    def _():
        wgb_ref[...] = wg_ref[...].astype(BF16)
        wub_ref[...] = wu_ref[...].astype(BF16)
        wdb_ref[...] = wd_ref[...].astype(BF16)

    @pl.when(j < used)
    def _():
        xb = xs_ref[...].astype(BF16)
        gt = jnp.dot(xb, wgb_ref[...], preferred_element_type=F32)
        up = jnp.dot(xb, wub_ref[...], preferred_element_type=F32)
        a = (gt * _sigmoid(gt) * up).astype(BF16)
        ys_ref[...] = jnp.dot(a, wdb_ref[...], preferred_element_type=F32)

    @pl.when(j >= used)
    def _():
        ys_ref[...] = jnp.zeros_like(ys_ref)


def _experts(tile_expert, n_used, xs, w_gate, w_up, w_down, layer):
    rows, d = xs.shape
    f = D_FF_EXPERT
    tile = lambda j, te, nv: (jnp.minimum(j, nv[0] - 1), 0)
    wsel = lambda j, te, nv: (layer, te[jnp.minimum(j, nv[0] - 1)], 0, 0)
    return pl.pallas_call(
        _experts_kernel,
        grid_spec=pltpu.PrefetchScalarGridSpec(
            num_scalar_prefetch=2,
            grid=(rows // MOE_TILE,),
            in_specs=[pl.BlockSpec((MOE_TILE, d), tile),
                      pl.BlockSpec((None, None, d, f), wsel), pl.BlockSpec((None, None, d, f), wsel),
                      pl.BlockSpec((None, None, f, d), wsel)],
            out_specs=pl.BlockSpec((MOE_TILE, d), lambda j, te, nv: (j, 0)),
            scratch_shapes=[pltpu.VMEM((d, f), BF16), pltpu.VMEM((d, f), BF16), pltpu.VMEM((f, d), BF16)],
        ),
        out_shape=jax.ShapeDtypeStruct((rows, d), F32),
        compiler_params=_params("arbitrary"),
        name="moe_experts",
    )(tile_expert, n_used, xs, w_gate, w_up, w_down)


def _combine_kernel(final, pos0_ref, pos1_ref, nxt0_ref, nxt1_ref, ys_hbm, wcol_ref, x_ref, mod_ref, fn_ref,
                    o_ref, buf_ref, sems):
    i = pl.program_id(0)
    tc, d = x_ref.shape
    groups = tc // SUBLANES

    def gather(into, p0_ref, p1_ref):
        def issue(g, carry):
            for u in range(SUBLANES):
                for k, pos_ref in enumerate((p0_ref, p1_ref)):
                    pltpu.make_async_copy(_row(ys_hbm, pos_ref[g * SUBLANES + u]),
                                          buf_ref.at[into, k, g, pl.ds(u, 1), :], sems.at[into]).start(priority=k)
            return carry

        lax.fori_loop(0, groups, issue, 0)

    def consume(slot):
        def drain(g, carry):
            for _ in range(SUBLANES * N_ASSIGN):
                pltpu.make_async_copy(_row(ys_hbm, 0), buf_ref.at[slot, 0, 0, pl.ds(0, 1), :], sems.at[slot]).wait()
            return carry

        lax.fori_loop(0, groups, drain, 0)
        w = wcol_ref[...]
        ffn = w[:, 0:1] * buf_ref[slot, 0].reshape(tc, d) + w[:, 1:2] * buf_ref[slot, 1].reshape(tc, d)
        y = x_ref[...] + mod_ref[...][5:6] * ffn
        if final:
            y = _rms(y) * fn_ref[...]
        o_ref[...] = y

    @pl.when(i == 0)
    def _():
        gather(0, pos0_ref, pos1_ref)

    for slot in range(2):
        @pl.when(i % 2 == slot)
        def _():
            @pl.when(i + 1 < pl.num_programs(0))
            def _():
                gather(1 - slot, nxt0_ref, nxt1_ref)

            consume(slot)


def _combine(pos, ys, wcol, x1, mod_l, fnorm, seq, final):
    t, d = x1.shape
    tc = DMA_TILE
    per_batch = seq // tc
    row = lambda i: (i, 0)
    last = t // tc - 1
    slot = pl.BlockSpec((tc,), lambda i: (i,), memory_space=pltpu.SMEM)
    ahead = pl.BlockSpec((tc,), lambda i: (jnp.minimum(i + 1, last),), memory_space=pltpu.SMEM)
    return pl.pallas_call(
        functools.partial(_combine_kernel, final),
        grid=(t // tc,),
        in_specs=[slot, slot, ahead, ahead,
                  pl.BlockSpec(memory_space=pl.ANY),
                  pl.BlockSpec((tc, LANES), row), pl.BlockSpec((tc, d), row),
                  pl.BlockSpec((None, 6, d), lambda i: (i // per_batch, 0, 0)),
                  pl.BlockSpec((1, d), lambda i: (0, 0))],
        out_specs=pl.BlockSpec((tc, d), row),
        out_shape=jax.ShapeDtypeStruct((t, d), F32),
        scratch_shapes=[pltpu.VMEM((2, N_ASSIGN, tc // SUBLANES, SUBLANES, d), F32),
                        pltpu.SemaphoreType.DMA((2,))],
        compiler_params=_params("arbitrary"),
        name="moe_combine",
    )(pos[0], pos[1], pos[0], pos[1], ys.reshape(ys.shape[0] // SUBLANES, SUBLANES, d), wcol, x1, mod_l, fnorm)


def _moe(h2, eidx, wcol, cnt, w_gate, w_up, w_down, layer, x1, mod_l, fnorm, seq, final):
    pos, meta = _positions(eidx, cnt)
    xs = _dispatch(meta[2], meta[3], meta[1], pos, h2)
    ys = _experts(meta[0], meta[1], xs, w_gate, w_up, w_down, layer)
    return _combine(pos, ys, wcol, x1, mod_l, fnorm, seq, final)


def _head_expand():
    m = np.zeros((LANES, ATTN_WIDTH), np.float32)
    for h in range(N_HEADS):
        m[h, h * HEAD_DIM:(h + 1) * HEAD_DIM] = 1.0
    return jnp.asarray(m, BF16)


def kernel(x, c, positions, w_in, w_out, attn_norm, hgrn_norm, lb_params, ada_w, ada_b,
           w_router, b_router, w_gate, w_up, w_down, final_norm):
    batch, seq, d = x.shape
    t = batch * seq
    c_pad = jnp.pad(c, ((0, 8 - batch), (0, 0)))
    mod = _ada_mod(c_pad, ada_w, ada_b)[:, :batch].reshape(DEPTH, batch, 6, d)
    lbs = _lower_bounds(lb_params)
    cos, sin = _rope_tables(positions)
    cos, sin = cos.reshape(t, LANES), sin.reshape(t, LANES)
    head_id = np.arange(MXU_WIDTH) // HEAD_DIM
    ones_bd = jnp.asarray(head_id[:, None] == head_id[None, :], BF16)
    em = _head_expand()
    wr_t = w_router.T
    br = b_router.reshape(N_EXPERTS, 1)
    fnorm = final_norm.reshape(1, d)

    x2d = x.reshape(t, d)
    for l in range(DEPTH):
        outs = _inproj(x2d, mod[l], w_in[l].astype(BF16), cos, sin, lbs[l:l + 1], seq)
        qs, ks, vs = outs[0:3], outs[3:6], outs[6:9]
        qh, kh, gh, ih, og = outs[9:]
        us, sts = [], []
        for p, (window, dilation) in enumerate(DILATED_PATTERNS):
            u, st = _attn_pattern(qs[p], ks[p], vs[p], window, dilation)
            us.append(u)
            sts.append(st)
        o_h = _hgrn(qh, kh, gh, ih, ones_bd, batch, seq)
        x1, h2, eidx, wcol, cnt = _outproj(us, sts, o_h, og, x2d, mod[l], attn_norm[l:l + 1], hgrn_norm[l:l + 1],
                                      w_out[l].astype(BF16), em, wr_t, br, seq)
        x2d = _moe(h2, eidx, wcol, cnt, w_gate, w_up, w_down, l, x1, mod[l], fnorm, seq,
                   final=(l == DEPTH - 1))
    return x2d.reshape(batch, seq, d)
```
